```python
import jax, jax.numpy as jnp
from jax import lax
import numpy as np

D_MODEL = 1024
BATCH = 2
SEQ = 16384
DEPTH = 4

MEM_LEN = 256
MIX_WIDTH = D_MODEL
D_FF = ((8 * D_MODEL // 3 + 255) // 256) * 256
XATTN_HEADS = 4
XATTN_HEAD_DIM = D_MODEL // XATTN_HEADS
POOL_WINDOWS = (2, 4, 8, 16)
POOL_GROUPS = 4
POOL_WIDTH = MIX_WIDTH // 2
POOL_GROUP_DIM = POOL_WIDTH // POOL_GROUPS
SGU_WIDTH = MIX_WIDTH // 2
SGU_HEADS = 4
SGU_CHUNK = 128
EVEN_IN = POOL_WIDTH + 2 * SGU_WIDTH
RWKV_WIDTH = MIX_WIDTH // 2
RWKV_HEAD_DIM = 64
RWKV_HEADS = RWKV_WIDTH // RWKV_HEAD_DIM
DECAY_RANK = 64
A_RANK = 64
GATE_RANK = 128
RWKV_IN = 3 * RWKV_WIDTH + DECAY_RANK + A_RANK + GATE_RANK
RWKV_GN_EPS = 64e-5
LRU_WIDTH = MIX_WIDTH // 2
LRU_BLOCKS = 8
LRU_BLOCK_DIM = LRU_WIDTH // LRU_BLOCKS
LRU_C = 8.0
CONV_WIDTH = 4
ODD_IN = RWKV_IN + 2 * LRU_WIDTH

N_EVEN = (DEPTH + 1) // 2
N_ODD = DEPTH // 2
LN_EPS = 1e-5
DEEPNORM_ALPHA = (2 * DEPTH) ** 0.25
DEEPNORM_BETA = (8 * DEPTH) ** -0.25
MACARON_WEIGHT = 0.5

kernel_name = "hybrid_pool_sgu_rwkv7_rglru_macaron_deepnorm"


def layer_norm(x, g, b, eps=LN_EPS):
    xf = x.astype(jnp.float32)
    mu = jnp.mean(xf, -1, keepdims=True)
    var = jnp.mean(jnp.square(xf - mu), -1, keepdims=True)
    return ((xf - mu) * lax.rsqrt(var + eps)).astype(x.dtype) * g + b


def shift_right(y):
    return jnp.pad(y[:, :-1], ((0, 0), (1, 0), (0, 0)))


def swiglu_ffn(x, w_in, w_out):
    gate, up = jnp.split(x @ w_in, 2, axis=-1)
    return (jax.nn.silu(gate) * up) @ w_out


def multiscale_pool(xa, pool_w, pool_scale):
    B, S, _ = xa.shape
    xg = xa.reshape(B, S, POOL_GROUPS, POOL_GROUP_DIM)
    csum = jnp.cumsum(xg.astype(jnp.float32), axis=1)
    pos = jnp.arange(1, S + 1, dtype=jnp.float32)[None, :, None]
    means = []
    for g, win in enumerate(POOL_WINDOWS):
        c = csum[:, :, g]
        lagged = jnp.pad(c[:, :-win], ((0, 0), (win, 0), (0, 0)))
        means.append((c - lagged) / jnp.minimum(pos, win))
    pooled = jnp.stack(means, axis=2).astype(xa.dtype) - xg
    y = jnp.einsum('bsgc,gcd->bsgd', pooled, pool_w).reshape(B, S, POOL_WIDTH)
    return y * pool_scale


def spatial_gating(u, v, ln_g, ln_b, w_s, b_s):
    B, S, W = v.shape
    v = layer_norm(v, ln_g, ln_b)
    vc = v.reshape(B, S // SGU_CHUNK, SGU_CHUNK, SGU_HEADS, W // SGU_HEADS)
    ws = jnp.tril(w_s)
    mixed = jnp.einsum('hts,bnshd->bnthd', ws, vc) + b_s.T[None, None, :, :, None]
    return u * mixed.reshape(B, S, W)


def even_mixer(x, w_in, w_out, pool_w, pool_scale, sgu_ln_g, sgu_ln_b, sgu_w, sgu_b):
    h = x @ w_in
    xa, u, v = jnp.split(h, [POOL_WIDTH, POOL_WIDTH + SGU_WIDTH], axis=-1)
    ya = multiscale_pool(xa, pool_w, pool_scale)
    yb = spatial_gating(jax.nn.gelu(u), jax.nn.gelu(v), sgu_ln_g, sgu_ln_b, sgu_w, sgu_b)
    return jnp.concatenate([ya, yb], axis=-1) @ w_out


def rwkv7_step(state, inp):
    r, w, k, v, a, b = inp
    sa = jnp.einsum('bhij,bhj->bhi', state, a)
    state = (state * w[:, :, None, :] + sa[..., None] * b[:, :, None, :]
             + v[..., None] * k[:, :, None, :])
    return state, jnp.einsum('bhij,bhj->bhi', state, r)


def rwkv7_time_mix(r, k, v, wd, ad, gd, w0, w_up, a0, a_up, g_up, k_k, k_a, r_k, gn_g, gn_b):
    B, S, W = r.shape
    H, N = RWKV_HEADS, RWKV_HEAD_DIM
    f32 = jnp.float32
    heads = lambda t: t.reshape(B, S, H, N)
    logw = -jax.nn.softplus(-(w0 + jnp.tanh(wd) @ w_up)) - 0.5
    decay = jnp.exp(-jnp.exp(logw.astype(f32)))
    a = jax.nn.sigmoid(a0 + ad @ a_up)
    g = jax.nn.sigmoid(gd) @ g_up
    kk = heads(k * k_k).astype(f32)
    kk = kk * lax.rsqrt(jnp.maximum(jnp.sum(kk * kk, -1, keepdims=True), 1e-24))
    k = k * (1 + (a - 1) * k_a)
    rh, kh, vh = heads(r), heads(k), heads(v)
    seq_first = lambda t: jnp.moveaxis(t.astype(f32), 1, 0)
    xs = (seq_first(rh), seq_first(heads(decay)), seq_first(kh), seq_first(vh),
          seq_first(-kk), seq_first(kk * heads(a).astype(f32)))
    state0 = jnp.zeros((B, H, N, N), f32)
    _, y = lax.scan(rwkv7_step, state0, xs)
    y = jnp.moveaxis(y, 0, 1)
    mu = jnp.mean(y, -1, keepdims=True)
    var = jnp.mean(jnp.square(y - mu), -1, keepdims=True)
    y = ((y - mu) * lax.rsqrt(var + RWKV_GN_EPS)).reshape(B, S, W).astype(r.dtype) * gn_g + gn_b
    bonus = jnp.sum(rh * kh * r_k, -1, keepdims=True) * vh
    return (y + bonus.reshape(B, S, W)) * g


def _linear_recurrence_combine(c1, c2):
    a1, b1 = c1
    a2, b2 = c2
    return a1 * a2, a2 * b1 + b2


def rglru_branch(xr, gate, conv_w, conv_b, w_a, b_a, w_x, b_x, lam):
    B, S, W = xr.shape
    f32 = jnp.float32
    xc = lax.conv_general_dilated(xr, conv_w[:, None, :], window_strides=(1,),
                                  padding=[(CONV_WIDTH - 1, 0)],
                                  dimension_numbers=('NWC', 'WIO', 'NWC'),
                                  feature_group_count=W) + conv_b
    xb = xc.reshape(B, S, LRU_BLOCKS, LRU_BLOCK_DIM)
    rec = jax.nn.sigmoid(jnp.einsum('bshi,hij->bshj', xb, w_a).reshape(B, S, W) + b_a)
    inp = jax.nn.sigmoid(jnp.einsum('bshi,hij->bshj', xb, w_x).reshape(B, S, W) + b_x)
    log_a = -LRU_C * rec.astype(f32) * jax.nn.softplus(-lam.astype(f32))
    a = jnp.exp(log_a)
    bx = jnp.sqrt(-jnp.expm1(2 * log_a)) * (inp * xc).astype(f32)
    _, h = lax.associative_scan(_linear_recurrence_combine, (a, bx), axis=1)
    return h.astype(xr.dtype) * jax.nn.gelu(gate)


def odd_mixer(x, w_in, w_out, mu, w0, w_up, a0, a_up, g_up, k_k, k_a, r_k, gn_g, gn_b,
              conv_w, conv_b, w_a, b_a, w_x, b_x, lam):
    h = x @ w_in
    hc, hd = h[..., :RWKV_IN], h[..., RWKV_IN:]
    hc = hc + mu * (shift_right(hc) - hc)
    sizes = [RWKV_WIDTH, RWKV_WIDTH, RWKV_WIDTH, DECAY_RANK, A_RANK, GATE_RANK]
    r, k, v, wd, ad, gd = jnp.split(hc, np.cumsum(sizes)[:-1].tolist(), axis=-1)
    yc = rwkv7_time_mix(r, k, v, wd, ad, gd, w0, w_up, a0, a_up, g_up, k_k, k_a, r_k, gn_g, gn_b)
    gate, xr = jnp.split(hd, 2, axis=-1)
    yd = rglru_branch(xr, gate, conv_w, conv_b, w_a, b_a, w_x, b_x, lam)
    return jnp.concatenate([yc, yd], axis=-1) @ w_out


def memory_cross_attention(x, mem, w_q, w_kv, w_o):
    B, S, _ = x.shape
    M = mem.shape[1]
    q = (x @ w_q).reshape(B, S, XATTN_HEADS, XATTN_HEAD_DIM)
    k, v = jnp.split(mem @ w_kv, 2, axis=-1)
    k = k.reshape(B, M, XATTN_HEADS, XATTN_HEAD_DIM)
    v = v.reshape(B, M, XATTN_HEADS, XATTN_HEAD_DIM)
    s = jnp.einsum('bshd,bmhd->bhsm', q, k, preferred_element_type=jnp.float32) * (XATTN_HEAD_DIM ** -0.5)
    p = jax.nn.softmax(s, axis=-1).astype(x.dtype)
    o = jnp.einsum('bhsm,bmhd->bshd', p, v).reshape(B, S, D_MODEL)
    return o @ w_o


def setup_inputs(seed: int = 0) -> dict:
    key = jax.random.key(seed)
    keys = jax.random.split(key, 64)
    ks = iter([keys[i] for i in range(64)])

    def nrm(shape, scale):
        return jax.random.normal(next(ks), shape, jnp.float32) * scale

    def unif(shape, lo, hi):
        return jax.random.uniform(next(ks), shape, jnp.float32, lo, hi)

    L, E, O = DEPTH, N_EVEN, N_ODD
    D, F = D_MODEL, D_FF
    beta = DEEPNORM_BETA
    x = nrm((BATCH, SEQ, D), 1.0)
    mem = nrm((BATCH, MEM_LEN, D), 1.0)
    ffn1_w_in = nrm((L, D, 2 * F), D ** -0.5)
    ffn1_w_out = nrm((L, F, D), F ** -0.5 * beta)
    ffn2_w_in = nrm((L, D, 2 * F), D ** -0.5)
    ffn2_w_out = nrm((L, F, D), F ** -0.5 * beta)
    ln_g = 1.0 + nrm((L, 4, D), 0.02)
    ln_b = nrm((L, 4, D), 0.02)
    xattn_w_q = nrm((L, D, D), D ** -0.5)
    xattn_w_kv = jnp.concatenate([nrm((L, D, D), D ** -0.5), nrm((L, D, D), D ** -0.5 * beta)], axis=-1)
    xattn_w_o = nrm((L, D, D), D ** -0.5 * beta)
    even_w_in = nrm((E, D, EVEN_IN), D ** -0.5)
    even_w_out = nrm((E, MIX_WIDTH, D), MIX_WIDTH ** -0.5 * beta)
    pool_w = nrm((E, POOL_GROUPS, POOL_GROUP_DIM, POOL_GROUP_DIM), POOL_GROUP_DIM ** -0.5)
    pool_scale = 1.0 + nrm((E, POOL_WIDTH), 0.1)
    sgu_ln_g = 1.0 + nrm((E, SGU_WIDTH), 0.02)
    sgu_ln_b = nrm((E, SGU_WIDTH), 0.02)
    sgu_w = nrm((E, SGU_HEADS, SGU_CHUNK, SGU_CHUNK), 0.02)
    sgu_b = 1.0 + nrm((E, SGU_HEADS, SGU_CHUNK), 0.01)
    odd_w_in = nrm((O, D, ODD_IN), D ** -0.5)
    odd_w_out = nrm((O, MIX_WIDTH, D), MIX_WIDTH ** -0.5 * beta)
    rwkv_mu = unif((O, RWKV_IN), 0.0, 1.0)
    rwkv_w0 = unif((O, RWKV_WIDTH), -6.0, -1.0)
    rwkv_w_up = nrm((O, DECAY_RANK, RWKV_WIDTH), 0.1 * DECAY_RANK ** -0.5)
    rwkv_a0 = nrm((O, RWKV_WIDTH), 0.1)
    rwkv_a_up = nrm((O, A_RANK, RWKV_WIDTH), 0.1 * A_RANK ** -0.5)
    rwkv_g_up = nrm((O, GATE_RANK, RWKV_WIDTH), GATE_RANK ** -0.5)
    rwkv_k_k = 0.85 + nrm((O, RWKV_WIDTH), 0.02)
    rwkv_k_a = 1.0 + nrm((O, RWKV_WIDTH), 0.02)
    rwkv_r_k = nrm((O, RWKV_HEADS, RWKV_HEAD_DIM), 0.1)
    rwkv_gn_g = 1.0 + nrm((O, RWKV_WIDTH), 0.02)
    rwkv_gn_b = nrm((O, RWKV_WIDTH), 0.02)
    lru_conv_w = nrm((O, CONV_WIDTH, LRU_WIDTH), CONV_WIDTH ** -0.5)
    lru_conv_b = nrm((O, LRU_WIDTH), 0.02)
    lru_w_a = nrm((O, LRU_BLOCKS, LRU_BLOCK_DIM, LRU_BLOCK_DIM), LRU_BLOCK_DIM ** -0.5)
    lru_b_a = nrm((O, LRU_WIDTH), 0.02)
    lru_w_x = nrm((O, LRU_BLOCKS, LRU_BLOCK_DIM, LRU_BLOCK_DIM), LRU_BLOCK_DIM ** -0.5)
    lru_b_x = nrm((O, LRU_WIDTH), 0.02)
    a_pow_c = unif((O, LRU_WIDTH), 0.9, 0.999)
    lru_lambda = -jnp.log(jnp.expm1(-jnp.log(a_pow_c) / LRU_C))
    return {
        "x": x, "mem": mem,
        "ffn1_w_in": ffn1_w_in, "ffn1_w_out": ffn1_w_out,
        "ffn2_w_in": ffn2_w_in, "ffn2_w_out": ffn2_w_out,
        "ln_g": ln_g, "ln_b": ln_b,
        "xattn_w_q": xattn_w_q, "xattn_w_kv": xattn_w_kv, "xattn_w_o": xattn_w_o,
        "even_w_in": even_w_in, "even_w_out": even_w_out,
        "pool_w": pool_w, "pool_scale": pool_scale,
        "sgu_ln_g": sgu_ln_g, "sgu_ln_b": sgu_ln_b, "sgu_w": sgu_w, "sgu_b": sgu_b,
        "odd_w_in": odd_w_in, "odd_w_out": odd_w_out,
        "rwkv_mu": rwkv_mu, "rwkv_w0": rwkv_w0, "rwkv_w_up": rwkv_w_up,
        "rwkv_a0": rwkv_a0, "rwkv_a_up": rwkv_a_up, "rwkv_g_up": rwkv_g_up,
        "rwkv_k_k": rwkv_k_k, "rwkv_k_a": rwkv_k_a, "rwkv_r_k": rwkv_r_k,
        "rwkv_gn_g": rwkv_gn_g, "rwkv_gn_b": rwkv_gn_b,
        "lru_conv_w": lru_conv_w, "lru_conv_b": lru_conv_b,
        "lru_w_a": lru_w_a, "lru_b_a": lru_b_a, "lru_w_x": lru_w_x, "lru_b_x": lru_b_x,
        "lru_lambda": lru_lambda,
    }


def reference(x, mem, ffn1_w_in, ffn1_w_out, ffn2_w_in, ffn2_w_out, ln_g, ln_b,
              xattn_w_q, xattn_w_kv, xattn_w_o, even_w_in, even_w_out, pool_w, pool_scale,
              sgu_ln_g, sgu_ln_b, sgu_w, sgu_b, odd_w_in, odd_w_out, rwkv_mu, rwkv_w0,
              rwkv_w_up, rwkv_a0, rwkv_a_up, rwkv_g_up, rwkv_k_k, rwkv_k_a, rwkv_r_k,
              rwkv_gn_g, rwkv_gn_b, lru_conv_w, lru_conv_b, lru_w_a, lru_b_a, lru_w_x,
              lru_b_x, lru_lambda):
    def post_norm(h, sub, l, j):
        return layer_norm(DEEPNORM_ALPHA * h + sub, ln_g[l, j], ln_b[l, j])

    for l in range(DEPTH):
        x = post_norm(x, MACARON_WEIGHT * swiglu_ffn(x, ffn1_w_in[l], ffn1_w_out[l]), l, 0)
        e = l // 2
        if l % 2 == 0:
            mix = even_mixer(x, even_w_in[e], even_w_out[e], pool_w[e], pool_scale[e],
                             sgu_ln_g[e], sgu_ln_b[e], sgu_w[e], sgu_b[e])
        else:
            mix = odd_mixer(x, odd_w_in[e], odd_w_out[e], rwkv_mu[e], rwkv_w0[e], rwkv_w_up[e],
                            rwkv_a0[e], rwkv_a_up[e], rwkv_g_up[e], rwkv_k_k[e], rwkv_k_a[e],
                            rwkv_r_k[e], rwkv_gn_g[e], rwkv_gn_b[e], lru_conv_w[e], lru_conv_b[e],
                            lru_w_a[e], lru_b_a[e], lru_w_x[e], lru_b_x[e], lru_lambda[e])
        x = post_norm(x, mix, l, 1)
        x = post_norm(x, memory_cross_attention(x, mem, xattn_w_q[l], xattn_w_kv[l], xattn_w_o[l]), l, 2)
        x = post_norm(x, MACARON_WEIGHT * swiglu_ffn(x, ffn2_w_in[l], ffn2_w_out[l]), l, 3)
    return x
```

```python
import functools

import jax
import jax.numpy as jnp
from jax import lax
from jax.experimental import pallas as pl
from jax.experimental.pallas import tpu as pltpu

F32 = jnp.float32
BF16 = jnp.bfloat16

DEPTH = 4
LN_EPS = 1e-5
DEEPNORM_ALPHA = (2 * DEPTH) ** 0.25
MACARON_WEIGHT = 0.5
POOL_WINDOWS = (2, 4, 8, 16)
POOL_HALO = 16
SGU_CHUNK = 128
XATTN_HEADS = 4
RWKV_HEAD_DIM = 64
RWKV_CHUNK = 64
RWKV_GN_EPS = 64e-5
LRU_C = 8.0
CONV_WIDTH = 4
CONV_HALO = 8

ROW_TILE = 512
FFN_CHUNK = 256
VMEM_LIMIT = 56 * 1024 * 1024


def _cparams():
    return pltpu.CompilerParams(dimension_semantics=("arbitrary", "arbitrary"),
                                vmem_limit_bytes=VMEM_LIMIT)


def _dot(a, b):
    return jnp.dot(a.astype(BF16), b.astype(BF16), preferred_element_type=F32)


def _split2(x):
    hi = x.astype(BF16)
    lo = (x - hi.astype(F32)).astype(BF16)
    return hi, lo


_NN = (((1,), (0,)), ((), ()))
_NT = (((1,), (1,)), ((), ()))
_TN = (((0,), (0,)), ((), ()))


def _mm3(a, b, dn=_NN):
    ah, al = _split2(a)
    bh, bl = _split2(b)
    f = lambda p, q: lax.dot_general(p, q, dn, preferred_element_type=F32)
    return f(ah, bh) + (f(ah, bl) + f(al, bh))


def _head_sum(x, ones_bd):
    p1 = x.astype(BF16)
    r1 = x - p1.astype(F32)
    p2 = r1.astype(BF16)
    p3 = (r1 - p2.astype(F32)).astype(BF16)
    f = lambda p: jnp.dot(p, ones_bd, preferred_element_type=F32)
    return f(p1) + (f(p2) + f(p3))


def _layer_norm(y, g, b, eps=LN_EPS):
    mu = jnp.mean(y, axis=-1, keepdims=True)
    yc = y - mu
    var = jnp.mean(yc * yc, axis=-1, keepdims=True)
    return yc * lax.rsqrt(var + eps) * g + b


def _softplus(t):
    return jnp.maximum(t, 0.0) + jnp.log(1.0 + jnp.exp(-jnp.abs(t)))


def _row_ids(rows, cols):
    return lax.broadcasted_iota(jnp.int32, (rows, cols), 0)


def _ffn_kernel(x_ref, win_ref, wout_ref, g_ref, b_ref, o_ref, *, d_ff):
    x = x_ref[...]
    xb = x.astype(BF16)
    acc = jnp.zeros(x.shape, F32)
    for c in range(d_ff // FFN_CHUNK):
        lo = c * FFN_CHUNK
        gate = jnp.dot(xb, win_ref[:, lo:lo + FFN_CHUNK], preferred_element_type=F32)
        up = jnp.dot(xb, win_ref[:, d_ff + lo:d_ff + lo + FFN_CHUNK], preferred_element_type=F32)
        h = (gate * jax.nn.sigmoid(gate) * up).astype(BF16)
        acc = acc + jnp.dot(h, wout_ref[lo:lo + FFN_CHUNK, :], preferred_element_type=F32)
    y = DEEPNORM_ALPHA * x + MACARON_WEIGHT * acc
    o_ref[...] = _layer_norm(y, g_ref[...], b_ref[...])


def _ffn_layer(xf, w_in, w_out, ln_g, ln_b, l, j, nb, ns):
    m, d = xf.shape
    d_ff = w_out.shape[1]
    return pl.pallas_call(
        functools.partial(_ffn_kernel, d_ff=d_ff),
        grid=(nb, ns),
        in_specs=[
            pl.BlockSpec((ROW_TILE, d), lambda b, s: (b * ns + s, 0)),
            pl.BlockSpec((None, d, 2 * d_ff), lambda b, s: (l, 0, 0)),
            pl.BlockSpec((None, d_ff, d), lambda b, s: (l, 0, 0)),
            pl.BlockSpec((None, 1, d), lambda b, s: (4 * l + j, 0, 0)),
            pl.BlockSpec((None, 1, d), lambda b, s: (4 * l + j, 0, 0)),
        ],
        out_specs=pl.BlockSpec((ROW_TILE, d), lambda b, s: (b * ns + s, 0)),
        out_shape=jax.ShapeDtypeStruct((m, d), F32),
        compiler_params=_cparams(),
        name="ffn",
    )(xf, w_in, w_out, ln_g, ln_b)


def _even_kernel(x_ref, win_ref, wout_ref, poolw_ref, pscale_ref, sg_ref, sb_ref, sw_ref, sbt_ref,
                 g_ref, b_ref, o_ref, ext_ref):
    s = pl.program_id(1)
    tm = x_ref.shape[0]
    x = x_ref[...]
    h = _dot(x, win_ref[...])
    pw = poolw_ref.shape[-1]
    width = pw * len(POOL_WINDOWS)
    xa = h[:, :width]

    @pl.when(s == 0)
    def _():
        ext_ref[0:POOL_HALO, :] = jnp.zeros((POOL_HALO, width), F32)

    ext_ref[POOL_HALO:, :] = xa
    ext = ext_ref[...]
    ext_ref[0:POOL_HALO, :] = ext[tm:tm + POOL_HALO, :]
    pos = (s * tm + 1 + _row_ids(tm, 1)).astype(F32)
    ya = []
    for g, win in enumerate(POOL_WINDOWS):
        acc = ext[:, g * pw:(g + 1) * pw]
        d = 1
        while d < win:
            acc = acc + pltpu.roll(acc, d, axis=0)
            d *= 2
        mean = acc[POOL_HALO:, :] / jnp.minimum(pos, float(win))
        pooled = mean - xa[:, g * pw:(g + 1) * pw]
        ya.append(_dot(pooled, poolw_ref[g]))
    ya = jnp.concatenate(ya, axis=-1) * pscale_ref[...]

    u = jax.nn.gelu(h[:, width:2 * width])
    v = jax.nn.gelu(h[:, 2 * width:3 * width])
    v = _layer_norm(v, sg_ref[...], sb_ref[...])
    heads = sw_ref.shape[0]
    hw = width // heads
    ti = lax.broadcasted_iota(jnp.int32, (SGU_CHUNK, SGU_CHUNK), 0)
    si = lax.broadcasted_iota(jnp.int32, (SGU_CHUNK, SGU_CHUNK), 1)
    causal = ti >= si
    cols = []
    for hd in range(heads):
        w = jnp.where(causal, sw_ref[hd], 0.0).astype(BF16)
        bias = sbt_ref[:, hd:hd + 1]
        rows = []
        for c in range(tm // SGU_CHUNK):
            vv = v[c * SGU_CHUNK:(c + 1) * SGU_CHUNK, hd * hw:(hd + 1) * hw].astype(BF16)
            rows.append(jnp.dot(w, vv, preferred_element_type=F32) + bias)
        cols.append(jnp.concatenate(rows, axis=0))
    yb = u * jnp.concatenate(cols, axis=-1)

    out = _dot(ya, wout_ref[0:width, :]) + _dot(yb, wout_ref[width:2 * width, :])
    o_ref[...] = _layer_norm(DEEPNORM_ALPHA * x + out, g_ref[...], b_ref[...])


def _even_layer(xf, w_in, w_out, pool_w, pool_scale, sgu_g, sgu_b, sgu_w, sgu_bt, ln_g, ln_b, e, l, nb, ns):
    m, d = xf.shape
    n_in = w_in.shape[-1]
    width = pool_scale.shape[-1]
    full = lambda shape: pl.BlockSpec((None,) + shape, lambda b, s: (e,) + (0,) * len(shape))
    return pl.pallas_call(
        _even_kernel,
        grid=(nb, ns),
        in_specs=[
            pl.BlockSpec((ROW_TILE, d), lambda b, s: (b * ns + s, 0)),
            full((d, n_in)),
            full((2 * width, d)),
            full(pool_w.shape[1:]),
            full((1, width)),
            full((1, width)),
            full((1, width)),
            full(sgu_w.shape[1:]),
            full(sgu_bt.shape[1:]),
            pl.BlockSpec((None, 1, d), lambda b, s: (4 * l + 1, 0, 0)),
            pl.BlockSpec((None, 1, d), lambda b, s: (4 * l + 1, 0, 0)),
        ],
        out_specs=pl.BlockSpec((ROW_TILE, d), lambda b, s: (b * ns + s, 0)),
        out_shape=jax.ShapeDtypeStruct((m, d), F32),
        scratch_shapes=[pltpu.VMEM((POOL_HALO + ROW_TILE, width), F32)],
        compiler_params=_cparams(),
        name="even_mixer",
    )(xf, w_in, w_out, pool_w, pool_scale, sgu_g, sgu_b, sgu_w, sgu_bt, ln_g, ln_b)


def _odd_pre_kernel(x_ref, win_ref, mu_ref, w0_ref, wup_ref, a0_ref, aup_ref, gup_ref, kk_ref, ka_ref,
                    rk_ref, cw_ref, cb_ref, wa_ref, ba_ref, wx_ref, bx_ref, lam_ref, ones_ref,
                    r_o, lw_o, k_o, v_o, a_o, b_o, g_o, bonus_o, yd_o,
                    hprev_ref, xr_ref, hlru_ref):
    s = pl.program_id(1)
    tm = x_ref.shape[0]
    w = w0_ref.shape[-1]
    n_c = mu_ref.shape[-1]
    x = x_ref[...]
    h = _dot(x, win_ref[...])

    @pl.when(s == 0)
    def _():
        hprev_ref[...] = jnp.zeros(hprev_ref.shape, F32)
        xr_ref[0:CONV_HALO, :] = jnp.zeros((CONV_HALO, w), F32)
        hlru_ref[...] = jnp.zeros(hlru_ref.shape, F32)

    hc = h[:, :n_c]
    first = _row_ids(tm, 1) == 0
    prev = jnp.where(first, hprev_ref[...], pltpu.roll(hc, 1, axis=0))
    hprev_ref[...] = hc[tm - 1:tm, :]
    hc = hc + mu_ref[...] * (prev - hc)
    r = hc[:, 0:w]
    k = hc[:, w:2 * w]
    v = hc[:, 2 * w:3 * w]
    rest = hc[:, 3 * w:]
    dr = wup_ref.shape[0]
    ar = aup_ref.shape[0]
    wd = rest[:, 0:dr]
    ad = rest[:, dr:dr + ar]
    gd = rest[:, dr + ar:]

    z = w0_ref[...] + _dot(jnp.tanh(wd), wup_ref[...])
    logw = -_softplus(-z) - 0.5
    lw_o[...] = -jnp.exp(logw)
    asig = jax.nn.sigmoid(a0_ref[...] + _dot(ad, aup_ref[...]))
    g_o[...] = _dot(jax.nn.sigmoid(gd), gup_ref[...])
    ones_bd = ones_ref[...]
    kk = k * kk_ref[...]
    kk = kk * lax.rsqrt(jnp.maximum(_head_sum(kk * kk, ones_bd), 1e-24))
    k2 = k * (1.0 + (asig - 1.0) * ka_ref[...])
    r_o[...] = r
    k_o[...] = k2
    v_o[...] = v
    a_o[...] = -kk
    b_o[...] = kk * asig
    bonus_o[...] = _head_sum(r * k2 * rk_ref[...], ones_bd) * v

    gate = h[:, n_c:n_c + w]
    xr = h[:, n_c + w:]
    xr_ref[CONV_HALO:, :] = xr
    ext = xr_ref[...]
    xr_ref[0:CONV_HALO, :] = ext[tm:tm + CONV_HALO, :]
    xc = cw_ref[CONV_WIDTH - 1:CONV_WIDTH, :] * xr + cb_ref[...]
    for j in range(1, CONV_WIDTH):
        xc = xc + cw_ref[CONV_WIDTH - 1 - j:CONV_WIDTH - j, :] * pltpu.roll(ext, j, axis=0)[CONV_HALO:, :]
    rec = jax.nn.sigmoid(_dot(xc, wa_ref[...]) + ba_ref[...])
    inp = jax.nn.sigmoid(_dot(xc, wx_ref[...]) + bx_ref[...])
    log_a = -LRU_C * rec * _softplus(-lam_ref[...])
    a = jnp.exp(log_a)
    bx = jnp.sqrt(-jnp.tanh(log_a) * (a * a + 1.0)) * (inp * xc)
    rows = _row_ids(tm, w)
    d = 1
    while d < tm:
        keep = rows >= d
        a_sh = jnp.where(keep, pltpu.roll(a, d, axis=0), 1.0)
        b_sh = jnp.where(keep, pltpu.roll(bx, d, axis=0), 0.0)
        bx = a * b_sh + bx
        a = a * a_sh
        d *= 2
    hl = a * hlru_ref[...] + bx
    hlru_ref[...] = hl[tm - 1:tm, :]
    yd_o[...] = hl * jax.nn.gelu(gate)


def _odd_pre(xf, w_in, mu, w0, w_up, a0, a_up, g_up, k_k, k_a, r_k, conv_w, conv_b, wa_bd, b_a, wx_bd, b_x,
             lam, ones_bd, e, nb, ns):
    m, d = xf.shape
    w = w0.shape[-1]
    full = lambda arr: pl.BlockSpec((None,) + arr.shape[1:], lambda b, s: (e,) + (0,) * (arr.ndim - 1))
    row = pl.BlockSpec((ROW_TILE, w), lambda b, s: (b * ns + s, 0))
    params = (w_in, mu, w0, w_up, a0, a_up, g_up, k_k, k_a, r_k, conv_w, conv_b, wa_bd, b_a, wx_bd, b_x, lam)
    return pl.pallas_call(
        _odd_pre_kernel,
        grid=(nb, ns),
        in_specs=[pl.BlockSpec((ROW_TILE, d), lambda b, s: (b * ns + s, 0))]
        + [full(p) for p in params]
        + [pl.BlockSpec(ones_bd.shape, lambda b, s: (0, 0))],
        out_specs=[row] * 9,
        out_shape=[jax.ShapeDtypeStruct((m, w), F32)] * 9,
        scratch_shapes=[pltpu.VMEM((1, mu.shape[-1]), F32),
                        pltpu.VMEM((CONV_HALO + ROW_TILE, w), F32),
                        pltpu.VMEM((1, w), F32)],
        compiler_params=_cparams(),
        name="odd_pre",
    )(xf, *params, ones_bd)


def _rwkv_kernel(r_ref, lw_ref, k_ref, v_ref, a_ref, b_ref, y_ref, t_ref):
    s = pl.program_id(1)
    tm, w = r_ref.shape
    n = RWKV_HEAD_DIM
    lc = RWKV_CHUNK
    heads = w // n

    @pl.when(s == 0)
    def _():
        t_ref[...] = jnp.zeros(t_ref.shape, F32)

    ti = lax.broadcasted_iota(jnp.int32, (lc, lc), 0)
    si = lax.broadcasted_iota(jnp.int32, (lc, lc), 1)
    strict = ti > si
    incl = ti >= si
    diag = ti == si
    eye = diag.astype(F32)
    rows = _row_ids(lc, w)

    def chunk(c, carry):
        sl = pl.ds(pl.multiple_of(c * lc, lc), lc)
        lw = lw_ref[sl, :]
        cum = lw
        d = 1
        while d < lc:
            cum = cum + jnp.where(rows >= d, pltpu.roll(cum, d, axis=0), 0.0)
            d *= 2
        cl = cum[lc - 1:lc, :]
        einv = jnp.exp(-cum)
        eend = jnp.exp(cl - cum)
        a = a_ref[sl, :]
        b = b_ref[sl, :]
        k = k_ref[sl, :]
        v_all = v_ref[sl, :]
        at_all = a * jnp.exp(cum - lw)
        rt_all = r_ref[sl, :] * jnp.exp(cum)
        bt_all = b * einv
        kt_all = k * einv
        bh_all = b * eend
        kh_all = k * eend
        pl_all = jnp.exp(cl)
        ys = []
        for hd in range(heads):
            hs = slice(hd * n, (hd + 1) * n)
            at, rt, bt, kt, bh, kh, v = (t[:, hs] for t in (at_all, rt_all, bt_all, kt_all, bh_all, kh_all, v_all))
            s1 = _mm3(jnp.concatenate([at, rt], axis=0), jnp.concatenate([bt, kt], axis=0), _NT)
            a_ab = jnp.where(strict, s1[:lc, :lc], 0.0)
            a_ak = jnp.where(strict, s1[:lc, lc:], 0.0)
            m_rb = jnp.where(incl, s1[lc:, :lc], 0.0)
            m_rk = jnp.where(incl, s1[lc:, lc:], 0.0)
            inv = eye + a_ab
            apow = a_ab
            span = 2
            while span < lc:
                apow = _mm3(apow, apow)
                inv = inv + _mm3(inv, apow)
                span *= 2
            x0 = _mm3(a_ak, v)
            av = _mm3(inv, jnp.concatenate([at, x0], axis=1))
            ap, vp = av[:, :n], av[:, n:]
            g = _mm3(bh, ap, _TN) + jnp.where(diag, jnp.broadcast_to(pl_all[:, hs], (n, n)), 0.0)
            hh = _mm3(bh, vp, _TN) + _mm3(kh, v, _TN)
            q = rt + _mm3(m_rb, ap)
            yl = _mm3(m_rb, vp) + _mm3(m_rk, v)
            t0 = t_ref[hd]
            ys.append(_mm3(q, t0) + yl)
            t_ref[hd] = _mm3(g, t0) + hh
        y_ref[sl, :] = jnp.concatenate(ys, axis=-1)
        return carry

    lax.fori_loop(0, tm // lc, chunk, 0)


def _rwkv_scan(r, lw, k, v, a, b, nb, ns):
    m, w = r.shape
    row = pl.BlockSpec((ROW_TILE, w), lambda bb, s: (bb * ns + s, 0))
    heads = w // RWKV_HEAD_DIM
    return pl.pallas_call(
        _rwkv_kernel,
        grid=(nb, ns),
        in_specs=[row] * 6,
        out_specs=row,
        out_shape=jax.ShapeDtypeStruct((m, w), F32),
        scratch_shapes=[pltpu.VMEM((heads, RWKV_HEAD_DIM, RWKV_HEAD_DIM), F32)],
        compiler_params=_cparams(),
        name="rwkv7",
    )(r, lw, k, v, a, b)


def _odd_post_kernel(x_ref, y_ref, g_ref, bonus_ref, yd_ref, gng_ref, gnb_ref, ones_ref, wout_ref,
                     lg_ref, lb_ref, o_ref):
    x = x_ref[...]
    y = y_ref[...]
    w = y.shape[-1]
    ones_bd = ones_ref[...]
    inv_n = 1.0 / RWKV_HEAD_DIM
    mu = _head_sum(y, ones_bd) * inv_n
    yc = y - mu
    var = _head_sum(yc * yc, ones_bd) * inv_n
    yn = yc * lax.rsqrt(var + RWKV_GN_EPS) * gng_ref[...] + gnb_ref[...]
    y_rwkv = (yn + bonus_ref[...]) * g_ref[...]
    out = _dot(y_rwkv, wout_ref[0:w, :]) + _dot(yd_ref[...], wout_ref[w:2 * w, :])
    o_ref[...] = _layer_norm(DEEPNORM_ALPHA * x + out, lg_ref[...], lb_ref[...])


def _odd_post(xf, y, g, bonus, yd, gn_g, gn_b, ones_bd, w_out, ln_g, ln_b, e, l, nb, ns):
    m, d = xf.shape
    w = y.shape[-1]
    row = pl.BlockSpec((ROW_TILE, w), lambda b, s: (b * ns + s, 0))
    xrow = pl.BlockSpec((ROW_TILE, d), lambda b, s: (b * ns + s, 0))
    return pl.pallas_call(
        _odd_post_kernel,
        grid=(nb, ns),
        in_specs=[xrow, row, row, row, row,
                  pl.BlockSpec((None, 1, w), lambda b, s: (e, 0, 0)),
                  pl.BlockSpec((None, 1, w), lambda b, s: (e, 0, 0)),
                  pl.BlockSpec(ones_bd.shape, lambda b, s: (0, 0)),
                  pl.BlockSpec((None, 2 * w, d), lambda b, s: (e, 0, 0)),
                  pl.BlockSpec((None, 1, d), lambda b, s: (4 * l + 1, 0, 0)),
                  pl.BlockSpec((None, 1, d), lambda b, s: (4 * l + 1, 0, 0))],
        out_specs=xrow,
        out_shape=jax.ShapeDtypeStruct((m, d), F32),
        compiler_params=_cparams(),
        name="odd_post",
    )(xf, y, g, bonus, yd, gn_g, gn_b, ones_bd, w_out, ln_g, ln_b)


def _kv_kernel(mem_ref, wkv_ref, kv_ref):
    kv_ref[...] = _dot(mem_ref[...], wkv_ref[...]).astype(BF16)


def _kv_proj(mem, w_kv):
    nb, mlen, d = mem.shape
    nl, _, n2 = w_kv.shape
    return pl.pallas_call(
        _kv_kernel,
        grid=(nl, nb),
        in_specs=[pl.BlockSpec((None, mlen, d), lambda l, b: (b, 0, 0)),
                  pl.BlockSpec((None, d, n2), lambda l, b: (l, 0, 0))],
        out_specs=pl.BlockSpec((None, None, mlen, n2), lambda l, b: (l, b, 0, 0)),
        out_shape=jax.ShapeDtypeStruct((nl, nb, mlen, n2), BF16),
        compiler_params=_cparams(),
        name="xattn_kv",
    )(mem, w_kv)


def _xattn_kernel(x_ref, wq_ref, kv_ref, wo_ref, g_ref, b_ref, o_ref):
    x = x_ref[...]
    d = x.shape[-1]
    hd = d // XATTN_HEADS
    q = _dot(x, wq_ref[...])
    outs = []
    for h in range(XATTN_HEADS):
        qh = q[:, h * hd:(h + 1) * hd].astype(BF16)
        kh = kv_ref[:, h * hd:(h + 1) * hd]
        vh = kv_ref[:, d + h * hd:d + (h + 1) * hd]
        sc = lax.dot_general(qh, kh, _NT, preferred_element_type=F32) * (hd ** -0.5)
        sc = sc - jnp.max(sc, axis=-1, keepdims=True)
        p = jnp.exp(sc)
        p = p / jnp.sum(p, axis=-1, keepdims=True)
        outs.append(jnp.dot(p.astype(BF16), vh, preferred_element_type=F32))
    o = jnp.concatenate(outs, axis=-1)
    out = _dot(o, wo_ref[...])
    o_ref[...] = _layer_norm(DEEPNORM_ALPHA * x + out, g_ref[...], b_ref[...])


def _xattn_layer(xf, w_q, kv, w_o, ln_g, ln_b, l, nb, ns):
    m, d = xf.shape
    mlen = kv.shape[2]
    xrow = pl.BlockSpec((ROW_TILE, d), lambda b, s: (b * ns + s, 0))
    return pl.pallas_call(
        _xattn_kernel,
        grid=(nb, ns),
        in_specs=[xrow,
                  pl.BlockSpec((None, d, d), lambda b, s: (l, 0, 0)),
                  pl.BlockSpec((None, None, mlen, 2 * d), lambda b, s: (l, b, 0, 0)),
                  pl.BlockSpec((None, d, d), lambda b, s: (l, 0, 0)),
                  pl.BlockSpec((None, 1, d), lambda b, s: (4 * l + 2, 0, 0)),
                  pl.BlockSpec((None, 1, d), lambda b, s: (4 * l + 2, 0, 0))],
        out_specs=xrow,
        out_shape=jax.ShapeDtypeStruct((m, d), F32),
        compiler_params=_cparams(),
        name="xattn",
    )(xf, w_q, kv, w_o, ln_g, ln_b)


def _block_diag(w):
    nblk, bi, bj = w.shape[-3:]
    eye = jnp.eye(nblk, dtype=w.dtype)
    out = w[..., :, :, None, :] * eye[:, None, :, None]
    return out.reshape(w.shape[:-3] + (nblk * bi, nblk * bj))


def kernel(x, mem, ffn1_w_in, ffn1_w_out, ffn2_w_in, ffn2_w_out, ln_g, ln_b, xattn_w_q, xattn_w_kv, xattn_w_o, even_w_in, even_w_out, pool_w, pool_scale, sgu_ln_g, sgu_ln_b, sgu_w, sgu_b, odd_w_in, odd_w_out, rwkv_mu, rwkv_w0, rwkv_w_up, rwkv_a0, rwkv_a_up, rwkv_g_up, rwkv_k_k, rwkv_k_a, rwkv_r_k, rwkv_gn_g, rwkv_gn_b, lru_conv_w, lru_conv_b, lru_w_a, lru_b_a, lru_w_x, lru_b_x, lru_lambda):
    nb, seq, d = x.shape
    assert seq % ROW_TILE == 0 and ROW_TILE % SGU_CHUNK == 0 and ROW_TILE % RWKV_CHUNK == 0
    ns = seq // ROW_TILE
    depth = ffn1_w_in.shape[0]
    xf = x.reshape(nb * seq, d)

    bf = lambda t: t.astype(BF16)
    row3 = lambda t: t.reshape(t.shape[0], 1, -1)
    lg = ln_g.reshape(-1, 1, d)
    lb = ln_b.reshape(-1, 1, d)
    f1_in, f1_out, f2_in, f2_out = bf(ffn1_w_in), bf(ffn1_w_out), bf(ffn2_w_in), bf(ffn2_w_out)
    wq, wo = bf(xattn_w_q), bf(xattn_w_o)
    kv = _kv_proj(mem, bf(xattn_w_kv))
    ev_in, ev_out = bf(even_w_in), bf(even_w_out)
    od_in, od_out = bf(odd_w_in), bf(odd_w_out)
    pw = bf(pool_w)
    sgu_bt = jnp.swapaxes(sgu_b, 1, 2)
    wa_bd, wx_bd = bf(_block_diag(lru_w_a)), bf(_block_diag(lru_w_x))
    w_rwkv = rwkv_w0.shape[-1]
    ones_bd = _block_diag(jnp.ones((w_rwkv // RWKV_HEAD_DIM, RWKV_HEAD_DIM, RWKV_HEAD_DIM), BF16))
    r_k = rwkv_r_k.reshape(rwkv_r_k.shape[0], 1, -1)

    for l in range(depth):
        xf = _ffn_layer(xf, f1_in, f1_out, lg, lb, l, 0, nb, ns)
        e = l // 2
        if l % 2 == 0:
            xf = _even_layer(xf, ev_in, ev_out, pw, row3(pool_scale), row3(sgu_ln_g), row3(sgu_ln_b),
                             sgu_w, sgu_bt, lg, lb, e, l, nb, ns)
        else:
            r, lw, k, v, a, b, g, bonus, yd = _odd_pre(
                xf, od_in, row3(rwkv_mu), row3(rwkv_w0), bf(rwkv_w_up), row3(rwkv_a0), bf(rwkv_a_up),
                bf(rwkv_g_up), row3(rwkv_k_k), row3(rwkv_k_a), r_k, lru_conv_w, row3(lru_conv_b),
                wa_bd, row3(lru_b_a), wx_bd, row3(lru_b_x), row3(lru_lambda), ones_bd, e, nb, ns)
            y = _rwkv_scan(r, lw, k, v, a, b, nb, ns)
            xf = _odd_post(xf, y, g, bonus, yd, row3(rwkv_gn_g), row3(rwkv_gn_b), ones_bd, od_out,
                           lg, lb, e, l, nb, ns)
        xf = _xattn_layer(xf, wq, kv, wo, lg, lb, l, nb, ns)
        xf = _ffn_layer(xf, f2_in, f2_out, lg, lb, l, 3, nb, ns)
    return xf.reshape(nb, seq, d)
```

```python
import functools

import jax
import jax.numpy as jnp
from jax import lax
from jax.experimental import pallas as pl
from jax.experimental.pallas import tpu as pltpu

F32 = jnp.float32
BF16 = jnp.bfloat16

DEPTH = 4
LN_EPS = 1e-5
DEEPNORM_ALPHA = (2 * DEPTH) ** 0.25
MACARON_WEIGHT = 0.5
POOL_WINDOWS = (2, 4, 8, 16)
POOL_HALO = 16
SGU_CHUNK = 128
XATTN_HEADS = 4
RWKV_HEAD_DIM = 64
RWKV_CHUNK = 64
RWKV_GN_EPS = 64e-5
LRU_C = 8.0
CONV_WIDTH = 4
CONV_HALO = 8

ROW_TILE = 512
FFN_CHUNK = 256
VMEM_LIMIT = 56 * 1024 * 1024


def _cparams():
    return pltpu.CompilerParams(dimension_semantics=("arbitrary", "arbitrary"),
                                vmem_limit_bytes=VMEM_LIMIT)


def _dot(a, b):
    return jnp.dot(a.astype(BF16), b.astype(BF16), preferred_element_type=F32)


_NN = (((1,), (0,)), ((), ()))
_NT = (((1,), (1,)), ((), ()))
_TN = (((0,), (0,)), ((), ()))


def _mm1(a, b, dn=_NN):
    return lax.dot_general(a.astype(BF16), b.astype(BF16), dn, preferred_element_type=F32)


def _head_sum(x, ones_bd):
    p1 = x.astype(BF16)
    r1 = x - p1.astype(F32)
    p2 = r1.astype(BF16)
    p3 = (r1 - p2.astype(F32)).astype(BF16)
    f = lambda p: jnp.dot(p, ones_bd, preferred_element_type=F32)
    return f(p1) + (f(p2) + f(p3))


def _layer_norm(y, g, b, eps=LN_EPS):
    mu = jnp.mean(y, axis=-1, keepdims=True)
    yc = y - mu
    var = jnp.mean(yc * yc, axis=-1, keepdims=True)
    return yc * lax.rsqrt(var + eps) * g + b


def _softplus(t):
    return jnp.maximum(t, 0.0) + jnp.log(1.0 + jnp.exp(-jnp.abs(t)))


def _row_ids(rows, cols):
    return lax.broadcasted_iota(jnp.int32, (rows, cols), 0)


def _ffn_kernel(x_ref, win_ref, wout_ref, g_ref, b_ref, o_ref, *, d_ff):
    x = x_ref[...]
    xb = x.astype(BF16)
    acc = jnp.zeros(x.shape, F32)
    for c in range(d_ff // FFN_CHUNK):
        lo = c * FFN_CHUNK
        gate = jnp.dot(xb, win_ref[:, lo:lo + FFN_CHUNK], preferred_element_type=F32)
        up = jnp.dot(xb, win_ref[:, d_ff + lo:d_ff + lo + FFN_CHUNK], preferred_element_type=F32)
        h = (gate * jax.nn.sigmoid(gate) * up).astype(BF16)
        acc = acc + jnp.dot(h, wout_ref[lo:lo + FFN_CHUNK, :], preferred_element_type=F32)
    y = DEEPNORM_ALPHA * x + MACARON_WEIGHT * acc
    o_ref[...] = _layer_norm(y, g_ref[...], b_ref[...])


def _ffn_layer(xf, w_in, w_out, ln_g, ln_b, l, j, nb, ns):
    m, d = xf.shape
    d_ff = w_out.shape[1]
    return pl.pallas_call(
        functools.partial(_ffn_kernel, d_ff=d_ff),
        grid=(nb, ns),
        in_specs=[
            pl.BlockSpec((ROW_TILE, d), lambda b, s: (b * ns + s, 0)),
            pl.BlockSpec((None, d, 2 * d_ff), lambda b, s: (l, 0, 0)),
            pl.BlockSpec((None, d_ff, d), lambda b, s: (l, 0, 0)),
            pl.BlockSpec((None, 1, d), lambda b, s: (4 * l + j, 0, 0)),
            pl.BlockSpec((None, 1, d), lambda b, s: (4 * l + j, 0, 0)),
        ],
        out_specs=pl.BlockSpec((ROW_TILE, d), lambda b, s: (b * ns + s, 0)),
        out_shape=jax.ShapeDtypeStruct((m, d), F32),
        compiler_params=_cparams(),
        name="ffn",
    )(xf, w_in, w_out, ln_g, ln_b)


def _even_kernel(x_ref, win_ref, wout_ref, poolw_ref, pscale_ref, sg_ref, sb_ref, sw_ref, sbt_ref,
                 g_ref, b_ref, o_ref, ext_ref):
    s = pl.program_id(1)
    tm = x_ref.shape[0]
    x = x_ref[...]
    h = _dot(x, win_ref[...])
    pw = poolw_ref.shape[-1]
    width = pw * len(POOL_WINDOWS)
    xa = h[:, :width]

    @pl.when(s == 0)
    def _():
        ext_ref[0:POOL_HALO, :] = jnp.zeros((POOL_HALO, width), F32)

    ext_ref[POOL_HALO:, :] = xa
    ext = ext_ref[...]
    ext_ref[0:POOL_HALO, :] = ext[tm:tm + POOL_HALO, :]
    pos = (s * tm + 1 + _row_ids(tm, 1)).astype(F32)
    ya = []
    for g, win in enumerate(POOL_WINDOWS):
        acc = ext[:, g * pw:(g + 1) * pw]
        d = 1
        while d < win:
            acc = acc + pltpu.roll(acc, d, axis=0)
            d *= 2
        mean = acc[POOL_HALO:, :] / jnp.minimum(pos, float(win))
        pooled = mean - xa[:, g * pw:(g + 1) * pw]
        ya.append(_dot(pooled, poolw_ref[g]))
    ya = jnp.concatenate(ya, axis=-1) * pscale_ref[...]

    u = jax.nn.gelu(h[:, width:2 * width])
    v = jax.nn.gelu(h[:, 2 * width:3 * width])
    v = _layer_norm(v, sg_ref[...], sb_ref[...])
    heads = sw_ref.shape[0]
    hw = width // heads
    ti = lax.broadcasted_iota(jnp.int32, (SGU_CHUNK, SGU_CHUNK), 0)
    si = lax.broadcasted_iota(jnp.int32, (SGU_CHUNK, SGU_CHUNK), 1)
    causal = ti >= si
    cols = []
    for hd in range(heads):
        w = jnp.where(causal, sw_ref[hd], 0.0).astype(BF16)
        bias = sbt_ref[:, hd:hd + 1]
        rows = []
        for c in range(tm // SGU_CHUNK):
            vv = v[c * SGU_CHUNK:(c + 1) * SGU_CHUNK, hd * hw:(hd + 1) * hw].astype(BF16)
            rows.append(jnp.dot(w, vv, preferred_element_type=F32) + bias)
        cols.append(jnp.concatenate(rows, axis=0))
    yb = u * jnp.concatenate(cols, axis=-1)

    out = _dot(ya, wout_ref[0:width, :]) + _dot(yb, wout_ref[width:2 * width, :])
    o_ref[...] = _layer_norm(DEEPNORM_ALPHA * x + out, g_ref[...], b_ref[...])


def _even_layer(xf, w_in, w_out, pool_w, pool_scale, sgu_g, sgu_b, sgu_w, sgu_bt, ln_g, ln_b, e, l, nb, ns):
    m, d = xf.shape
    n_in = w_in.shape[-1]
    width = pool_scale.shape[-1]
    full = lambda shape: pl.BlockSpec((None,) + shape, lambda b, s: (e,) + (0,) * len(shape))
    return pl.pallas_call(
        _even_kernel,
        grid=(nb, ns),
        in_specs=[
            pl.BlockSpec((ROW_TILE, d), lambda b, s: (b * ns + s, 0)),
            full((d, n_in)),
            full((2 * width, d)),
            full(pool_w.shape[1:]),
            full((1, width)),
            full((1, width)),
            full((1, width)),
            full(sgu_w.shape[1:]),
            full(sgu_bt.shape[1:]),
            pl.BlockSpec((None, 1, d), lambda b, s: (4 * l + 1, 0, 0)),
            pl.BlockSpec((None, 1, d), lambda b, s: (4 * l + 1, 0, 0)),
        ],
        out_specs=pl.BlockSpec((ROW_TILE, d), lambda b, s: (b * ns + s, 0)),
        out_shape=jax.ShapeDtypeStruct((m, d), F32),
        scratch_shapes=[pltpu.VMEM((POOL_HALO + ROW_TILE, width), F32)],
        compiler_params=_cparams(),
        name="even_mixer",
    )(xf, w_in, w_out, pool_w, pool_scale, sgu_g, sgu_b, sgu_w, sgu_bt, ln_g, ln_b)


def _odd_pre_kernel(x_ref, win_ref, mu_ref, w0_ref, wup_ref, a0_ref, aup_ref, gup_ref, kk_ref, ka_ref,
                    rk_ref, cw_ref, cb_ref, wa_ref, ba_ref, wx_ref, bx_ref, lam_ref, ones_ref,
                    r_o, lw_o, k_o, v_o, a_o, b_o, g_o, bonus_o, yd_o,
                    hprev_ref, xr_ref, hlru_ref):
    s = pl.program_id(1)
    tm = x_ref.shape[0]
    w = w0_ref.shape[-1]
    n_c = mu_ref.shape[-1]
    x = x_ref[...]
    h = _dot(x, win_ref[...])

    @pl.when(s == 0)
    def _():
        hprev_ref[...] = jnp.zeros(hprev_ref.shape, F32)
        xr_ref[0:CONV_HALO, :] = jnp.zeros((CONV_HALO, w), F32)
        hlru_ref[...] = jnp.zeros(hlru_ref.shape, F32)

    hc = h[:, :n_c]
    first = _row_ids(tm, 1) == 0
    prev = jnp.where(first, hprev_ref[...], pltpu.roll(hc, 1, axis=0))
    hprev_ref[...] = hc[tm - 1:tm, :]
    hc = hc + mu_ref[...] * (prev - hc)
    r = hc[:, 0:w]
    k = hc[:, w:2 * w]
    v = hc[:, 2 * w:3 * w]
    rest = hc[:, 3 * w:]
    dr = wup_ref.shape[0]
    ar = aup_ref.shape[0]
    wd = rest[:, 0:dr]
    ad = rest[:, dr:dr + ar]
    gd = rest[:, dr + ar:]

    z = w0_ref[...] + _dot(jnp.tanh(wd), wup_ref[...])
    logw = -_softplus(-z) - 0.5
    lw_o[...] = -jnp.exp(logw)
    asig = jax.nn.sigmoid(a0_ref[...] + _dot(ad, aup_ref[...]))
    g_o[...] = _dot(jax.nn.sigmoid(gd), gup_ref[...])
    ones_bd = ones_ref[...]
    kk = k * kk_ref[...]
    kk = kk * lax.rsqrt(jnp.maximum(_head_sum(kk * kk, ones_bd), 1e-24))
    k2 = k * (1.0 + (asig - 1.0) * ka_ref[...])
    r_o[...] = r
    k_o[...] = k2
    v_o[...] = v
    a_o[...] = -kk
    b_o[...] = kk * asig
    bonus_o[...] = _head_sum(r * k2 * rk_ref[...], ones_bd) * v

    gate = h[:, n_c:n_c + w]
    xr = h[:, n_c + w:]
    xr_ref[CONV_HALO:, :] = xr
    ext = xr_ref[...]
    xr_ref[0:CONV_HALO, :] = ext[tm:tm + CONV_HALO, :]
    xc = cw_ref[CONV_WIDTH - 1:CONV_WIDTH, :] * xr + cb_ref[...]
    for j in range(1, CONV_WIDTH):
        xc = xc + cw_ref[CONV_WIDTH - 1 - j:CONV_WIDTH - j, :] * pltpu.roll(ext, j, axis=0)[CONV_HALO:, :]
    rec = jax.nn.sigmoid(_dot(xc, wa_ref[...]) + ba_ref[...])
    inp = jax.nn.sigmoid(_dot(xc, wx_ref[...]) + bx_ref[...])
    log_a = -LRU_C * rec * _softplus(-lam_ref[...])
    a = jnp.exp(log_a)
    bx = jnp.sqrt(-jnp.tanh(log_a) * (a * a + 1.0)) * (inp * xc)
    rows = _row_ids(tm, w)
    d = 1
    while d < tm:
        keep = rows >= d
        a_sh = jnp.where(keep, pltpu.roll(a, d, axis=0), 1.0)
        b_sh = jnp.where(keep, pltpu.roll(bx, d, axis=0), 0.0)
        bx = a * b_sh + bx
        a = a * a_sh
        d *= 2
    hl = a * hlru_ref[...] + bx
    hlru_ref[...] = hl[tm - 1:tm, :]
    yd_o[...] = hl * jax.nn.gelu(gate)


def _odd_pre(xf, w_in, mu, w0, w_up, a0, a_up, g_up, k_k, k_a, r_k, conv_w, conv_b, wa_bd, b_a, wx_bd, b_x,
             lam, ones_bd, e, nb, ns):
    m, d = xf.shape
    w = w0.shape[-1]
    full = lambda arr: pl.BlockSpec((None,) + arr.shape[1:], lambda b, s: (e,) + (0,) * (arr.ndim - 1))
    row = pl.BlockSpec((ROW_TILE, w), lambda b, s: (b * ns + s, 0))
    params = (w_in, mu, w0, w_up, a0, a_up, g_up, k_k, k_a, r_k, conv_w, conv_b, wa_bd, b_a, wx_bd, b_x, lam)
    return pl.pallas_call(
        _odd_pre_kernel,
        grid=(nb, ns),
        in_specs=[pl.BlockSpec((ROW_TILE, d), lambda b, s: (b * ns + s, 0))]
        + [full(p) for p in params]
        + [pl.BlockSpec(ones_bd.shape, lambda b, s: (0, 0))],
        out_specs=[row] * 9,
        out_shape=[jax.ShapeDtypeStruct((m, w), F32)] * 9,
        scratch_shapes=[pltpu.VMEM((1, mu.shape[-1]), F32),
                        pltpu.VMEM((CONV_HALO + ROW_TILE, w), F32),
                        pltpu.VMEM((1, w), F32)],
        compiler_params=_cparams(),
        name="odd_pre",
    )(xf, *params, ones_bd)


def _rwkv_kernel(r_ref, lw_ref, k_ref, v_ref, a_ref, b_ref, y_ref, t_ref):
    s = pl.program_id(1)
    tm, w = r_ref.shape
    n = RWKV_HEAD_DIM
    lc = RWKV_CHUNK
    heads = w // n

    @pl.when(s == 0)
    def _():
        t_ref[...] = jnp.zeros(t_ref.shape, F32)

    ti = lax.broadcasted_iota(jnp.int32, (lc, lc), 0)
    si = lax.broadcasted_iota(jnp.int32, (lc, lc), 1)
    strict = ti > si
    incl = ti >= si
    diag = ti == si
    eye = diag.astype(F32)
    rows = _row_ids(lc, w)

    def chunk(c, carry):
        sl = pl.ds(pl.multiple_of(c * lc, lc), lc)
        lw = lw_ref[sl, :]
        cum = lw
        d = 1
        while d < lc:
            cum = cum + jnp.where(rows >= d, pltpu.roll(cum, d, axis=0), 0.0)
            d *= 2
        cl = cum[lc - 1:lc, :]
        einv = jnp.exp(-cum)
        eend = jnp.exp(cl - cum)
        a = a_ref[sl, :]
        b = b_ref[sl, :]
        k = k_ref[sl, :]
        v_all = v_ref[sl, :]
        at_all = a * jnp.exp(cum - lw)
        rt_all = r_ref[sl, :] * jnp.exp(cum)
        bt_all = b * einv
        kt_all = k * einv
        bh_all = b * eend
        kh_all = k * eend
        pl_all = jnp.exp(cl)
        hr = range(heads)
        cut = lambda t: [t[:, hd * n:(hd + 1) * n] for hd in hr]
        at, rt, bt, kt, bh, kh, v = (cut(t) for t in (at_all, rt_all, bt_all, kt_all, bh_all, kh_all, v_all))
        s1 = [_mm1(jnp.concatenate([at[i], rt[i]], axis=0), jnp.concatenate([bt[i], kt[i]], axis=0), _NT)
              for i in hr]
        a_ab = [jnp.where(strict, s[:lc, :lc], 0.0) for s in s1]
        a_ak = [jnp.where(strict, s[:lc, lc:], 0.0) for s in s1]
        m_rb = [jnp.where(incl, s[lc:, :lc], 0.0) for s in s1]
        m_rk = [jnp.where(incl, s[lc:, lc:], 0.0) for s in s1]
        inv = [eye + t for t in a_ab]
        apow = a_ab
        span = 2
        while span < lc:
            apow = [_mm1(t, t) for t in apow]
            inv = [w_ + _mm1(w_, t) for w_, t in zip(inv, apow)]
            span *= 2
        x0 = [_mm1(a_ak[i], v[i]) for i in hr]
        av = [_mm1(inv[i], jnp.concatenate([at[i], x0[i]], axis=1)) for i in hr]
        ap = [t[:, :n] for t in av]
        vp = [t[:, n:] for t in av]
        g = [_mm1(bh[i], ap[i], _TN) + jnp.where(diag, jnp.broadcast_to(pl_all[:, i * n:(i + 1) * n], (n, n)), 0.0)
             for i in hr]
        hh = [_mm1(bh[i], vp[i], _TN) + _mm1(kh[i], v[i], _TN) for i in hr]
        q = [rt[i] + _mm1(m_rb[i], ap[i]) for i in hr]
        yl = [_mm1(m_rb[i], vp[i]) + _mm1(m_rk[i], v[i]) for i in hr]
        t0 = [t_ref[i] for i in hr]
        ys = [_mm1(q[i], t0[i]) + yl[i] for i in hr]
        for i in hr:
            t_ref[i] = _mm1(g[i], t0[i]) + hh[i]
        y_ref[sl, :] = jnp.concatenate(ys, axis=-1)
        return carry

    lax.fori_loop(0, tm // lc, chunk, 0)


def _rwkv_scan(r, lw, k, v, a, b, nb, ns):
    m, w = r.shape
    row = pl.BlockSpec((ROW_TILE, w), lambda bb, s: (bb * ns + s, 0))
    heads = w // RWKV_HEAD_DIM
    return pl.pallas_call(
        _rwkv_kernel,
        grid=(nb, ns),
        in_specs=[row] * 6,
        out_specs=row,
        out_shape=jax.ShapeDtypeStruct((m, w), F32),
        scratch_shapes=[pltpu.VMEM((heads, RWKV_HEAD_DIM, RWKV_HEAD_DIM), F32)],
        compiler_params=_cparams(),
        name="rwkv7",
    )(r, lw, k, v, a, b)


def _odd_post_kernel(x_ref, y_ref, g_ref, bonus_ref, yd_ref, gng_ref, gnb_ref, ones_ref, wout_ref,
                     lg_ref, lb_ref, o_ref):
    x = x_ref[...]
    y = y_ref[...]
    w = y.shape[-1]
    ones_bd = ones_ref[...]
    inv_n = 1.0 / RWKV_HEAD_DIM
    mu = _head_sum(y, ones_bd) * inv_n
    yc = y - mu
    var = _head_sum(yc * yc, ones_bd) * inv_n
    yn = yc * lax.rsqrt(var + RWKV_GN_EPS) * gng_ref[...] + gnb_ref[...]
    y_rwkv = (yn + bonus_ref[...]) * g_ref[...]
    out = _dot(y_rwkv, wout_ref[0:w, :]) + _dot(yd_ref[...], wout_ref[w:2 * w, :])
    o_ref[...] = _layer_norm(DEEPNORM_ALPHA * x + out, lg_ref[...], lb_ref[...])


def _odd_post(xf, y, g, bonus, yd, gn_g, gn_b, ones_bd, w_out, ln_g, ln_b, e, l, nb, ns):
    m, d = xf.shape
    w = y.shape[-1]
    row = pl.BlockSpec((ROW_TILE, w), lambda b, s: (b * ns + s, 0))
    xrow = pl.BlockSpec((ROW_TILE, d), lambda b, s: (b * ns + s, 0))
    return pl.pallas_call(
        _odd_post_kernel,
        grid=(nb, ns),
        in_specs=[xrow, row, row, row, row,
                  pl.BlockSpec((None, 1, w), lambda b, s: (e, 0, 0)),
                  pl.BlockSpec((None, 1, w), lambda b, s: (e, 0, 0)),
                  pl.BlockSpec(ones_bd.shape, lambda b, s: (0, 0)),
                  pl.BlockSpec((None, 2 * w, d), lambda b, s: (e, 0, 0)),
                  pl.BlockSpec((None, 1, d), lambda b, s: (4 * l + 1, 0, 0)),
                  pl.BlockSpec((None, 1, d), lambda b, s: (4 * l + 1, 0, 0))],
        out_specs=xrow,
        out_shape=jax.ShapeDtypeStruct((m, d), F32),
        compiler_params=_cparams(),
        name="odd_post",
    )(xf, y, g, bonus, yd, gn_g, gn_b, ones_bd, w_out, ln_g, ln_b)


def _kv_kernel(mem_ref, wkv_ref, kv_ref):
    kv_ref[...] = _dot(mem_ref[...], wkv_ref[...]).astype(BF16)


def _kv_proj(mem, w_kv):
    nb, mlen, d = mem.shape
    nl, _, n2 = w_kv.shape
    return pl.pallas_call(
        _kv_kernel,
        grid=(nl, nb),
        in_specs=[pl.BlockSpec((None, mlen, d), lambda l, b: (b, 0, 0)),
                  pl.BlockSpec((None, d, n2), lambda l, b: (l, 0, 0))],
        out_specs=pl.BlockSpec((None, None, mlen, n2), lambda l, b: (l, b, 0, 0)),
        out_shape=jax.ShapeDtypeStruct((nl, nb, mlen, n2), BF16),
        compiler_params=_cparams(),
        name="xattn_kv",
    )(mem, w_kv)


def _xattn_kernel(x_ref, wq_ref, kv_ref, wo_ref, g_ref, b_ref, o_ref):
    x = x_ref[...]
    d = x.shape[-1]
    hd = d // XATTN_HEADS
    q = _dot(x, wq_ref[...])
    outs = []
    for h in range(XATTN_HEADS):
        qh = q[:, h * hd:(h + 1) * hd].astype(BF16)
        kh = kv_ref[:, h * hd:(h + 1) * hd]
        vh = kv_ref[:, d + h * hd:d + (h + 1) * hd]
        sc = lax.dot_general(qh, kh, _NT, preferred_element_type=F32) * (hd ** -0.5)
        sc = sc - jnp.max(sc, axis=-1, keepdims=True)
        p = jnp.exp(sc)
        p = p / jnp.sum(p, axis=-1, keepdims=True)
        outs.append(jnp.dot(p.astype(BF16), vh, preferred_element_type=F32))
    o = jnp.concatenate(outs, axis=-1)
    out = _dot(o, wo_ref[...])
    o_ref[...] = _layer_norm(DEEPNORM_ALPHA * x + out, g_ref[...], b_ref[...])


def _xattn_layer(xf, w_q, kv, w_o, ln_g, ln_b, l, nb, ns):
    m, d = xf.shape
    mlen = kv.shape[2]
    xrow = pl.BlockSpec((ROW_TILE, d), lambda b, s: (b * ns + s, 0))
    return pl.pallas_call(
        _xattn_kernel,
        grid=(nb, ns),
        in_specs=[xrow,
                  pl.BlockSpec((None, d, d), lambda b, s: (l, 0, 0)),
                  pl.BlockSpec((None, None, mlen, 2 * d), lambda b, s: (l, b, 0, 0)),
                  pl.BlockSpec((None, d, d), lambda b, s: (l, 0, 0)),
                  pl.BlockSpec((None, 1, d), lambda b, s: (4 * l + 2, 0, 0)),
                  pl.BlockSpec((None, 1, d), lambda b, s: (4 * l + 2, 0, 0))],
        out_specs=xrow,
        out_shape=jax.ShapeDtypeStruct((m, d), F32),
        compiler_params=_cparams(),
        name="xattn",
    )(xf, w_q, kv, w_o, ln_g, ln_b)


def _block_diag(w):
    nblk, bi, bj = w.shape[-3:]
    eye = jnp.eye(nblk, dtype=w.dtype)
    out = w[..., :, :, None, :] * eye[:, None, :, None]
    return out.reshape(w.shape[:-3] + (nblk * bi, nblk * bj))


def kernel(x, mem, ffn1_w_in, ffn1_w_out, ffn2_w_in, ffn2_w_out, ln_g, ln_b, xattn_w_q, xattn_w_kv, xattn_w_o, even_w_in, even_w_out, pool_w, pool_scale, sgu_ln_g, sgu_ln_b, sgu_w, sgu_b, odd_w_in, odd_w_out, rwkv_mu, rwkv_w0, rwkv_w_up, rwkv_a0, rwkv_a_up, rwkv_g_up, rwkv_k_k, rwkv_k_a, rwkv_r_k, rwkv_gn_g, rwkv_gn_b, lru_conv_w, lru_conv_b, lru_w_a, lru_b_a, lru_w_x, lru_b_x, lru_lambda):
    nb, seq, d = x.shape
    assert seq % ROW_TILE == 0 and ROW_TILE % SGU_CHUNK == 0 and ROW_TILE % RWKV_CHUNK == 0
    ns = seq // ROW_TILE
    depth = ffn1_w_in.shape[0]
    xf = x.reshape(nb * seq, d)

    bf = lambda t: t.astype(BF16)
    row3 = lambda t: t.reshape(t.shape[0], 1, -1)
    lg = ln_g.reshape(-1, 1, d)
    lb = ln_b.reshape(-1, 1, d)
    f1_in, f1_out, f2_in, f2_out = bf(ffn1_w_in), bf(ffn1_w_out), bf(ffn2_w_in), bf(ffn2_w_out)
    wq, wo = bf(xattn_w_q), bf(xattn_w_o)
    kv = _kv_proj(mem, bf(xattn_w_kv))
    ev_in, ev_out = bf(even_w_in), bf(even_w_out)
    od_in, od_out = bf(odd_w_in), bf(odd_w_out)
    pw = bf(pool_w)
    sgu_bt = jnp.swapaxes(sgu_b, 1, 2)
    wa_bd, wx_bd = bf(_block_diag(lru_w_a)), bf(_block_diag(lru_w_x))
    w_rwkv = rwkv_w0.shape[-1]
    ones_bd = _block_diag(jnp.ones((w_rwkv // RWKV_HEAD_DIM, RWKV_HEAD_DIM, RWKV_HEAD_DIM), BF16))
    r_k = rwkv_r_k.reshape(rwkv_r_k.shape[0], 1, -1)

    for l in range(depth):
        xf = _ffn_layer(xf, f1_in, f1_out, lg, lb, l, 0, nb, ns)
        e = l // 2
        if l % 2 == 0:
            xf = _even_layer(xf, ev_in, ev_out, pw, row3(pool_scale), row3(sgu_ln_g), row3(sgu_ln_b),
                             sgu_w, sgu_bt, lg, lb, e, l, nb, ns)
        else:
            r, lw, k, v, a, b, g, bonus, yd = _odd_pre(
                xf, od_in, row3(rwkv_mu), row3(rwkv_w0), bf(rwkv_w_up), row3(rwkv_a0), bf(rwkv_a_up),
                bf(rwkv_g_up), row3(rwkv_k_k), row3(rwkv_k_a), r_k, lru_conv_w, row3(lru_conv_b),
                wa_bd, row3(lru_b_a), wx_bd, row3(lru_b_x), row3(lru_lambda), ones_bd, e, nb, ns)
            y = _rwkv_scan(r, lw, k, v, a, b, nb, ns)
            xf = _odd_post(xf, y, g, bonus, yd, row3(rwkv_gn_g), row3(rwkv_gn_b), ones_bd, od_out,
                           lg, lb, e, l, nb, ns)
        xf = _xattn_layer(xf, wq, kv, wo, lg, lb, l, nb, ns)
        xf = _ffn_layer(xf, f2_in, f2_out, lg, lb, l, 3, nb, ns)
    return xf.reshape(nb, seq, d)
```

```python
import functools

import jax
import jax.numpy as jnp
from jax import lax
from jax.experimental import pallas as pl
from jax.experimental.pallas import tpu as pltpu

F32 = jnp.float32
BF16 = jnp.bfloat16

DEPTH = 4
LN_EPS = 1e-5
DEEPNORM_ALPHA = (2 * DEPTH) ** 0.25
MACARON_WEIGHT = 0.5
POOL_WINDOWS = (2, 4, 8, 16)
POOL_HALO = 16
SGU_CHUNK = 128
XATTN_HEADS = 4
RWKV_HEAD_DIM = 64
RWKV_CHUNK = 64
RWKV_GN_EPS = 64e-5
MXU_WIDTH = 256
RWKV_INTERLEAVE = 4
LRU_C = 8.0
CONV_WIDTH = 4
CONV_HALO = 8

ROW_TILE = 512
FFN_CHUNK = 256
VMEM_LIMIT = 56 * 1024 * 1024


def _cparams():
    return pltpu.CompilerParams(dimension_semantics=("arbitrary", "arbitrary"),
                                vmem_limit_bytes=VMEM_LIMIT)


def _dot(a, b):
    return jnp.dot(a.astype(BF16), b.astype(BF16), preferred_element_type=F32)


_NN = (((1,), (0,)), ((), ()))
_NT = (((1,), (1,)), ((), ()))
_TN = (((0,), (0,)), ((), ()))


def _mm1(a, b, dn=_NN):
    return lax.dot_general(a.astype(BF16), b.astype(BF16), dn, preferred_element_type=F32)


def _head_sum(x, ones_bd):
    p1 = x.astype(BF16)
    r1 = x - p1.astype(F32)
    p2 = r1.astype(BF16)
    p3 = (r1 - p2.astype(F32)).astype(BF16)
    f = lambda p: jnp.dot(p, ones_bd, preferred_element_type=F32)
    return f(p1) + (f(p2) + f(p3))


def _layer_norm(y, g, b, eps=LN_EPS):
    mu = jnp.mean(y, axis=-1, keepdims=True)
    yc = y - mu
    var = jnp.mean(yc * yc, axis=-1, keepdims=True)
    return yc * lax.rsqrt(var + eps) * g + b


def _softplus(t):
    return jnp.maximum(t, 0.0) + jnp.log(1.0 + jnp.exp(-jnp.abs(t)))


def _row_ids(rows, cols):
    return lax.broadcasted_iota(jnp.int32, (rows, cols), 0)


def _ffn_kernel(x_ref, win_ref, wout_ref, g_ref, b_ref, o_ref, *, d_ff):
    x = x_ref[...]
    xb = x.astype(BF16)
    acc = jnp.zeros(x.shape, F32)
    for c in range(d_ff // FFN_CHUNK):
        lo = c * FFN_CHUNK
        gate = jnp.dot(xb, win_ref[:, lo:lo + FFN_CHUNK], preferred_element_type=F32)
        up = jnp.dot(xb, win_ref[:, d_ff + lo:d_ff + lo + FFN_CHUNK], preferred_element_type=F32)
        h = (gate * jax.nn.sigmoid(gate) * up).astype(BF16)
        acc = acc + jnp.dot(h, wout_ref[lo:lo + FFN_CHUNK, :], preferred_element_type=F32)
    y = DEEPNORM_ALPHA * x + MACARON_WEIGHT * acc
    o_ref[...] = _layer_norm(y, g_ref[...], b_ref[...])


def _ffn_layer(xf, w_in, w_out, ln_g, ln_b, l, j, nb, ns):
    m, d = xf.shape
    d_ff = w_out.shape[1]
    return pl.pallas_call(
        functools.partial(_ffn_kernel, d_ff=d_ff),
        grid=(nb, ns),
        in_specs=[
            pl.BlockSpec((ROW_TILE, d), lambda b, s: (b * ns + s, 0)),
            pl.BlockSpec((None, d, 2 * d_ff), lambda b, s: (l, 0, 0)),
            pl.BlockSpec((None, d_ff, d), lambda b, s: (l, 0, 0)),
            pl.BlockSpec((None, 1, d), lambda b, s: (4 * l + j, 0, 0)),
            pl.BlockSpec((None, 1, d), lambda b, s: (4 * l + j, 0, 0)),
        ],
        out_specs=pl.BlockSpec((ROW_TILE, d), lambda b, s: (b * ns + s, 0)),
        out_shape=jax.ShapeDtypeStruct((m, d), F32),
        compiler_params=_cparams(),
        name="ffn",
    )(xf, w_in, w_out, ln_g, ln_b)


def _even_kernel(x_ref, win_ref, wout_ref, poolw_ref, pscale_ref, sg_ref, sb_ref, sw_ref, sbt_ref,
                 g_ref, b_ref, o_ref, ext_ref):
    s = pl.program_id(1)
    tm = x_ref.shape[0]
    x = x_ref[...]
    h = _dot(x, win_ref[...])
    pw = poolw_ref.shape[-1]
    width = pw * len(POOL_WINDOWS)
    xa = h[:, :width]

    @pl.when(s == 0)
    def _():
        ext_ref[0:POOL_HALO, :] = jnp.zeros((POOL_HALO, width), F32)

    ext_ref[POOL_HALO:, :] = xa
    ext = ext_ref[...]
    ext_ref[0:POOL_HALO, :] = ext[tm:tm + POOL_HALO, :]
    pos = (s * tm + 1 + _row_ids(tm, 1)).astype(F32)
    ya = []
    for g, win in enumerate(POOL_WINDOWS):
        acc = ext[:, g * pw:(g + 1) * pw]
        d = 1
        while d < win:
            acc = acc + pltpu.roll(acc, d, axis=0)
            d *= 2
        mean = acc[POOL_HALO:, :] / jnp.minimum(pos, float(win))
        pooled = mean - xa[:, g * pw:(g + 1) * pw]
        ya.append(_dot(pooled, poolw_ref[g]))
    ya = jnp.concatenate(ya, axis=-1) * pscale_ref[...]

    u = jax.nn.gelu(h[:, width:2 * width])
    v = jax.nn.gelu(h[:, 2 * width:3 * width])
    v = _layer_norm(v, sg_ref[...], sb_ref[...])
    heads = sw_ref.shape[0]
    hw = width // heads
    ti = lax.broadcasted_iota(jnp.int32, (SGU_CHUNK, SGU_CHUNK), 0)
    si = lax.broadcasted_iota(jnp.int32, (SGU_CHUNK, SGU_CHUNK), 1)
    causal = ti >= si
    cols = []
    for hd in range(heads):
        w = jnp.where(causal, sw_ref[hd], 0.0).astype(BF16)
        bias = sbt_ref[:, hd:hd + 1]
        rows = []
        for c in range(tm // SGU_CHUNK):
            vv = v[c * SGU_CHUNK:(c + 1) * SGU_CHUNK, hd * hw:(hd + 1) * hw].astype(BF16)
            rows.append(jnp.dot(w, vv, preferred_element_type=F32) + bias)
        cols.append(jnp.concatenate(rows, axis=0))
    yb = u * jnp.concatenate(cols, axis=-1)

    out = _dot(ya, wout_ref[0:width, :]) + _dot(yb, wout_ref[width:2 * width, :])
    o_ref[...] = _layer_norm(DEEPNORM_ALPHA * x + out, g_ref[...], b_ref[...])


def _even_layer(xf, w_in, w_out, pool_w, pool_scale, sgu_g, sgu_b, sgu_w, sgu_bt, ln_g, ln_b, e, l, nb, ns):
    m, d = xf.shape
    n_in = w_in.shape[-1]
    width = pool_scale.shape[-1]
    full = lambda shape: pl.BlockSpec((None,) + shape, lambda b, s: (e,) + (0,) * len(shape))
    return pl.pallas_call(
        _even_kernel,
        grid=(nb, ns),
        in_specs=[
            pl.BlockSpec((ROW_TILE, d), lambda b, s: (b * ns + s, 0)),
            full((d, n_in)),
            full((2 * width, d)),
            full(pool_w.shape[1:]),
            full((1, width)),
            full((1, width)),
            full((1, width)),
            full(sgu_w.shape[1:]),
            full(sgu_bt.shape[1:]),
            pl.BlockSpec((None, 1, d), lambda b, s: (4 * l + 1, 0, 0)),
            pl.BlockSpec((None, 1, d), lambda b, s: (4 * l + 1, 0, 0)),
        ],
        out_specs=pl.BlockSpec((ROW_TILE, d), lambda b, s: (b * ns + s, 0)),
        out_shape=jax.ShapeDtypeStruct((m, d), F32),
        scratch_shapes=[pltpu.VMEM((POOL_HALO + ROW_TILE, width), F32)],
        compiler_params=_cparams(),
        name="even_mixer",
    )(xf, w_in, w_out, pool_w, pool_scale, sgu_g, sgu_b, sgu_w, sgu_bt, ln_g, ln_b)


def _odd_pre_kernel(x_ref, win_ref, mu_ref, w0_ref, wup_ref, a0_ref, aup_ref, gup_ref, kk_ref, ka_ref,
                    rk_ref, cw_ref, cb_ref, wa_ref, ba_ref, wx_ref, bx_ref, lam_ref, ones_ref,
                    r_o, lw_o, k_o, v_o, a_o, b_o, g_o, bonus_o, yd_o,
                    hprev_ref, xr_ref, hlru_ref):
    s = pl.program_id(1)
    tm = x_ref.shape[0]
    w = w0_ref.shape[-1]
    n_c = mu_ref.shape[-1]
    x = x_ref[...]
    h = _dot(x, win_ref[...])

    @pl.when(s == 0)
    def _():
        hprev_ref[...] = jnp.zeros(hprev_ref.shape, F32)
        xr_ref[0:CONV_HALO, :] = jnp.zeros((CONV_HALO, w), F32)
        hlru_ref[...] = jnp.zeros(hlru_ref.shape, F32)

    hc = h[:, :n_c]
    first = _row_ids(tm, 1) == 0
    prev = jnp.where(first, hprev_ref[...], pltpu.roll(hc, 1, axis=0))
    hprev_ref[...] = hc[tm - 1:tm, :]
    hc = hc + mu_ref[...] * (prev - hc)
    r = hc[:, 0:w]
    k = hc[:, w:2 * w]
    v = hc[:, 2 * w:3 * w]
    rest = hc[:, 3 * w:]
    dr = wup_ref.shape[0]
    ar = aup_ref.shape[0]
    wd = rest[:, 0:dr]
    ad = rest[:, dr:dr + ar]
    gd = rest[:, dr + ar:]

    z = w0_ref[...] + _dot(jnp.tanh(wd), wup_ref[...])
    logw = -_softplus(-z) - 0.5
    lw_o[...] = -jnp.exp(logw)
    asig = jax.nn.sigmoid(a0_ref[...] + _dot(ad, aup_ref[...]))
    g_o[...] = _dot(jax.nn.sigmoid(gd), gup_ref[...])
    ones_bd = ones_ref[...]
    kk = k * kk_ref[...]
    kk = kk * lax.rsqrt(jnp.maximum(_head_sum(kk * kk, ones_bd), 1e-24))
    k2 = k * (1.0 + (asig - 1.0) * ka_ref[...])
    r_o[...] = r
    k_o[...] = k2
    v_o[...] = v
    a_o[...] = -kk
    b_o[...] = kk * asig
    bonus_o[...] = _head_sum(r * k2 * rk_ref[...], ones_bd) * v

    gate = h[:, n_c:n_c + w]
    xr = h[:, n_c + w:]
    xr_ref[CONV_HALO:, :] = xr
    ext = xr_ref[...]
    xr_ref[0:CONV_HALO, :] = ext[tm:tm + CONV_HALO, :]
    xc = cw_ref[CONV_WIDTH - 1:CONV_WIDTH, :] * xr + cb_ref[...]
    for j in range(1, CONV_WIDTH):
        xc = xc + cw_ref[CONV_WIDTH - 1 - j:CONV_WIDTH - j, :] * pltpu.roll(ext, j, axis=0)[CONV_HALO:, :]
    rec = jax.nn.sigmoid(_dot(xc, wa_ref[...]) + ba_ref[...])
    inp = jax.nn.sigmoid(_dot(xc, wx_ref[...]) + bx_ref[...])
    log_a = -LRU_C * rec * _softplus(-lam_ref[...])
    a = jnp.exp(log_a)
    bx = jnp.sqrt(-jnp.tanh(log_a) * (a * a + 1.0)) * (inp * xc)
    rows = _row_ids(tm, w)
    d = 1
    while d < tm:
        keep = rows >= d
        a_sh = jnp.where(keep, pltpu.roll(a, d, axis=0), 1.0)
        b_sh = jnp.where(keep, pltpu.roll(bx, d, axis=0), 0.0)
        bx = a * b_sh + bx
        a = a * a_sh
        d *= 2
    hl = a * hlru_ref[...] + bx
    hlru_ref[...] = hl[tm - 1:tm, :]
    yd_o[...] = hl * jax.nn.gelu(gate)


def _odd_pre(xf, w_in, mu, w0, w_up, a0, a_up, g_up, k_k, k_a, r_k, conv_w, conv_b, wa_bd, b_a, wx_bd, b_x,
             lam, ones_bd, e, nb, ns):
    m, d = xf.shape
    w = w0.shape[-1]
    full = lambda arr: pl.BlockSpec((None,) + arr.shape[1:], lambda b, s: (e,) + (0,) * (arr.ndim - 1))
    row = pl.BlockSpec((ROW_TILE, w), lambda b, s: (b * ns + s, 0))
    params = (w_in, mu, w0, w_up, a0, a_up, g_up, k_k, k_a, r_k, conv_w, conv_b, wa_bd, b_a, wx_bd, b_x, lam)
    return pl.pallas_call(
        _odd_pre_kernel,
        grid=(nb, ns),
        in_specs=[pl.BlockSpec((ROW_TILE, d), lambda b, s: (b * ns + s, 0))]
        + [full(p) for p in params]
        + [pl.BlockSpec(ones_bd.shape, lambda b, s: (0, 0))],
        out_specs=[row] * 9,
        out_shape=[jax.ShapeDtypeStruct((m, w), F32)] * 9,
        scratch_shapes=[pltpu.VMEM((1, mu.shape[-1]), F32),
                        pltpu.VMEM((CONV_HALO + ROW_TILE, w), F32),
                        pltpu.VMEM((1, w), F32)],
        compiler_params=_cparams(),
        name="odd_pre",
    )(xf, *params, ones_bd)


def _rwkv_kernel(r_ref, lw_ref, k_ref, v_ref, a_ref, b_ref, y_ref, t_ref, q_ref, g_ref, h_ref):
    s = pl.program_id(1)
    tm, w = r_ref.shape
    n = RWKV_HEAD_DIM
    lc = RWKV_CHUNK
    gw = MXU_WIDTH
    hpg = gw // n
    groups = range(w // gw)

    @pl.when(s == 0)
    def _():
        t_ref[...] = jnp.zeros(t_ref.shape, F32)

    row_t = lax.broadcasted_iota(jnp.int32, (lc, gw), 0)
    col = lax.broadcasted_iota(jnp.int32, (lc, gw), 1)
    col_s = col & (n - 1)
    col_h = col >> (n.bit_length() - 1)
    strict = row_t > col_s
    incl = row_t >= col_s
    eye_m = row_t == col_s
    eye = eye_m.astype(F32)
    blk_r = lax.broadcasted_iota(jnp.int32, (gw, gw), 0) >> (n.bit_length() - 1)
    blk_c = lax.broadcasted_iota(jnp.int32, (gw, gw), 1) >> (n.bit_length() - 1)
    bd_mask = (blk_r == blk_c).astype(BF16)
    rows = _row_ids(lc, w)

    def bd(x):
        xb = x.astype(BF16)
        return jnp.concatenate([xb] * hpg, axis=0) * bd_mask

    def mm(a, b_bf16, dn=_NN):
        return lax.dot_general(a.astype(BF16), b_bf16, dn, preferred_element_type=F32)

    def diag_blocks(p):
        out = p[(hpg - 1) * n:, :]
        for h in range(hpg - 1):
            out = jnp.where(col_h == h, p[h * n:(h + 1) * n, :], out)
        return out

    def state_free(i, carry):
        at, rt, bt, kt, bh, kh, v, pend, where = [], [], [], [], [], [], [], [], []
        for ci in range(RWKV_INTERLEAVE):
            sl = pl.ds(pl.multiple_of((i * RWKV_INTERLEAVE + ci) * lc, lc), lc)
            lw = lw_ref[sl, :]
            cum = lw
            d = 1
            while d < lc:
                cum = cum + jnp.where(rows >= d, pltpu.roll(cum, d, axis=0), 0.0)
                d *= 2
            cl = cum[lc - 1:lc, :]
            einv = jnp.exp(-cum)
            eend = jnp.exp(cl - cum)
            a = a_ref[sl, :]
            b = b_ref[sl, :]
            k = k_ref[sl, :]
            full = (a * jnp.exp(cum - lw), r_ref[sl, :] * jnp.exp(cum), b * einv, k * einv, b * eend, k * eend,
                    v_ref[sl, :], jnp.exp(cl))
            for g in groups:
                for dst, t in zip((at, rt, bt, kt, bh, kh, v, pend), full):
                    dst.append(t[:, g * gw:(g + 1) * gw])
                where.append((sl, slice(g * gw, (g + 1) * gw)))
        units = range(len(where))
        lhs = [jnp.concatenate([at[u], rt[u]], axis=0).astype(BF16) for u in units]
        sb = [mm(lhs[u], bd(bt[u]), _NT) for u in units]
        sk = [mm(lhs[u], bd(kt[u]), _NT) for u in units]
        a_ab = [jnp.where(strict, t[:lc], 0.0) for t in sb]
        m_rb = [jnp.where(incl, t[lc:], 0.0) for t in sb]
        a_ak = [jnp.where(strict, t[:lc], 0.0) for t in sk]
        m_rk = [jnp.where(incl, t[lc:], 0.0) for t in sk]
        xm = [mm(jnp.concatenate([a_ak[u], m_rk[u]], axis=0), bd(v[u])) for u in units]
        x0 = [t[:lc] for t in xm]
        ykv = [t[lc:] for t in xm]
        inv = [eye + t for t in a_ab]
        apow = [mm(a_ab[u], bd(a_ab[u])) for u in units]
        span = 4
        while span < lc:
            p = [mm(jnp.concatenate([inv[u], apow[u]], axis=0), bd(apow[u])) for u in units]
            inv = [inv[u] + p[u][:lc] for u in units]
            apow = [t[lc:] for t in p]
            span *= 2
        inv = [inv[u] + mm(inv[u], bd(apow[u])) for u in units]
        ap = [mm(inv[u], bd(at[u])) for u in units]
        vp = [mm(inv[u], bd(x0[u])) for u in units]
        q = [rt[u] + mm(m_rb[u], bd(ap[u])) for u in units]
        yl = [mm(m_rb[u], bd(vp[u])) + ykv[u] for u in units]
        gm = [diag_blocks(mm(bh[u], ap[u].astype(BF16), _TN))
              + jnp.where(eye_m, jnp.broadcast_to(pend[u], (lc, gw)), 0.0) for u in units]
        hm = [diag_blocks(mm(jnp.concatenate([bh[u], kh[u]], axis=0),
                             jnp.concatenate([vp[u], v[u]], axis=0).astype(BF16), _TN)) for u in units]
        for u in units:
            q_ref[where[u]] = q[u]
            y_ref[where[u]] = yl[u]
            g_ref[where[u]] = gm[u]
            h_ref[where[u]] = hm[u]
        return carry

    lax.fori_loop(0, tm // (lc * RWKV_INTERLEAVE), state_free, 0)

    def state_pass(c, carry):
        sl = pl.ds(pl.multiple_of(c * lc, lc), lc)
        fin = [mm(jnp.concatenate([q_ref[sl, g * gw:(g + 1) * gw], g_ref[sl, g * gw:(g + 1) * gw]], axis=0),
                  bd(t_ref[:, g * gw:(g + 1) * gw])) for g in groups]
        for g in groups:
            gs = slice(g * gw, (g + 1) * gw)
            y_ref[sl, gs] = y_ref[sl, gs] + fin[g][:lc]
            t_ref[:, gs] = fin[g][lc:] + h_ref[sl, gs]
        return carry

    lax.fori_loop(0, tm // lc, state_pass, 0)


def _rwkv_scan(r, lw, k, v, a, b, nb, ns):
    m, w = r.shape
    assert RWKV_CHUNK == RWKV_HEAD_DIM and w % MXU_WIDTH == 0 and MXU_WIDTH % RWKV_HEAD_DIM == 0
    assert RWKV_HEAD_DIM & (RWKV_HEAD_DIM - 1) == 0
    row = pl.BlockSpec((ROW_TILE, w), lambda bb, s: (bb * ns + s, 0))
    return pl.pallas_call(
        _rwkv_kernel,
        grid=(nb, ns),
        in_specs=[row] * 6,
        out_specs=row,
        out_shape=jax.ShapeDtypeStruct((m, w), F32),
        scratch_shapes=[pltpu.VMEM((RWKV_HEAD_DIM, w), F32)]
        + [pltpu.VMEM((ROW_TILE, w), F32)] * 3,
        compiler_params=_cparams(),
        name="rwkv7",
    )(r, lw, k, v, a, b)


def _odd_post_kernel(x_ref, y_ref, g_ref, bonus_ref, yd_ref, gng_ref, gnb_ref, ones_ref, wout_ref,
                     lg_ref, lb_ref, o_ref):
    x = x_ref[...]
    y = y_ref[...]
    w = y.shape[-1]
    ones_bd = ones_ref[...]
    inv_n = 1.0 / RWKV_HEAD_DIM
    mu = _head_sum(y, ones_bd) * inv_n
    yc = y - mu
    var = _head_sum(yc * yc, ones_bd) * inv_n
    yn = yc * lax.rsqrt(var + RWKV_GN_EPS) * gng_ref[...] + gnb_ref[...]
    y_rwkv = (yn + bonus_ref[...]) * g_ref[...]
    out = _dot(y_rwkv, wout_ref[0:w, :]) + _dot(yd_ref[...], wout_ref[w:2 * w, :])
    o_ref[...] = _layer_norm(DEEPNORM_ALPHA * x + out, lg_ref[...], lb_ref[...])


def _odd_post(xf, y, g, bonus, yd, gn_g, gn_b, ones_bd, w_out, ln_g, ln_b, e, l, nb, ns):
    m, d = xf.shape
    w = y.shape[-1]
    row = pl.BlockSpec((ROW_TILE, w), lambda b, s: (b * ns + s, 0))
    xrow = pl.BlockSpec((ROW_TILE, d), lambda b, s: (b * ns + s, 0))
    return pl.pallas_call(
        _odd_post_kernel,
        grid=(nb, ns),
        in_specs=[xrow, row, row, row, row,
                  pl.BlockSpec((None, 1, w), lambda b, s: (e, 0, 0)),
                  pl.BlockSpec((None, 1, w), lambda b, s: (e, 0, 0)),
                  pl.BlockSpec(ones_bd.shape, lambda b, s: (0, 0)),
                  pl.BlockSpec((None, 2 * w, d), lambda b, s: (e, 0, 0)),
                  pl.BlockSpec((None, 1, d), lambda b, s: (4 * l + 1, 0, 0)),
                  pl.BlockSpec((None, 1, d), lambda b, s: (4 * l + 1, 0, 0))],
        out_specs=xrow,
        out_shape=jax.ShapeDtypeStruct((m, d), F32),
        compiler_params=_cparams(),
        name="odd_post",
    )(xf, y, g, bonus, yd, gn_g, gn_b, ones_bd, w_out, ln_g, ln_b)


def _kv_kernel(mem_ref, wkv_ref, kv_ref):
    kv_ref[...] = _dot(mem_ref[...], wkv_ref[...]).astype(BF16)


def _kv_proj(mem, w_kv):
    nb, mlen, d = mem.shape
    nl, _, n2 = w_kv.shape
    return pl.pallas_call(
        _kv_kernel,
        grid=(nl, nb),
        in_specs=[pl.BlockSpec((None, mlen, d), lambda l, b: (b, 0, 0)),
                  pl.BlockSpec((None, d, n2), lambda l, b: (l, 0, 0))],
        out_specs=pl.BlockSpec((None, None, mlen, n2), lambda l, b: (l, b, 0, 0)),
        out_shape=jax.ShapeDtypeStruct((nl, nb, mlen, n2), BF16),
        compiler_params=_cparams(),
        name="xattn_kv",
    )(mem, w_kv)


def _xattn_kernel(x_ref, wq_ref, kv_ref, wo_ref, g_ref, b_ref, o_ref):
    x = x_ref[...]
    d = x.shape[-1]
    hd = d // XATTN_HEADS
    q = _dot(x, wq_ref[...])
    outs = []
    for h in range(XATTN_HEADS):
        qh = q[:, h * hd:(h + 1) * hd].astype(BF16)
        kh = kv_ref[:, h * hd:(h + 1) * hd]
        vh = kv_ref[:, d + h * hd:d + (h + 1) * hd]
        sc = lax.dot_general(qh, kh, _NT, preferred_element_type=F32) * (hd ** -0.5)
        sc = sc - jnp.max(sc, axis=-1, keepdims=True)
        p = jnp.exp(sc)
        p = p / jnp.sum(p, axis=-1, keepdims=True)
        outs.append(jnp.dot(p.astype(BF16), vh, preferred_element_type=F32))
    o = jnp.concatenate(outs, axis=-1)
    out = _dot(o, wo_ref[...])
    o_ref[...] = _layer_norm(DEEPNORM_ALPHA * x + out, g_ref[...], b_ref[...])


def _xattn_layer(xf, w_q, kv, w_o, ln_g, ln_b, l, nb, ns):
    m, d = xf.shape
    mlen = kv.shape[2]
    xrow = pl.BlockSpec((ROW_TILE, d), lambda b, s: (b * ns + s, 0))
    return pl.pallas_call(
        _xattn_kernel,
        grid=(nb, ns),
        in_specs=[xrow,
                  pl.BlockSpec((None, d, d), lambda b, s: (l, 0, 0)),
                  pl.BlockSpec((None, None, mlen, 2 * d), lambda b, s: (l, b, 0, 0)),
                  pl.BlockSpec((None, d, d), lambda b, s: (l, 0, 0)),
                  pl.BlockSpec((None, 1, d), lambda b, s: (4 * l + 2, 0, 0)),
                  pl.BlockSpec((None, 1, d), lambda b, s: (4 * l + 2, 0, 0))],
        out_specs=xrow,
        out_shape=jax.ShapeDtypeStruct((m, d), F32),
        compiler_params=_cparams(),
        name="xattn",
    )(xf, w_q, kv, w_o, ln_g, ln_b)


def _block_diag(w):
    nblk, bi, bj = w.shape[-3:]
    eye = jnp.eye(nblk, dtype=w.dtype)
    out = w[..., :, :, None, :] * eye[:, None, :, None]
    return out.reshape(w.shape[:-3] + (nblk * bi, nblk * bj))


def kernel(x, mem, ffn1_w_in, ffn1_w_out, ffn2_w_in, ffn2_w_out, ln_g, ln_b, xattn_w_q, xattn_w_kv, xattn_w_o, even_w_in, even_w_out, pool_w, pool_scale, sgu_ln_g, sgu_ln_b, sgu_w, sgu_b, odd_w_in, odd_w_out, rwkv_mu, rwkv_w0, rwkv_w_up, rwkv_a0, rwkv_a_up, rwkv_g_up, rwkv_k_k, rwkv_k_a, rwkv_r_k, rwkv_gn_g, rwkv_gn_b, lru_conv_w, lru_conv_b, lru_w_a, lru_b_a, lru_w_x, lru_b_x, lru_lambda):
    nb, seq, d = x.shape
    assert seq % ROW_TILE == 0 and ROW_TILE % SGU_CHUNK == 0 and ROW_TILE % RWKV_CHUNK == 0
    ns = seq // ROW_TILE
    depth = ffn1_w_in.shape[0]
    xf = x.reshape(nb * seq, d)

    bf = lambda t: t.astype(BF16)
    row3 = lambda t: t.reshape(t.shape[0], 1, -1)
    lg = ln_g.reshape(-1, 1, d)
    lb = ln_b.reshape(-1, 1, d)
    f1_in, f1_out, f2_in, f2_out = bf(ffn1_w_in), bf(ffn1_w_out), bf(ffn2_w_in), bf(ffn2_w_out)
    wq, wo = bf(xattn_w_q), bf(xattn_w_o)
    kv = _kv_proj(mem, bf(xattn_w_kv))
    ev_in, ev_out = bf(even_w_in), bf(even_w_out)
    od_in, od_out = bf(odd_w_in), bf(odd_w_out)
    pw = bf(pool_w)
    sgu_bt = jnp.swapaxes(sgu_b, 1, 2)
    wa_bd, wx_bd = bf(_block_diag(lru_w_a)), bf(_block_diag(lru_w_x))
    w_rwkv = rwkv_w0.shape[-1]
    ones_bd = _block_diag(jnp.ones((w_rwkv // RWKV_HEAD_DIM, RWKV_HEAD_DIM, RWKV_HEAD_DIM), BF16))
    r_k = rwkv_r_k.reshape(rwkv_r_k.shape[0], 1, -1)

    for l in range(depth):
        xf = _ffn_layer(xf, f1_in, f1_out, lg, lb, l, 0, nb, ns)
        e = l // 2
        if l % 2 == 0:
            xf = _even_layer(xf, ev_in, ev_out, pw, row3(pool_scale), row3(sgu_ln_g), row3(sgu_ln_b),
                             sgu_w, sgu_bt, lg, lb, e, l, nb, ns)
        else:
            r, lw, k, v, a, b, g, bonus, yd = _odd_pre(
                xf, od_in, row3(rwkv_mu), row3(rwkv_w0), bf(rwkv_w_up), row3(rwkv_a0), bf(rwkv_a_up),
                bf(rwkv_g_up), row3(rwkv_k_k), row3(rwkv_k_a), r_k, lru_conv_w, row3(lru_conv_b),
                wa_bd, row3(lru_b_a), wx_bd, row3(lru_b_x), row3(lru_lambda), ones_bd, e, nb, ns)
            y = _rwkv_scan(r, lw, k, v, a, b, nb, ns)
            xf = _odd_post(xf, y, g, bonus, yd, row3(rwkv_gn_g), row3(rwkv_gn_b), ones_bd, od_out,
                           lg, lb, e, l, nb, ns)
        xf = _xattn_layer(xf, wq, kv, wo, lg, lb, l, nb, ns)
        xf = _ffn_layer(xf, f2_in, f2_out, lg, lb, l, 3, nb, ns)
    return xf.reshape(nb, seq, d)
```

```python
import functools

import jax
import jax.numpy as jnp
from jax import lax
from jax.experimental import pallas as pl
from jax.experimental.pallas import tpu as pltpu

F32 = jnp.float32
BF16 = jnp.bfloat16

DEPTH = 4
LN_EPS = 1e-5
DEEPNORM_ALPHA = (2 * DEPTH) ** 0.25
MACARON_WEIGHT = 0.5
POOL_WINDOWS = (2, 4, 8, 16)
POOL_HALO = 16
POOL_PACK = 2
SGU_CHUNK = 128
XATTN_HEADS = 4
RWKV_HEAD_DIM = 64
RWKV_CHUNK = 64
RWKV_GN_EPS = 64e-5
MXU_WIDTH = 256
RWKV_INTERLEAVE = 4
LRU_C = 8.0
CONV_WIDTH = 4
SUBLANES = 8
CONV_HALO = SUBLANES

ROW_TILE = 512
FFN_CHUNK = 256
FFN_SUBTILES = 2
SUBTILES = 2
VMEM_LIMIT = 56 * 1024 * 1024


def _cparams():
    return pltpu.CompilerParams(dimension_semantics=("arbitrary", "arbitrary"),
                                vmem_limit_bytes=VMEM_LIMIT)


def _dot(a, b):
    return jnp.dot(a.astype(BF16), b.astype(BF16), preferred_element_type=F32)


_NN = (((1,), (0,)), ((), ()))
_NT = (((1,), (1,)), ((), ()))
_TN = (((0,), (0,)), ((), ()))


def _mm1(a, b, dn=_NN):
    return lax.dot_general(a.astype(BF16), b.astype(BF16), dn, preferred_element_type=F32)


def _head_sum(x, ones_bd, pieces):
    total = None
    rest = x
    for i in range(pieces):
        p = rest.astype(BF16)
        if i + 1 < pieces:
            rest = rest - p.astype(F32)
        term = jnp.dot(p, ones_bd, preferred_element_type=F32)
        total = term if total is None else total + term
    return total


def _layer_norm(y, g, b, eps=LN_EPS):
    mu = jnp.mean(y, axis=-1, keepdims=True)
    yc = y - mu
    var = jnp.mean(yc * yc, axis=-1, keepdims=True)
    return yc * lax.rsqrt(var + eps) * g + b


def _softplus(t):
    return jnp.maximum(t, 0.0) + jnp.log(1.0 + jnp.exp(-jnp.abs(t)))


def _row_ids(rows, cols):
    return lax.broadcasted_iota(jnp.int32, (rows, cols), 0)


def _ffn_kernel(x_ref, win_ref, wout_ref, g_ref, b_ref, o_ref, h_ref, *, d_ff):
    sub = x_ref.shape[0] // FFN_SUBTILES
    accs = []
    for t in range(FFN_SUBTILES):
        rows = slice(t * sub, (t + 1) * sub)
        xb = x_ref[rows, :].astype(BF16)
        for c in range(d_ff // FFN_CHUNK):
            lo = c * FFN_CHUNK
            gate = jnp.dot(xb, win_ref[:, lo:lo + FFN_CHUNK], preferred_element_type=F32)
            up = jnp.dot(xb, win_ref[:, d_ff + lo:d_ff + lo + FFN_CHUNK], preferred_element_type=F32)
            h_ref[rows, lo:lo + FFN_CHUNK] = (gate * jax.nn.sigmoid(gate) * up).astype(BF16)
        accs.append(jnp.dot(h_ref[rows, :], wout_ref[...], preferred_element_type=F32))
    for t in range(FFN_SUBTILES):
        rows = slice(t * sub, (t + 1) * sub)
        y = DEEPNORM_ALPHA * x_ref[rows, :] + MACARON_WEIGHT * accs[t]
        o_ref[rows, :] = _layer_norm(y, g_ref[...], b_ref[...])


def _ffn_layer(xf, w_in, w_out, ln_g, ln_b, l, j, nb, ns):
    m, d = xf.shape
    d_ff = w_out.shape[1]
    return pl.pallas_call(
        functools.partial(_ffn_kernel, d_ff=d_ff),
        grid=(nb, ns),
        in_specs=[
            pl.BlockSpec((ROW_TILE, d), lambda b, s: (b * ns + s, 0)),
            pl.BlockSpec((None, d, 2 * d_ff), lambda b, s: (l, 0, 0)),
            pl.BlockSpec((None, d_ff, d), lambda b, s: (l, 0, 0)),
            pl.BlockSpec((None, 1, d), lambda b, s: (4 * l + j, 0, 0)),
            pl.BlockSpec((None, 1, d), lambda b, s: (4 * l + j, 0, 0)),
        ],
        out_specs=pl.BlockSpec((ROW_TILE, d), lambda b, s: (b * ns + s, 0)),
        out_shape=jax.ShapeDtypeStruct((m, d), F32),
        scratch_shapes=[pltpu.VMEM((ROW_TILE, d_ff), BF16)],
        compiler_params=_cparams(),
        name="ffn",
    )(xf, w_in, w_out, ln_g, ln_b)


def _even_kernel(x_ref, win_ref, wout_ref, poolw_ref, pscale_ref, sg_ref, sb_ref, sw_ref, sbt_ref,
                 g_ref, b_ref, o_ref, ext_ref):
    s = pl.program_id(1)
    tm = x_ref.shape[0]
    width = pscale_ref.shape[-1]
    pw = width // len(POOL_WINDOWS)
    heads = sw_ref.shape[0]
    hw = width // heads
    sub = tm // SUBTILES
    nchunk = sub // SGU_CHUNK
    tiles = [slice(t * sub, (t + 1) * sub) for t in range(SUBTILES)]
    subs = range(SUBTILES)

    @pl.when(s == 0)
    def _():
        ext_ref[0:POOL_HALO, :] = jnp.zeros((POOL_HALO, width), F32)

    h = [_dot(x_ref[rows, :], win_ref[...]) for rows in tiles]
    for t in subs:
        ext_ref[POOL_HALO + t * sub:POOL_HALO + (t + 1) * sub, :] = h[t][:, :width]

    ti = lax.broadcasted_iota(jnp.int32, (SGU_CHUNK, SGU_CHUNK), 0)
    si = lax.broadcasted_iota(jnp.int32, (SGU_CHUNK, SGU_CHUNK), 1)
    w_sgu = [jnp.where(ti >= si, sw_ref[hd], 0.0).astype(BF16) for hd in range(heads)]

    ya, yb = [], []
    for t in subs:
        ext = ext_ref[t * sub:(t + 1) * sub + POOL_HALO, :]
        pos = (s * tm + t * sub + 1 + _row_ids(sub, 1)).astype(F32)
        pooled = []
        for g, win in enumerate(POOL_WINDOWS):
            acc = ext[:, g * pw:(g + 1) * pw]
            d = 1
            while d < win:
                acc = acc + pltpu.roll(acc, d, axis=0)
                d *= 2
            mean = acc[POOL_HALO:, :] / jnp.minimum(pos, float(win))
            pooled.append(mean - h[t][:, g * pw:(g + 1) * pw])
        yat = [_dot(jnp.concatenate(pooled[p * POOL_PACK:(p + 1) * POOL_PACK], axis=-1), poolw_ref[p])
               for p in range(len(POOL_WINDOWS) // POOL_PACK)]
        ya.append(jnp.concatenate(yat, axis=-1) * pscale_ref[...])

        u = jax.nn.gelu(h[t][:, width:2 * width])
        v = jax.nn.gelu(h[t][:, 2 * width:3 * width])
        vb = _layer_norm(v, sg_ref[...], sb_ref[...]).astype(BF16)
        mixed = []
        for hd in range(heads):
            bias = sbt_ref[:, hd:hd + 1]
            vh = jnp.concatenate([vb[c * SGU_CHUNK:(c + 1) * SGU_CHUNK, hd * hw:(hd + 1) * hw]
                                  for c in range(nchunk)], axis=1)
            mixed.append(jnp.dot(w_sgu[hd], vh, preferred_element_type=F32) + bias)
        yb.append(u * jnp.concatenate(
            [jnp.concatenate([mixed[hd][:, c * hw:(c + 1) * hw] for hd in range(heads)], axis=1)
             for c in range(nchunk)], axis=0))

    out = [_dot(ya[t], wout_ref[0:width, :]) + _dot(yb[t], wout_ref[width:2 * width, :]) for t in subs]
    for t, rows in enumerate(tiles):
        o_ref[rows, :] = _layer_norm(DEEPNORM_ALPHA * x_ref[rows, :] + out[t], g_ref[...], b_ref[...])
    ext_ref[0:POOL_HALO, :] = ext_ref[tm:tm + POOL_HALO, :]


def _even_layer(xf, w_in, w_out, pool_w, pool_scale, sgu_g, sgu_b, sgu_w, sgu_bt, ln_g, ln_b, e, l, nb, ns):
    m, d = xf.shape
    n_in = w_in.shape[-1]
    width = pool_scale.shape[-1]
    full = lambda shape: pl.BlockSpec((None,) + shape, lambda b, s: (e,) + (0,) * len(shape))
    return pl.pallas_call(
        _even_kernel,
        grid=(nb, ns),
        in_specs=[
            pl.BlockSpec((ROW_TILE, d), lambda b, s: (b * ns + s, 0)),
            full((d, n_in)),
            full((2 * width, d)),
            full(pool_w.shape[1:]),
            full((1, width)),
            full((1, width)),
            full((1, width)),
            full(sgu_w.shape[1:]),
            full(sgu_bt.shape[1:]),
            pl.BlockSpec((None, 1, d), lambda b, s: (4 * l + 1, 0, 0)),
            pl.BlockSpec((None, 1, d), lambda b, s: (4 * l + 1, 0, 0)),
        ],
        out_specs=pl.BlockSpec((ROW_TILE, d), lambda b, s: (b * ns + s, 0)),
        out_shape=jax.ShapeDtypeStruct((m, d), F32),
        scratch_shapes=[pltpu.VMEM((POOL_HALO + ROW_TILE, width), F32)],
        compiler_params=_cparams(),
        name="even_mixer",
    )(xf, w_in, w_out, pool_w, pool_scale, sgu_g, sgu_b, sgu_w, sgu_bt, ln_g, ln_b)


def _odd_pre_kernel(x_ref, win_ref, mu_ref, w0_ref, wup_ref, a0_ref, aup_ref, gup_ref, kk_ref, ka_ref,
                    rk_ref, cw_ref, cb_ref, wa_ref, ba_ref, wx_ref, bx_ref, lam_ref, ones_ref,
                    r_o, lw_o, k_o, v_o, a_o, b_o, g_o, bonus_o, yd_o,
                    hprev_ref, xr_ref, hlru_ref):
    s = pl.program_id(1)
    tm = x_ref.shape[0]
    w = w0_ref.shape[-1]
    n_c = mu_ref.shape[-1]
    sub = tm // SUBTILES
    tiles = [slice(t * sub, (t + 1) * sub) for t in range(SUBTILES)]
    dr = wup_ref.shape[0]
    ar = aup_ref.shape[0]
    ones_bd = ones_ref[...]

    @pl.when(s == 0)
    def _():
        hprev_ref[...] = jnp.zeros(hprev_ref.shape, F32)
        xr_ref[0:CONV_HALO, :] = jnp.zeros((CONV_HALO, w), F32)
        hlru_ref[...] = jnp.zeros(hlru_ref.shape, F32)

    h = [_dot(x_ref[rows, :], win_ref[...]) for rows in tiles]
    for t in range(SUBTILES):
        xr_ref[CONV_HALO + t * sub:CONV_HALO + (t + 1) * sub, :] = h[t][:, n_c + w:]

    first = _row_ids(sub, 1) == 0
    ngrp = sub // SUBLANES
    sub_id = lax.broadcasted_iota(jnp.int32, (ngrp, SUBLANES, w), 1)
    hprev = hprev_ref[...]
    hcar = hlru_ref[...]
    for t, rows in enumerate(tiles):
        hc = h[t][:, :n_c]
        prev = jnp.where(first, hprev, pltpu.roll(hc, 1, axis=0))
        hprev = hc[sub - 1:sub, :]
        hc = hc + mu_ref[...] * (prev - hc)
        r = hc[:, 0:w]
        k = hc[:, w:2 * w]
        v = hc[:, 2 * w:3 * w]
        rest = hc[:, 3 * w:]
        wd = rest[:, 0:dr]
        ad = rest[:, dr:dr + ar]
        gd = rest[:, dr + ar:]

        z = w0_ref[...] + _dot(jnp.tanh(wd), wup_ref[...])
        logw = -_softplus(-z) - 0.5
        lw_o[rows, :] = -jnp.exp(logw)
        asig = jax.nn.sigmoid(a0_ref[...] + _dot(ad, aup_ref[...]))
        g_o[rows, :] = _dot(jax.nn.sigmoid(gd), gup_ref[...])
        kk = k * kk_ref[...]
        kk = kk * lax.rsqrt(jnp.maximum(_head_sum(kk * kk, ones_bd, 1), 1e-24))
        k2 = k * (1.0 + (asig - 1.0) * ka_ref[...])
        r_o[rows, :] = r
        k_o[rows, :] = k2
        v_o[rows, :] = v
        a_o[rows, :] = -kk
        b_o[rows, :] = kk * asig
        bonus_o[rows, :] = _head_sum(r * k2 * rk_ref[...], ones_bd, 2) * v

        gate = h[t][:, n_c:n_c + w]
        ext = xr_ref[t * sub:(t + 1) * sub + CONV_HALO, :]
        xc = cw_ref[CONV_WIDTH - 1:CONV_WIDTH, :] * ext[CONV_HALO:, :] + cb_ref[...]
        for j in range(1, CONV_WIDTH):
            xc = xc + cw_ref[CONV_WIDTH - 1 - j:CONV_WIDTH - j, :] * pltpu.roll(ext, j, axis=0)[CONV_HALO:, :]
        rec = jax.nn.sigmoid(_dot(xc, wa_ref[...]) + ba_ref[...])
        inp = jax.nn.sigmoid(_dot(xc, wx_ref[...]) + bx_ref[...])
        log_a = -LRU_C * rec * _softplus(-lam_ref[...])
        a = jnp.exp(log_a)
        bx = jnp.sqrt(-jnp.tanh(log_a) * (a * a + 1.0)) * (inp * xc)
        a3 = a.reshape(ngrp, SUBLANES, w)
        b3 = bx.reshape(ngrp, SUBLANES, w)
        d = 1
        while d < SUBLANES:
            keep = sub_id >= d
            a_sh = jnp.where(keep, pltpu.roll(a3, d, axis=1), 1.0)
            b_sh = jnp.where(keep, pltpu.roll(b3, d, axis=1), 0.0)
            b3 = a3 * b_sh + b3
            a3 = a3 * a_sh
            d *= 2
        gate_act = jax.nn.gelu(gate)
        for j in range(ngrp):
            hj = a3[j] * hcar + b3[j]
            lo = t * sub + j * SUBLANES
            yd_o[lo:lo + SUBLANES, :] = hj * gate_act[j * SUBLANES:(j + 1) * SUBLANES, :]
            hcar = hj[SUBLANES - 1:SUBLANES, :]
    hprev_ref[...] = hprev
    hlru_ref[...] = hcar
    xr_ref[0:CONV_HALO, :] = xr_ref[tm:tm + CONV_HALO, :]


def _odd_pre(xf, w_in, mu, w0, w_up, a0, a_up, g_up, k_k, k_a, r_k, conv_w, conv_b, wa_bd, b_a, wx_bd, b_x,
             lam, ones_bd, e, nb, ns):
    m, d = xf.shape
    w = w0.shape[-1]
    full = lambda arr: pl.BlockSpec((None,) + arr.shape[1:], lambda b, s: (e,) + (0,) * (arr.ndim - 1))
    row = pl.BlockSpec((ROW_TILE, w), lambda b, s: (b * ns + s, 0))
    params = (w_in, mu, w0, w_up, a0, a_up, g_up, k_k, k_a, r_k, conv_w, conv_b, wa_bd, b_a, wx_bd, b_x, lam)
    return pl.pallas_call(
        _odd_pre_kernel,
        grid=(nb, ns),
        in_specs=[pl.BlockSpec((ROW_TILE, d), lambda b, s: (b * ns + s, 0))]
        + [full(p) for p in params]
        + [pl.BlockSpec(ones_bd.shape, lambda b, s: (0, 0))],
        out_specs=[row] * 9,
        out_shape=[jax.ShapeDtypeStruct((m, w), F32)] * 9,
        scratch_shapes=[pltpu.VMEM((1, mu.shape[-1]), F32),
                        pltpu.VMEM((CONV_HALO + ROW_TILE, w), F32),
                        pltpu.VMEM((1, w), F32)],
        compiler_params=_cparams(),
        name="odd_pre",
    )(xf, *params, ones_bd)


def _rwkv_kernel(r_ref, lw_ref, k_ref, v_ref, a_ref, b_ref, y_ref, t_ref, q_ref, g_ref, h_ref):
    s = pl.program_id(1)
    tm, w = r_ref.shape
    n = RWKV_HEAD_DIM
    lc = RWKV_CHUNK
    gw = MXU_WIDTH
    hpg = gw // n
    groups = range(w // gw)

    @pl.when(s == 0)
    def _():
        t_ref[...] = jnp.zeros(t_ref.shape, F32)

    row_t = lax.broadcasted_iota(jnp.int32, (lc, gw), 0)
    col = lax.broadcasted_iota(jnp.int32, (lc, gw), 1)
    col_s = col & (n - 1)
    col_h = col >> (n.bit_length() - 1)
    strict = row_t > col_s
    incl = row_t >= col_s
    eye_m = row_t == col_s
    eye = eye_m.astype(F32)
    blk_r = lax.broadcasted_iota(jnp.int32, (gw, gw), 0) >> (n.bit_length() - 1)
    blk_c = lax.broadcasted_iota(jnp.int32, (gw, gw), 1) >> (n.bit_length() - 1)
    bd_mask = (blk_r == blk_c).astype(BF16)
    rows = _row_ids(lc, w)

    def bd(x):
        xb = x.astype(BF16)
        return jnp.concatenate([xb] * hpg, axis=0) * bd_mask

    def mm(a, b_bf16, dn=_NN):
        return lax.dot_general(a.astype(BF16), b_bf16, dn, preferred_element_type=F32)

    def diag_blocks(p):
        out = p[(hpg - 1) * n:, :]
        for h in range(hpg - 1):
            out = jnp.where(col_h == h, p[h * n:(h + 1) * n, :], out)
        return out

    def state_free(i, carry):
        at, rt, bt, kt, bh, kh, v, pend, where = [], [], [], [], [], [], [], [], []
        for ci in range(RWKV_INTERLEAVE):
            sl = pl.ds(pl.multiple_of((i * RWKV_INTERLEAVE + ci) * lc, lc), lc)
            lw = lw_ref[sl, :]
            cum = lw
            d = 1
            while d < lc:
                cum = cum + jnp.where(rows >= d, pltpu.roll(cum, d, axis=0), 0.0)
                d *= 2
            cl = cum[lc - 1:lc, :]
            einv = jnp.exp(-cum)
            eend = jnp.exp(cl - cum)
            a = a_ref[sl, :]
            b = b_ref[sl, :]
            k = k_ref[sl, :]
            full = (a * jnp.exp(cum - lw), r_ref[sl, :] * jnp.exp(cum), b * einv, k * einv, b * eend, k * eend,
                    v_ref[sl, :], jnp.exp(cl))
            for g in groups:
                for dst, t in zip((at, rt, bt, kt, bh, kh, v, pend), full):
                    dst.append(t[:, g * gw:(g + 1) * gw])
                where.append((sl, slice(g * gw, (g + 1) * gw)))
        units = range(len(where))
        lhs = [jnp.concatenate([at[u], rt[u]], axis=0).astype(BF16) for u in units]
        sb = [mm(lhs[u], bd(bt[u]), _NT) for u in units]
        sk = [mm(lhs[u], bd(kt[u]), _NT) for u in units]
        a_ab = [jnp.where(strict, t[:lc], 0.0) for t in sb]
        m_rb = [jnp.where(incl, t[lc:], 0.0) for t in sb]
        a_ak = [jnp.where(strict, t[:lc], 0.0) for t in sk]
        m_rk = [jnp.where(incl, t[lc:], 0.0) for t in sk]
        xm = [mm(jnp.concatenate([a_ak[u], m_rk[u]], axis=0), bd(v[u])) for u in units]
        x0 = [t[:lc] for t in xm]
        ykv = [t[lc:] for t in xm]
        inv = [eye + t for t in a_ab]
        apow = [mm(a_ab[u], bd(a_ab[u])) for u in units]
        span = 4
        while span < lc:
            p = [mm(jnp.concatenate([inv[u], apow[u]], axis=0), bd(apow[u])) for u in units]
            inv = [inv[u] + p[u][:lc] for u in units]
            apow = [t[lc:] for t in p]
            span *= 2
        inv = [inv[u] + mm(inv[u], bd(apow[u])) for u in units]
        ap = [mm(inv[u], bd(at[u])) for u in units]
        vp = [mm(inv[u], bd(x0[u])) for u in units]
        q = [rt[u] + mm(m_rb[u], bd(ap[u])) for u in units]
        yl = [mm(m_rb[u], bd(vp[u])) + ykv[u] for u in units]
        gm = [diag_blocks(mm(bh[u], ap[u].astype(BF16), _TN))
              + jnp.where(eye_m, jnp.broadcast_to(pend[u], (lc, gw)), 0.0) for u in units]
        hm = [diag_blocks(mm(jnp.concatenate([bh[u], kh[u]], axis=0),
                             jnp.concatenate([vp[u], v[u]], axis=0).astype(BF16), _TN)) for u in units]
        for u in units:
            q_ref[where[u]] = q[u]
            y_ref[where[u]] = yl[u]
            g_ref[where[u]] = gm[u]
            h_ref[where[u]] = hm[u]
        return carry

    lax.fori_loop(0, tm // (lc * RWKV_INTERLEAVE), state_free, 0)

    def state_pass(c, carry):
        sl = pl.ds(pl.multiple_of(c * lc, lc), lc)
        fin = [mm(jnp.concatenate([q_ref[sl, g * gw:(g + 1) * gw], g_ref[sl, g * gw:(g + 1) * gw]], axis=0),
                  bd(t_ref[:, g * gw:(g + 1) * gw])) for g in groups]
        for g in groups:
            gs = slice(g * gw, (g + 1) * gw)
            y_ref[sl, gs] = y_ref[sl, gs] + fin[g][:lc]
            t_ref[:, gs] = fin[g][lc:] + h_ref[sl, gs]
        return carry

    lax.fori_loop(0, tm // lc, state_pass, 0)


def _rwkv_scan(r, lw, k, v, a, b, nb, ns):
    m, w = r.shape
    assert RWKV_CHUNK == RWKV_HEAD_DIM and w % MXU_WIDTH == 0 and MXU_WIDTH % RWKV_HEAD_DIM == 0
    assert RWKV_HEAD_DIM & (RWKV_HEAD_DIM - 1) == 0
    row = pl.BlockSpec((ROW_TILE, w), lambda bb, s: (bb * ns + s, 0))
    return pl.pallas_call(
        _rwkv_kernel,
        grid=(nb, ns),
        in_specs=[row] * 6,
        out_specs=row,
        out_shape=jax.ShapeDtypeStruct((m, w), F32),
        scratch_shapes=[pltpu.VMEM((RWKV_HEAD_DIM, w), F32)]
        + [pltpu.VMEM((ROW_TILE, w), F32)] * 3,
        compiler_params=_cparams(),
        name="rwkv7",
    )(r, lw, k, v, a, b)


def _odd_post_kernel(x_ref, y_ref, g_ref, bonus_ref, yd_ref, gng_ref, gnb_ref, ones_ref, wout_ref,
                     lg_ref, lb_ref, o_ref):
    w = y_ref.shape[-1]
    sub = x_ref.shape[0] // SUBTILES
    tiles = [slice(t * sub, (t + 1) * sub) for t in range(SUBTILES)]
    ones_bd = ones_ref[...]
    inv_n = 1.0 / RWKV_HEAD_DIM
    y = [y_ref[rows, :] for rows in tiles]
    mu = [_head_sum(t, ones_bd, 2) * inv_n for t in y]
    yc = [a - b for a, b in zip(y, mu)]
    var = [_head_sum(t * t, ones_bd, 1) * inv_n for t in yc]
    out = []
    for t, rows in enumerate(tiles):
        yn = yc[t] * lax.rsqrt(var[t] + RWKV_GN_EPS) * gng_ref[...] + gnb_ref[...]
        y_rwkv = (yn + bonus_ref[rows, :]) * g_ref[rows, :]
        out.append(_dot(y_rwkv, wout_ref[0:w, :]) + _dot(yd_ref[rows, :], wout_ref[w:2 * w, :]))
    for t, rows in enumerate(tiles):
        o_ref[rows, :] = _layer_norm(DEEPNORM_ALPHA * x_ref[rows, :] + out[t], lg_ref[...], lb_ref[...])


def _odd_post(xf, y, g, bonus, yd, gn_g, gn_b, ones_bd, w_out, ln_g, ln_b, e, l, nb, ns):
    m, d = xf.shape
    w = y.shape[-1]
    row = pl.BlockSpec((ROW_TILE, w), lambda b, s: (b * ns + s, 0))
    xrow = pl.BlockSpec((ROW_TILE, d), lambda b, s: (b * ns + s, 0))
    return pl.pallas_call(
        _odd_post_kernel,
        grid=(nb, ns),
        in_specs=[xrow, row, row, row, row,
                  pl.BlockSpec((None, 1, w), lambda b, s: (e, 0, 0)),
                  pl.BlockSpec((None, 1, w), lambda b, s: (e, 0, 0)),
                  pl.BlockSpec(ones_bd.shape, lambda b, s: (0, 0)),
                  pl.BlockSpec((None, 2 * w, d), lambda b, s: (e, 0, 0)),
                  pl.BlockSpec((None, 1, d), lambda b, s: (4 * l + 1, 0, 0)),
                  pl.BlockSpec((None, 1, d), lambda b, s: (4 * l + 1, 0, 0))],
        out_specs=xrow,
        out_shape=jax.ShapeDtypeStruct((m, d), F32),
        compiler_params=_cparams(),
        name="odd_post",
    )(xf, y, g, bonus, yd, gn_g, gn_b, ones_bd, w_out, ln_g, ln_b)


def _kv_kernel(mem_ref, wkv_ref, kv_ref):
    kv_ref[...] = _dot(mem_ref[...], wkv_ref[...]).astype(BF16)


def _kv_proj(mem, w_kv):
    nb, mlen, d = mem.shape
    nl, _, n2 = w_kv.shape
    return pl.pallas_call(
        _kv_kernel,
        grid=(nl, nb),
        in_specs=[pl.BlockSpec((None, mlen, d), lambda l, b: (b, 0, 0)),
                  pl.BlockSpec((None, d, n2), lambda l, b: (l, 0, 0))],
        out_specs=pl.BlockSpec((None, None, mlen, n2), lambda l, b: (l, b, 0, 0)),
        out_shape=jax.ShapeDtypeStruct((nl, nb, mlen, n2), BF16),
        compiler_params=_cparams(),
        name="xattn_kv",
    )(mem, w_kv)


def _xattn_kernel(x_ref, wq_ref, kv_ref, wo_ref, g_ref, b_ref, o_ref):
    d = x_ref.shape[-1]
    hd = d // XATTN_HEADS
    sub = x_ref.shape[0] // SUBTILES
    tiles = [slice(t * sub, (t + 1) * sub) for t in range(SUBTILES)]
    q = [_dot(x_ref[rows, :], wq_ref[...]) for rows in tiles]
    outs = [[] for _ in tiles]
    for h in range(XATTN_HEADS):
        kh = kv_ref[:, h * hd:(h + 1) * hd]
        vh = kv_ref[:, d + h * hd:d + (h + 1) * hd]
        for t in range(SUBTILES):
            qh = q[t][:, h * hd:(h + 1) * hd].astype(BF16)
            sc = lax.dot_general(qh, kh, _NT, preferred_element_type=F32) * (hd ** -0.5)
            sc = sc - jnp.max(sc, axis=-1, keepdims=True)
            p = jnp.exp(sc)
            p = p / jnp.sum(p, axis=-1, keepdims=True)
            outs[t].append(jnp.dot(p.astype(BF16), vh, preferred_element_type=F32))
    proj = [_dot(jnp.concatenate(outs[t], axis=-1), wo_ref[...]) for t in range(SUBTILES)]
    for t, rows in enumerate(tiles):
        o_ref[rows, :] = _layer_norm(DEEPNORM_ALPHA * x_ref[rows, :] + proj[t], g_ref[...], b_ref[...])


def _xattn_layer(xf, w_q, kv, w_o, ln_g, ln_b, l, nb, ns):
    m, d = xf.shape
    mlen = kv.shape[2]
    xrow = pl.BlockSpec((ROW_TILE, d), lambda b, s: (b * ns + s, 0))
    return pl.pallas_call(
        _xattn_kernel,
        grid=(nb, ns),
        in_specs=[xrow,
                  pl.BlockSpec((None, d, d), lambda b, s: (l, 0, 0)),
                  pl.BlockSpec((None, None, mlen, 2 * d), lambda b, s: (l, b, 0, 0)),
                  pl.BlockSpec((None, d, d), lambda b, s: (l, 0, 0)),
                  pl.BlockSpec((None, 1, d), lambda b, s: (4 * l + 2, 0, 0)),
                  pl.BlockSpec((None, 1, d), lambda b, s: (4 * l + 2, 0, 0))],
        out_specs=xrow,
        out_shape=jax.ShapeDtypeStruct((m, d), F32),
        compiler_params=_cparams(),
        name="xattn",
    )(xf, w_q, kv, w_o, ln_g, ln_b)


def _block_diag(w):
    nblk, bi, bj = w.shape[-3:]
    eye = jnp.eye(nblk, dtype=w.dtype)
    out = w[..., :, :, None, :] * eye[:, None, :, None]
    return out.reshape(w.shape[:-3] + (nblk * bi, nblk * bj))


def kernel(x, mem, ffn1_w_in, ffn1_w_out, ffn2_w_in, ffn2_w_out, ln_g, ln_b, xattn_w_q, xattn_w_kv, xattn_w_o, even_w_in, even_w_out, pool_w, pool_scale, sgu_ln_g, sgu_ln_b, sgu_w, sgu_b, odd_w_in, odd_w_out, rwkv_mu, rwkv_w0, rwkv_w_up, rwkv_a0, rwkv_a_up, rwkv_g_up, rwkv_k_k, rwkv_k_a, rwkv_r_k, rwkv_gn_g, rwkv_gn_b, lru_conv_w, lru_conv_b, lru_w_a, lru_b_a, lru_w_x, lru_b_x, lru_lambda):
    nb, seq, d = x.shape
    assert seq % ROW_TILE == 0 and ROW_TILE % SGU_CHUNK == 0 and ROW_TILE % RWKV_CHUNK == 0
    ns = seq // ROW_TILE
    depth = ffn1_w_in.shape[0]
    xf = x.reshape(nb * seq, d)

    bf = lambda t: t.astype(BF16)
    row3 = lambda t: t.reshape(t.shape[0], 1, -1)
    lg = ln_g.reshape(-1, 1, d)
    lb = ln_b.reshape(-1, 1, d)
    f1_in, f1_out, f2_in, f2_out = bf(ffn1_w_in), bf(ffn1_w_out), bf(ffn2_w_in), bf(ffn2_w_out)
    wq, wo = bf(xattn_w_q), bf(xattn_w_o)
    kv = _kv_proj(mem, bf(xattn_w_kv))
    ev_in, ev_out = bf(even_w_in), bf(even_w_out)
    od_in, od_out = bf(odd_w_in), bf(odd_w_out)
    n_e, n_pool, pool_dim = pool_w.shape[:3]
    pw = bf(_block_diag(pool_w.reshape(n_e, n_pool // POOL_PACK, POOL_PACK, pool_dim, pool_dim)))
    sgu_bt = jnp.swapaxes(sgu_b, 1, 2)
    wa_bd, wx_bd = bf(_block_diag(lru_w_a)), bf(_block_diag(lru_w_x))
    w_rwkv = rwkv_w0.shape[-1]
    ones_bd = _block_diag(jnp.ones((w_rwkv // RWKV_HEAD_DIM, RWKV_HEAD_DIM, RWKV_HEAD_DIM), BF16))
    r_k = rwkv_r_k.reshape(rwkv_r_k.shape[0], 1, -1)

    for l in range(depth):
        xf = _ffn_layer(xf, f1_in, f1_out, lg, lb, l, 0, nb, ns)
        e = l // 2
        if l % 2 == 0:
            xf = _even_layer(xf, ev_in, ev_out, pw, row3(pool_scale), row3(sgu_ln_g), row3(sgu_ln_b),
                             sgu_w, sgu_bt, lg, lb, e, l, nb, ns)
        else:
            r, lw, k, v, a, b, g, bonus, yd = _odd_pre(
                xf, od_in, row3(rwkv_mu), row3(rwkv_w0), bf(rwkv_w_up), row3(rwkv_a0), bf(rwkv_a_up),
                bf(rwkv_g_up), row3(rwkv_k_k), row3(rwkv_k_a), r_k, lru_conv_w, row3(lru_conv_b),
                wa_bd, row3(lru_b_a), wx_bd, row3(lru_b_x), row3(lru_lambda), ones_bd, e, nb, ns)
            y = _rwkv_scan(r, lw, k, v, a, b, nb, ns)
            xf = _odd_post(xf, y, g, bonus, yd, row3(rwkv_gn_g), row3(rwkv_gn_b), ones_bd, od_out,
                           lg, lb, e, l, nb, ns)
        xf = _xattn_layer(xf, wq, kv, wo, lg, lb, l, nb, ns)
        xf = _ffn_layer(xf, f2_in, f2_out, lg, lb, l, 3, nb, ns)
    return xf.reshape(nb, seq, d)
```

```python
import functools

import jax
import jax.numpy as jnp
from jax import lax
from jax.experimental import pallas as pl
from jax.experimental.pallas import tpu as pltpu

F32 = jnp.float32
BF16 = jnp.bfloat16

DEPTH = 4
LN_EPS = 1e-5
DEEPNORM_ALPHA = (2 * DEPTH) ** 0.25
MACARON_WEIGHT = 0.5
POOL_WINDOWS = (2, 4, 8, 16)
POOL_HALO = 16
POOL_PACK = 2
SGU_CHUNK = 128
XATTN_HEADS = 4
RWKV_HEAD_DIM = 64
RWKV_CHUNK = 64
RWKV_GN_EPS = 64e-5
MXU_WIDTH = 256
RWKV_INTERLEAVE = 4
LRU_C = 8.0
CONV_WIDTH = 4
SUBLANES = 8
CONV_HALO = SUBLANES

ROW_TILE = 512
WIDE_TILE = 1024
SUB_ROWS = 256
FFN_CHUNK = 256
VMEM_LIMIT = 56 * 1024 * 1024


def _cparams(grid_rank=2):
    return pltpu.CompilerParams(dimension_semantics=("arbitrary",) * grid_rank,
                                vmem_limit_bytes=VMEM_LIMIT)


def _dot(a, b):
    return jnp.dot(a.astype(BF16), b.astype(BF16), preferred_element_type=F32)


_NN = (((1,), (0,)), ((), ()))
_NT = (((1,), (1,)), ((), ()))
_TN = (((0,), (0,)), ((), ()))


def _mm1(a, b, dn=_NN):
    return lax.dot_general(a.astype(BF16), b.astype(BF16), dn, preferred_element_type=F32)


def _head_sum(x, ones_bd, pieces):
    total = None
    rest = x
    for i in range(pieces):
        p = rest.astype(BF16)
        if i + 1 < pieces:
            rest = rest - p.astype(F32)
        term = jnp.dot(p, ones_bd, preferred_element_type=F32)
        total = term if total is None else total + term
    return total


def _layer_norm(y, g, b, eps=LN_EPS):
    mu = jnp.mean(y, axis=-1, keepdims=True)
    yc = y - mu
    var = jnp.mean(yc * yc, axis=-1, keepdims=True)
    return yc * lax.rsqrt(var + eps) * g + b


def _softplus(t):
    return jnp.maximum(t, 0.0) + jnp.log(1.0 + jnp.exp(-jnp.abs(t)))


def _row_ids(rows, cols):
    return lax.broadcasted_iota(jnp.int32, (rows, cols), 0)


def _ffn_kernel(x_ref, win_ref, wout_ref, g_ref, b_ref, o_ref, h_ref, *, d_ff):
    sub = SUB_ROWS
    nsub = x_ref.shape[0] // sub
    accs = []
    for t in range(nsub):
        rows = slice(t * sub, (t + 1) * sub)
        xb = x_ref[rows, :].astype(BF16)
        for c in range(d_ff // FFN_CHUNK):
            lo = c * FFN_CHUNK
            gate = jnp.dot(xb, win_ref[:, lo:lo + FFN_CHUNK], preferred_element_type=F32)
            up = jnp.dot(xb, win_ref[:, d_ff + lo:d_ff + lo + FFN_CHUNK], preferred_element_type=F32)
            h_ref[rows, lo:lo + FFN_CHUNK] = (gate * jax.nn.sigmoid(gate) * up).astype(BF16)
        accs.append(jnp.dot(h_ref[rows, :], wout_ref[...], preferred_element_type=F32))
    for t in range(nsub):
        rows = slice(t * sub, (t + 1) * sub)
        y = DEEPNORM_ALPHA * x_ref[rows, :] + MACARON_WEIGHT * accs[t]
        o_ref[rows, :] = _layer_norm(y, g_ref[...], b_ref[...])


def _ffn_layer(xf, w_in, w_out, ln_g, ln_b, l, j):
    m, d = xf.shape
    d_ff = w_out.shape[1]
    assert m % WIDE_TILE == 0
    once = pl.Buffered(1)
    return pl.pallas_call(
        functools.partial(_ffn_kernel, d_ff=d_ff),
        grid=(m // WIDE_TILE,),
        in_specs=[
            pl.BlockSpec((WIDE_TILE, d), lambda i: (i, 0)),
            pl.BlockSpec((None, d, 2 * d_ff), lambda i: (l, 0, 0), pipeline_mode=once),
            pl.BlockSpec((None, d_ff, d), lambda i: (l, 0, 0), pipeline_mode=once),
            pl.BlockSpec((None, 1, d), lambda i: (4 * l + j, 0, 0)),
            pl.BlockSpec((None, 1, d), lambda i: (4 * l + j, 0, 0)),
        ],
        out_specs=pl.BlockSpec((WIDE_TILE, d), lambda i: (i, 0)),
        out_shape=jax.ShapeDtypeStruct((m, d), F32),
        scratch_shapes=[pltpu.VMEM((WIDE_TILE, d_ff), BF16)],
        compiler_params=_cparams(1),
        name="ffn",
    )(xf, w_in, w_out, ln_g, ln_b)


def _even_kernel(x_ref, win_ref, wout_ref, poolw_ref, pscale_ref, sg_ref, sb_ref, sw_ref, sbt_ref,
                 g_ref, b_ref, o_ref, ext_ref):
    s = pl.program_id(1)
    tm = x_ref.shape[0]
    width = pscale_ref.shape[-1]
    pw = width // len(POOL_WINDOWS)
    heads = sw_ref.shape[0]
    hw = width // heads
    sub = SUB_ROWS
    nchunk = sub // SGU_CHUNK
    subs = range(tm // sub)
    tiles = [slice(t * sub, (t + 1) * sub) for t in subs]

    @pl.when(s == 0)
    def _():
        ext_ref[0:POOL_HALO, :] = jnp.zeros((POOL_HALO, width), F32)

    h = [_dot(x_ref[rows, :], win_ref[...]) for rows in tiles]
    for t in subs:
        ext_ref[POOL_HALO + t * sub:POOL_HALO + (t + 1) * sub, :] = h[t][:, :width]

    ti = lax.broadcasted_iota(jnp.int32, (SGU_CHUNK, SGU_CHUNK), 0)
    si = lax.broadcasted_iota(jnp.int32, (SGU_CHUNK, SGU_CHUNK), 1)
    w_sgu = [jnp.where(ti >= si, sw_ref[hd], 0.0).astype(BF16) for hd in range(heads)]

    ya, yb = [], []
    for t in subs:
        ext = ext_ref[t * sub:(t + 1) * sub + POOL_HALO, :]
        pos = (s * tm + t * sub + 1 + _row_ids(sub, 1)).astype(F32)
        pooled = []
        for g, win in enumerate(POOL_WINDOWS):
            acc = ext[:, g * pw:(g + 1) * pw]
            d = 1
            while d < win:
                acc = acc + pltpu.roll(acc, d, axis=0)
                d *= 2
            mean = acc[POOL_HALO:, :] / jnp.minimum(pos, float(win))
            pooled.append(mean - h[t][:, g * pw:(g + 1) * pw])
        yat = [_dot(jnp.concatenate(pooled[p * POOL_PACK:(p + 1) * POOL_PACK], axis=-1), poolw_ref[p])
               for p in range(len(POOL_WINDOWS) // POOL_PACK)]
        ya.append(jnp.concatenate(yat, axis=-1) * pscale_ref[...])

        u = jax.nn.gelu(h[t][:, width:2 * width])
        v = jax.nn.gelu(h[t][:, 2 * width:3 * width])
        vb = _layer_norm(v, sg_ref[...], sb_ref[...]).astype(BF16)
        mixed = []
        for hd in range(heads):
            bias = sbt_ref[:, hd:hd + 1]
            vh = jnp.concatenate([vb[c * SGU_CHUNK:(c + 1) * SGU_CHUNK, hd * hw:(hd + 1) * hw]
                                  for c in range(nchunk)], axis=1)
            mixed.append(jnp.dot(w_sgu[hd], vh, preferred_element_type=F32) + bias)
        yb.append(u * jnp.concatenate(
            [jnp.concatenate([mixed[hd][:, c * hw:(c + 1) * hw] for hd in range(heads)], axis=1)
             for c in range(nchunk)], axis=0))

    out = [_dot(ya[t], wout_ref[0:width, :]) + _dot(yb[t], wout_ref[width:2 * width, :]) for t in subs]
    for t, rows in enumerate(tiles):
        o_ref[rows, :] = _layer_norm(DEEPNORM_ALPHA * x_ref[rows, :] + out[t], g_ref[...], b_ref[...])
    ext_ref[0:POOL_HALO, :] = ext_ref[tm:tm + POOL_HALO, :]


def _even_layer(xf, w_in, w_out, pool_w, pool_scale, sgu_g, sgu_b, sgu_w, sgu_bt, ln_g, ln_b, e, l, nb, ns):
    m, d = xf.shape
    n_in = w_in.shape[-1]
    width = pool_scale.shape[-1]
    full = lambda shape: pl.BlockSpec((None,) + shape, lambda b, s: (e,) + (0,) * len(shape))
    return pl.pallas_call(
        _even_kernel,
        grid=(nb, ns),
        in_specs=[
            pl.BlockSpec((WIDE_TILE, d), lambda b, s: (b * ns + s, 0)),
            full((d, n_in)),
            full((2 * width, d)),
            full(pool_w.shape[1:]),
            full((1, width)),
            full((1, width)),
            full((1, width)),
            full(sgu_w.shape[1:]),
            full(sgu_bt.shape[1:]),
            pl.BlockSpec((None, 1, d), lambda b, s: (4 * l + 1, 0, 0)),
            pl.BlockSpec((None, 1, d), lambda b, s: (4 * l + 1, 0, 0)),
        ],
        out_specs=pl.BlockSpec((WIDE_TILE, d), lambda b, s: (b * ns + s, 0)),
        out_shape=jax.ShapeDtypeStruct((m, d), F32),
        scratch_shapes=[pltpu.VMEM((POOL_HALO + WIDE_TILE, width), F32)],
        compiler_params=_cparams(),
        name="even_mixer",
    )(xf, w_in, w_out, pool_w, pool_scale, sgu_g, sgu_b, sgu_w, sgu_bt, ln_g, ln_b)


def _odd_pre_kernel(x_ref, win_ref, mu_ref, w0_ref, wup_ref, a0_ref, aup_ref, gup_ref, kk_ref, ka_ref,
                    rk_ref, cw_ref, cb_ref, wa_ref, ba_ref, wx_ref, bx_ref, lam_ref, ones_ref,
                    r_o, lw_o, k_o, v_o, a_o, b_o, g_o, bonus_o, yd_o,
                    hprev_ref, xr_ref, hlru_ref):
    s = pl.program_id(1)
    tm = x_ref.shape[0]
    w = w0_ref.shape[-1]
    n_c = mu_ref.shape[-1]
    sub = SUB_ROWS
    nsub = tm // sub
    tiles = [slice(t * sub, (t + 1) * sub) for t in range(nsub)]
    dr = wup_ref.shape[0]
    ar = aup_ref.shape[0]
    ones_bd = ones_ref[...]

    @pl.when(s == 0)
    def _():
        hprev_ref[...] = jnp.zeros(hprev_ref.shape, F32)
        xr_ref[0:CONV_HALO, :] = jnp.zeros((CONV_HALO, w), F32)
        hlru_ref[...] = jnp.zeros(hlru_ref.shape, F32)

    h = [_dot(x_ref[rows, :], win_ref[...]) for rows in tiles]
    for t in range(nsub):
        xr_ref[CONV_HALO + t * sub:CONV_HALO + (t + 1) * sub, :] = h[t][:, n_c + w:]

    first = _row_ids(sub, 1) == 0
    ngrp = sub // SUBLANES
    sub_id = lax.broadcasted_iota(jnp.int32, (ngrp, SUBLANES, w), 1)
    hprev = hprev_ref[...]
    hcar = hlru_ref[...]
    for t, rows in enumerate(tiles):
        hc = h[t][:, :n_c]
        prev = jnp.where(first, hprev, pltpu.roll(hc, 1, axis=0))
        hprev = hc[sub - 1:sub, :]
        hc = hc + mu_ref[...] * (prev - hc)
        r = hc[:, 0:w]
        k = hc[:, w:2 * w]
        v = hc[:, 2 * w:3 * w]
        rest = hc[:, 3 * w:]
        wd = rest[:, 0:dr]
        ad = rest[:, dr:dr + ar]
        gd = rest[:, dr + ar:]

        z = w0_ref[...] + _dot(jnp.tanh(wd), wup_ref[...])
        logw = -_softplus(-z) - 0.5
        lw_o[rows, :] = -jnp.exp(logw)
        asig = jax.nn.sigmoid(a0_ref[...] + _dot(ad, aup_ref[...]))
        g_o[rows, :] = _dot(jax.nn.sigmoid(gd), gup_ref[...])
        kk = k * kk_ref[...]
        kk = kk * lax.rsqrt(jnp.maximum(_head_sum(kk * kk, ones_bd, 1), 1e-24))
        k2 = k * (1.0 + (asig - 1.0) * ka_ref[...])
        r_o[rows, :] = r
        k_o[rows, :] = k2
        v_o[rows, :] = v
        a_o[rows, :] = -kk
        b_o[rows, :] = kk * asig
        bonus_o[rows, :] = _head_sum(r * k2 * rk_ref[...], ones_bd, 2) * v

        gate = h[t][:, n_c:n_c + w]
        ext = xr_ref[t * sub:(t + 1) * sub + CONV_HALO, :]
        xc = cw_ref[CONV_WIDTH - 1:CONV_WIDTH, :] * ext[CONV_HALO:, :] + cb_ref[...]
        for j in range(1, CONV_WIDTH):
            xc = xc + cw_ref[CONV_WIDTH - 1 - j:CONV_WIDTH - j, :] * pltpu.roll(ext, j, axis=0)[CONV_HALO:, :]
        rec = jax.nn.sigmoid(_dot(xc, wa_ref[...]) + ba_ref[...])
        inp = jax.nn.sigmoid(_dot(xc, wx_ref[...]) + bx_ref[...])
        log_a = -LRU_C * rec * _softplus(-lam_ref[...])
        a = jnp.exp(log_a)
        bx = jnp.sqrt(-jnp.tanh(log_a) * (a * a + 1.0)) * (inp * xc)
        a3 = a.reshape(ngrp, SUBLANES, w)
        b3 = bx.reshape(ngrp, SUBLANES, w)
        d = 1
        while d < SUBLANES:
            keep = sub_id >= d
            a_sh = jnp.where(keep, pltpu.roll(a3, d, axis=1), 1.0)
            b_sh = jnp.where(keep, pltpu.roll(b3, d, axis=1), 0.0)
            b3 = a3 * b_sh + b3
            a3 = a3 * a_sh
            d *= 2
        gate_act = jax.nn.gelu(gate)
        for j in range(ngrp):
            hj = a3[j] * hcar + b3[j]
            lo = t * sub + j * SUBLANES
            yd_o[lo:lo + SUBLANES, :] = hj * gate_act[j * SUBLANES:(j + 1) * SUBLANES, :]
            hcar = hj[SUBLANES - 1:SUBLANES, :]
    hprev_ref[...] = hprev
    hlru_ref[...] = hcar
    xr_ref[0:CONV_HALO, :] = xr_ref[tm:tm + CONV_HALO, :]


def _odd_pre(xf, w_in, mu, w0, w_up, a0, a_up, g_up, k_k, k_a, r_k, conv_w, conv_b, wa_bd, b_a, wx_bd, b_x,
             lam, ones_bd, e, nb, ns):
    m, d = xf.shape
    w = w0.shape[-1]
    full = lambda arr: pl.BlockSpec((None,) + arr.shape[1:], lambda b, s: (e,) + (0,) * (arr.ndim - 1))
    row = pl.BlockSpec((ROW_TILE, w), lambda b, s: (b * ns + s, 0))
    params = (w_in, mu, w0, w_up, a0, a_up, g_up, k_k, k_a, r_k, conv_w, conv_b, wa_bd, b_a, wx_bd, b_x, lam)
    return pl.pallas_call(
        _odd_pre_kernel,
        grid=(nb, ns),
        in_specs=[pl.BlockSpec((ROW_TILE, d), lambda b, s: (b * ns + s, 0))]
        + [full(p) for p in params]
        + [pl.BlockSpec(ones_bd.shape, lambda b, s: (0, 0))],
        out_specs=[row] * 9,
        out_shape=[jax.ShapeDtypeStruct((m, w), F32)] * 9,
        scratch_shapes=[pltpu.VMEM((1, mu.shape[-1]), F32),
                        pltpu.VMEM((CONV_HALO + ROW_TILE, w), F32),
                        pltpu.VMEM((1, w), F32)],
        compiler_params=_cparams(),
        name="odd_pre",
    )(xf, *params, ones_bd)


def _rwkv_kernel(r_ref, lw_ref, k_ref, v_ref, a_ref, b_ref, y_ref, t_ref, q_ref, g_ref, h_ref):
    s = pl.program_id(1)
    tm, w = r_ref.shape
    n = RWKV_HEAD_DIM
    lc = RWKV_CHUNK
    gw = MXU_WIDTH
    hpg = gw // n
    groups = range(w // gw)

    @pl.when(s == 0)
    def _():
        t_ref[...] = jnp.zeros(t_ref.shape, F32)

    row_t = lax.broadcasted_iota(jnp.int32, (lc, gw), 0)
    col = lax.broadcasted_iota(jnp.int32, (lc, gw), 1)
    col_s = col & (n - 1)
    col_h = col >> (n.bit_length() - 1)
    strict = row_t > col_s
    incl = row_t >= col_s
    eye_m = row_t == col_s
    eye = eye_m.astype(F32)
    blk_r = lax.broadcasted_iota(jnp.int32, (gw, gw), 0) >> (n.bit_length() - 1)
    blk_c = lax.broadcasted_iota(jnp.int32, (gw, gw), 1) >> (n.bit_length() - 1)
    bd_mask = (blk_r == blk_c).astype(BF16)
    rows = _row_ids(lc, w)

    def bd(x):
        xb = x.astype(BF16)
        return jnp.concatenate([xb] * hpg, axis=0) * bd_mask

    def mm(a, b_bf16, dn=_NN):
        return lax.dot_general(a.astype(BF16), b_bf16, dn, preferred_element_type=F32)

    def diag_blocks(p):
        out = p[(hpg - 1) * n:, :]
        for h in range(hpg - 1):
            out = jnp.where(col_h == h, p[h * n:(h + 1) * n, :], out)
        return out

    def state_free(i, carry):
        at, rt, bt, kt, bh, kh, v, pend, where = [], [], [], [], [], [], [], [], []
        for ci in range(RWKV_INTERLEAVE):
            sl = pl.ds((i * RWKV_INTERLEAVE + ci) * lc, lc)
            lw = lw_ref[sl, :]
            cum = lw
            d = 1
            while d < lc:
                cum = cum + jnp.where(rows >= d, pltpu.roll(cum, d, axis=0), 0.0)
                d *= 2
            cl = cum[lc - 1:lc, :]
            einv = jnp.exp(-cum)
            eend = jnp.exp(cl - cum)
            a = a_ref[sl, :]
            b = b_ref[sl, :]
            k = k_ref[sl, :]
            full = (a * jnp.exp(cum - lw), r_ref[sl, :] * jnp.exp(cum), b * einv, k * einv, b * eend, k * eend,
                    v_ref[sl, :], jnp.exp(cl))
            for g in groups:
                for dst, t in zip((at, rt, bt, kt, bh, kh, v, pend), full):
                    dst.append(t[:, g * gw:(g + 1) * gw])
                where.append((sl, slice(g * gw, (g + 1) * gw)))
        units = range(len(where))
        lhs = [jnp.concatenate([at[u], rt[u]], axis=0).astype(BF16) for u in units]
        sb = [mm(lhs[u], bd(bt[u]), _NT) for u in units]
        sk = [mm(lhs[u], bd(kt[u]), _NT) for u in units]
        a_ab = [jnp.where(strict, t[:lc], 0.0) for t in sb]
        m_rb = [jnp.where(incl, t[lc:], 0.0) for t in sb]
        a_ak = [jnp.where(strict, t[:lc], 0.0) for t in sk]
        m_rk = [jnp.where(incl, t[lc:], 0.0) for t in sk]
        xm = [mm(jnp.concatenate([a_ak[u], m_rk[u]], axis=0), bd(v[u])) for u in units]
        x0 = [t[:lc] for t in xm]
        ykv = [t[lc:] for t in xm]
        inv = [eye + t for t in a_ab]
        apow = [mm(a_ab[u], bd(a_ab[u])) for u in units]
        span = 4
        while span < lc:
            p = [mm(jnp.concatenate([inv[u], apow[u]], axis=0), bd(apow[u])) for u in units]
            inv = [inv[u] + p[u][:lc] for u in units]
            apow = [t[lc:] for t in p]
            span *= 2
        inv = [inv[u] + mm(inv[u], bd(apow[u])) for u in units]
        ap = [mm(inv[u], bd(at[u])) for u in units]
        vp = [mm(inv[u], bd(x0[u])) for u in units]
        q = [rt[u] + mm(m_rb[u], bd(ap[u])) for u in units]
        yl = [mm(m_rb[u], bd(vp[u])) + ykv[u] for u in units]
        gm = [diag_blocks(mm(bh[u], ap[u].astype(BF16), _TN))
              + jnp.where(eye_m, jnp.broadcast_to(pend[u], (lc, gw)), 0.0) for u in units]
        hm = [diag_blocks(mm(jnp.concatenate([bh[u], kh[u]], axis=0),
                             jnp.concatenate([vp[u], v[u]], axis=0).astype(BF16), _TN)) for u in units]
        for u in units:
            q_ref[where[u]] = q[u]
            y_ref[where[u]] = yl[u]
            g_ref[where[u]] = gm[u]
            h_ref[where[u]] = hm[u]
        return carry

    for i in range(tm // (lc * RWKV_INTERLEAVE)):
        state_free(i, 0)

    def state_pass(c, carry):
        sl = pl.ds(pl.multiple_of(c * lc, lc), lc)
        fin = [mm(jnp.concatenate([q_ref[sl, g * gw:(g + 1) * gw], g_ref[sl, g * gw:(g + 1) * gw]], axis=0),
                  bd(t_ref[:, g * gw:(g + 1) * gw])) for g in groups]
        for g in groups:
            gs = slice(g * gw, (g + 1) * gw)
            y_ref[sl, gs] = y_ref[sl, gs] + fin[g][:lc]
            t_ref[:, gs] = fin[g][lc:] + h_ref[sl, gs]
        return carry

    lax.fori_loop(0, tm // lc, state_pass, 0)


def _rwkv_scan(r, lw, k, v, a, b, nb, ns):
    m, w = r.shape
    assert RWKV_CHUNK == RWKV_HEAD_DIM and w % MXU_WIDTH == 0 and MXU_WIDTH % RWKV_HEAD_DIM == 0
    assert RWKV_HEAD_DIM & (RWKV_HEAD_DIM - 1) == 0
    row = pl.BlockSpec((ROW_TILE, w), lambda bb, s: (bb * ns + s, 0))
    return pl.pallas_call(
        _rwkv_kernel,
        grid=(nb, ns),
        in_specs=[row] * 6,
        out_specs=row,
        out_shape=jax.ShapeDtypeStruct((m, w), F32),
        scratch_shapes=[pltpu.VMEM((RWKV_HEAD_DIM, w), F32)]
        + [pltpu.VMEM((ROW_TILE, w), F32)] * 3,
        compiler_params=_cparams(),
        name="rwkv7",
    )(r, lw, k, v, a, b)


def _odd_post_kernel(x_ref, y_ref, g_ref, bonus_ref, yd_ref, gng_ref, gnb_ref, ones_ref, wout_ref,
                     lg_ref, lb_ref, o_ref):
    w = y_ref.shape[-1]
    sub = SUB_ROWS
    tiles = [slice(t * sub, (t + 1) * sub) for t in range(x_ref.shape[0] // sub)]
    ones_bd = ones_ref[...]
    inv_n = 1.0 / RWKV_HEAD_DIM
    y = [y_ref[rows, :] for rows in tiles]
    mu = [_head_sum(t, ones_bd, 2) * inv_n for t in y]
    yc = [a - b for a, b in zip(y, mu)]
    var = [_head_sum(t * t, ones_bd, 1) * inv_n for t in yc]
    out = []
    for t, rows in enumerate(tiles):
        yn = yc[t] * lax.rsqrt(var[t] + RWKV_GN_EPS) * gng_ref[...] + gnb_ref[...]
        y_rwkv = (yn + bonus_ref[rows, :]) * g_ref[rows, :]
        out.append(_dot(y_rwkv, wout_ref[0:w, :]) + _dot(yd_ref[rows, :], wout_ref[w:2 * w, :]))
    for t, rows in enumerate(tiles):
        o_ref[rows, :] = _layer_norm(DEEPNORM_ALPHA * x_ref[rows, :] + out[t], lg_ref[...], lb_ref[...])


def _odd_post(xf, y, g, bonus, yd, gn_g, gn_b, ones_bd, w_out, ln_g, ln_b, e, l, nb, ns):
    m, d = xf.shape
    w = y.shape[-1]
    row = pl.BlockSpec((WIDE_TILE, w), lambda b, s: (b * ns + s, 0))
    xrow = pl.BlockSpec((WIDE_TILE, d), lambda b, s: (b * ns + s, 0))
    return pl.pallas_call(
        _odd_post_kernel,
        grid=(nb, ns),
        in_specs=[xrow, row, row, row, row,
                  pl.BlockSpec((None, 1, w), lambda b, s: (e, 0, 0)),
                  pl.BlockSpec((None, 1, w), lambda b, s: (e, 0, 0)),
                  pl.BlockSpec(ones_bd.shape, lambda b, s: (0, 0)),
                  pl.BlockSpec((None, 2 * w, d), lambda b, s: (e, 0, 0)),
                  pl.BlockSpec((None, 1, d), lambda b, s: (4 * l + 1, 0, 0)),
                  pl.BlockSpec((None, 1, d), lambda b, s: (4 * l + 1, 0, 0))],
        out_specs=xrow,
        out_shape=jax.ShapeDtypeStruct((m, d), F32),
        compiler_params=_cparams(),
        name="odd_post",
    )(xf, y, g, bonus, yd, gn_g, gn_b, ones_bd, w_out, ln_g, ln_b)


def _kv_kernel(mem_ref, wkv_ref, kv_ref):
    kv_ref[...] = _dot(mem_ref[...], wkv_ref[...]).astype(BF16)


def _kv_proj(mem, w_kv):
    nb, mlen, d = mem.shape
    nl, _, n2 = w_kv.shape
    return pl.pallas_call(
        _kv_kernel,
        grid=(nl, nb),
        in_specs=[pl.BlockSpec((None, mlen, d), lambda l, b: (b, 0, 0)),
                  pl.BlockSpec((None, d, n2), lambda l, b: (l, 0, 0))],
        out_specs=pl.BlockSpec((None, None, mlen, n2), lambda l, b: (l, b, 0, 0)),
        out_shape=jax.ShapeDtypeStruct((nl, nb, mlen, n2), BF16),
        compiler_params=_cparams(),
        name="xattn_kv",
    )(mem, w_kv)


def _xattn_kernel(x_ref, wq_ref, kv_ref, wo_ref, g_ref, b_ref, o_ref):
    d = x_ref.shape[-1]
    hd = d // XATTN_HEADS
    sub = SUB_ROWS
    nsub = x_ref.shape[0] // sub
    tiles = [slice(t * sub, (t + 1) * sub) for t in range(nsub)]
    q = [_dot(x_ref[rows, :], wq_ref[...]) for rows in tiles]
    outs = [[] for _ in tiles]
    for h in range(XATTN_HEADS):
        kh = kv_ref[:, h * hd:(h + 1) * hd]
        vh = kv_ref[:, d + h * hd:d + (h + 1) * hd]
        for t in range(nsub):
            qh = q[t][:, h * hd:(h + 1) * hd].astype(BF16)
            sc = lax.dot_general(qh, kh, _NT, preferred_element_type=F32) * (hd ** -0.5)
            sc = sc - jnp.max(sc, axis=-1, keepdims=True)
            p = jnp.exp(sc)
            p = p / jnp.sum(p, axis=-1, keepdims=True)
            outs[t].append(jnp.dot(p.astype(BF16), vh, preferred_element_type=F32))
    proj = [_dot(jnp.concatenate(outs[t], axis=-1), wo_ref[...]) for t in range(nsub)]
    for t, rows in enumerate(tiles):
        o_ref[rows, :] = _layer_norm(DEEPNORM_ALPHA * x_ref[rows, :] + proj[t], g_ref[...], b_ref[...])


def _xattn_layer(xf, w_q, kv, w_o, ln_g, ln_b, l, nb, ns):
    m, d = xf.shape
    mlen = kv.shape[2]
    xrow = pl.BlockSpec((WIDE_TILE, d), lambda b, s: (b * ns + s, 0))
    return pl.pallas_call(
        _xattn_kernel,
        grid=(nb, ns),
        in_specs=[xrow,
                  pl.BlockSpec((None, d, d), lambda b, s: (l, 0, 0)),
                  pl.BlockSpec((None, None, mlen, 2 * d), lambda b, s: (l, b, 0, 0)),
                  pl.BlockSpec((None, d, d), lambda b, s: (l, 0, 0)),
                  pl.BlockSpec((None, 1, d), lambda b, s: (4 * l + 2, 0, 0)),
                  pl.BlockSpec((None, 1, d), lambda b, s: (4 * l + 2, 0, 0))],
        out_specs=xrow,
        out_shape=jax.ShapeDtypeStruct((m, d), F32),
        compiler_params=_cparams(),
        name="xattn",
    )(xf, w_q, kv, w_o, ln_g, ln_b)


def _block_diag(w):
    nblk, bi, bj = w.shape[-3:]
    eye = jnp.eye(nblk, dtype=w.dtype)
    out = w[..., :, :, None, :] * eye[:, None, :, None]
    return out.reshape(w.shape[:-3] + (nblk * bi, nblk * bj))


def kernel(x, mem, ffn1_w_in, ffn1_w_out, ffn2_w_in, ffn2_w_out, ln_g, ln_b, xattn_w_q, xattn_w_kv, xattn_w_o, even_w_in, even_w_out, pool_w, pool_scale, sgu_ln_g, sgu_ln_b, sgu_w, sgu_b, odd_w_in, odd_w_out, rwkv_mu, rwkv_w0, rwkv_w_up, rwkv_a0, rwkv_a_up, rwkv_g_up, rwkv_k_k, rwkv_k_a, rwkv_r_k, rwkv_gn_g, rwkv_gn_b, lru_conv_w, lru_conv_b, lru_w_a, lru_b_a, lru_w_x, lru_b_x, lru_lambda):
    nb, seq, d = x.shape
    assert seq % WIDE_TILE == 0 and WIDE_TILE % ROW_TILE == 0 and ROW_TILE % SUB_ROWS == 0
    assert SUB_ROWS % SGU_CHUNK == 0 and ROW_TILE % (RWKV_CHUNK * RWKV_INTERLEAVE) == 0
    ns = seq // ROW_TILE
    nsw = seq // WIDE_TILE
    depth = ffn1_w_in.shape[0]
    xf = x.reshape(nb * seq, d)

    bf = lambda t: t.astype(BF16)
    row3 = lambda t: t.reshape(t.shape[0], 1, -1)
    lg = ln_g.reshape(-1, 1, d)
    lb = ln_b.reshape(-1, 1, d)
    f1_in, f1_out, f2_in, f2_out = bf(ffn1_w_in), bf(ffn1_w_out), bf(ffn2_w_in), bf(ffn2_w_out)
    wq, wo = bf(xattn_w_q), bf(xattn_w_o)
    kv = _kv_proj(mem, bf(xattn_w_kv))
    ev_in, ev_out = bf(even_w_in), bf(even_w_out)
    od_in, od_out = bf(odd_w_in), bf(odd_w_out)
    n_e, n_pool, pool_dim = pool_w.shape[:3]
    pw = bf(_block_diag(pool_w.reshape(n_e, n_pool // POOL_PACK, POOL_PACK, pool_dim, pool_dim)))
    sgu_bt = jnp.swapaxes(sgu_b, 1, 2)
    wa_bd, wx_bd = bf(_block_diag(lru_w_a)), bf(_block_diag(lru_w_x))
    w_rwkv = rwkv_w0.shape[-1]
    ones_bd = _block_diag(jnp.ones((w_rwkv // RWKV_HEAD_DIM, RWKV_HEAD_DIM, RWKV_HEAD_DIM), BF16))
    r_k = rwkv_r_k.reshape(rwkv_r_k.shape[0], 1, -1)

    for l in range(depth):
        xf = _ffn_layer(xf, f1_in, f1_out, lg, lb, l, 0)
        e = l // 2
        if l % 2 == 0:
            xf = _even_layer(xf, ev_in, ev_out, pw, row3(pool_scale), row3(sgu_ln_g), row3(sgu_ln_b),
                             sgu_w, sgu_bt, lg, lb, e, l, nb, nsw)
        else:
            r, lw, k, v, a, b, g, bonus, yd = _odd_pre(
                xf, od_in, row3(rwkv_mu), row3(rwkv_w0), bf(rwkv_w_up), row3(rwkv_a0), bf(rwkv_a_up),
                bf(rwkv_g_up), row3(rwkv_k_k), row3(rwkv_k_a), r_k, lru_conv_w, row3(lru_conv_b),
                wa_bd, row3(lru_b_a), wx_bd, row3(lru_b_x), row3(lru_lambda), ones_bd, e, nb, ns)
            y = _rwkv_scan(r, lw, k, v, a, b, nb, ns)
            xf = _odd_post(xf, y, g, bonus, yd, row3(rwkv_gn_g), row3(rwkv_gn_b), ones_bd, od_out,
                           lg, lb, e, l, nb, nsw)
        xf = _xattn_layer(xf, wq, kv, wo, lg, lb, l, nb, nsw)
        xf = _ffn_layer(xf, f2_in, f2_out, lg, lb, l, 3)
    return xf.reshape(nb, seq, d)
```

```python
import functools

import jax
import jax.numpy as jnp
from jax import lax
from jax.experimental import pallas as pl
from jax.experimental.pallas import tpu as pltpu

F32 = jnp.float32
BF16 = jnp.bfloat16

DEPTH = 4
LN_EPS = 1e-5
DEEPNORM_ALPHA = (2 * DEPTH) ** 0.25
MACARON_WEIGHT = 0.5
POOL_WINDOWS = (2, 4, 8, 16)
POOL_HALO = 16
POOL_PACK = 2
SGU_CHUNK = 128
XATTN_HEADS = 4
XATTN_SKEW = 2
RWKV_HEAD_DIM = 64
RWKV_CHUNK = 64
RWKV_GN_EPS = 64e-5
MXU_WIDTH = 256
RWKV_INTERLEAVE = 4
LRU_C = 8.0
CONV_WIDTH = 4
SUBLANES = 8
CONV_HALO = SUBLANES

ROW_TILE = 512
FFN_TILE = 1024
WIDE_TILE = 1024
XATTN_TILE = 1024
SUB_ROWS = 256
FFN_CHUNK = 256
VMEM_LIMIT = 56 * 1024 * 1024


def _cparams(grid_rank=2):
    return pltpu.CompilerParams(dimension_semantics=("arbitrary",) * grid_rank,
                                vmem_limit_bytes=VMEM_LIMIT)


def _dot(a, b):
    return jnp.dot(a.astype(BF16), b.astype(BF16), preferred_element_type=F32)


_NN = (((1,), (0,)), ((), ()))
_NT = (((1,), (1,)), ((), ()))
_TN = (((0,), (0,)), ((), ()))


def _mm1(a, b, dn=_NN):
    return lax.dot_general(a.astype(BF16), b.astype(BF16), dn, preferred_element_type=F32)


def _head_sum(x, ones_bd, pieces):
    total = None
    rest = x
    for i in range(pieces):
        p = rest.astype(BF16)
        if i + 1 < pieces:
            rest = rest - p.astype(F32)
        term = jnp.dot(p, ones_bd, preferred_element_type=F32)
        total = term if total is None else total + term
    return total


def _layer_norm(y, g, b, eps=LN_EPS):
    mu = jnp.mean(y, axis=-1, keepdims=True)
    yc = y - mu
    var = jnp.mean(yc * yc, axis=-1, keepdims=True)
    return yc * lax.rsqrt(var + eps) * g + b


def _softplus(t):
    return jnp.maximum(t, 0.0) + jnp.log(1.0 + jnp.exp(-jnp.abs(t)))


def _row_ids(rows, cols):
    return lax.broadcasted_iota(jnp.int32, (rows, cols), 0)


def _ffn_kernel(x_ref, win_ref, wout_ref, g_ref, b_ref, o_ref, h_ref, *, d_ff):
    sub = SUB_ROWS
    nsub = x_ref.shape[0] // sub
    accs = []
    for t in range(nsub):
        rows = slice(t * sub, (t + 1) * sub)
        xb = x_ref[rows, :].astype(BF16)
        for c in range(d_ff // FFN_CHUNK):
            lo = c * FFN_CHUNK
            gate = jnp.dot(xb, win_ref[:, lo:lo + FFN_CHUNK], preferred_element_type=F32)
            up = jnp.dot(xb, win_ref[:, d_ff + lo:d_ff + lo + FFN_CHUNK], preferred_element_type=F32)
            h_ref[rows, lo:lo + FFN_CHUNK] = (gate * jax.nn.sigmoid(gate) * up).astype(BF16)
        accs.append(jnp.dot(h_ref[rows, :], wout_ref[...], preferred_element_type=F32))
    for t in range(nsub):
        rows = slice(t * sub, (t + 1) * sub)
        y = DEEPNORM_ALPHA * x_ref[rows, :] + MACARON_WEIGHT * accs[t]
        o_ref[rows, :] = _layer_norm(y, g_ref[...], b_ref[...])


def _ffn_layer(xf, w_in, w_out, ln_g, ln_b, l, j):
    m, d = xf.shape
    d_ff = w_out.shape[1]
    assert m % FFN_TILE == 0
    once = pl.Buffered(1)
    return pl.pallas_call(
        functools.partial(_ffn_kernel, d_ff=d_ff),
        grid=(m // FFN_TILE,),
        in_specs=[
            pl.BlockSpec((FFN_TILE, d), lambda i: (i, 0)),
            pl.BlockSpec((None, d, 2 * d_ff), lambda i: (l, 0, 0), pipeline_mode=once),
            pl.BlockSpec((None, d_ff, d), lambda i: (l, 0, 0), pipeline_mode=once),
            pl.BlockSpec((None, 1, d), lambda i: (4 * l + j, 0, 0)),
            pl.BlockSpec((None, 1, d), lambda i: (4 * l + j, 0, 0)),
        ],
        out_specs=pl.BlockSpec((FFN_TILE, d), lambda i: (i, 0)),
        out_shape=jax.ShapeDtypeStruct((m, d), F32),
        scratch_shapes=[pltpu.VMEM((FFN_TILE, d_ff), BF16)],
        compiler_params=_cparams(1),
        name="ffn",
    )(xf, w_in, w_out, ln_g, ln_b)


def _even_kernel(x_ref, win_ref, wout_ref, poolw_ref, pscale_ref, sg_ref, sb_ref, sw_ref, sbt_ref,
                 g_ref, b_ref, o_ref, ext_ref):
    s = pl.program_id(1)
    tm = x_ref.shape[0]
    width = pscale_ref.shape[-1]
    pw = width // len(POOL_WINDOWS)
    heads = sw_ref.shape[0]
    hw = width // heads
    sub = SUB_ROWS
    nchunk = sub // SGU_CHUNK
    subs = range(tm // sub)
    tiles = [slice(t * sub, (t + 1) * sub) for t in subs]

    @pl.when(s == 0)
    def _():
        ext_ref[0:POOL_HALO, :] = jnp.zeros((POOL_HALO, width), F32)

    h = [_dot(x_ref[rows, :], win_ref[...]) for rows in tiles]
    for t in subs:
        ext_ref[POOL_HALO + t * sub:POOL_HALO + (t + 1) * sub, :] = h[t][:, :width]

    ti = lax.broadcasted_iota(jnp.int32, (SGU_CHUNK, SGU_CHUNK), 0)
    si = lax.broadcasted_iota(jnp.int32, (SGU_CHUNK, SGU_CHUNK), 1)
    w_sgu = [jnp.where(ti >= si, sw_ref[hd], 0.0).astype(BF16) for hd in range(heads)]

    ya, yb = [], []
    for t in subs:
        ext = ext_ref[t * sub:(t + 1) * sub + POOL_HALO, :]
        pos = (s * tm + t * sub + 1 + _row_ids(sub, 1)).astype(F32)
        pooled = []
        for g, win in enumerate(POOL_WINDOWS):
            acc = ext[:, g * pw:(g + 1) * pw]
            d = 1
            while d < win:
                acc = acc + pltpu.roll(acc, d, axis=0)
                d *= 2
            mean = acc[POOL_HALO:, :] / jnp.minimum(pos, float(win))
            pooled.append(mean - h[t][:, g * pw:(g + 1) * pw])
        yat = [_dot(jnp.concatenate(pooled[p * POOL_PACK:(p + 1) * POOL_PACK], axis=-1), poolw_ref[p])
               for p in range(len(POOL_WINDOWS) // POOL_PACK)]
        ya.append(jnp.concatenate(yat, axis=-1) * pscale_ref[...])

        u = jax.nn.gelu(h[t][:, width:2 * width])
        v = jax.nn.gelu(h[t][:, 2 * width:3 * width])
        vb = _layer_norm(v, sg_ref[...], sb_ref[...]).astype(BF16)
        mixed = []
        for hd in range(heads):
            bias = sbt_ref[:, hd:hd + 1]
            vh = jnp.concatenate([vb[c * SGU_CHUNK:(c + 1) * SGU_CHUNK, hd * hw:(hd + 1) * hw]
                                  for c in range(nchunk)], axis=1)
            mixed.append(jnp.dot(w_sgu[hd], vh, preferred_element_type=F32) + bias)
        yb.append(u * jnp.concatenate(
            [jnp.concatenate([mixed[hd][:, c * hw:(c + 1) * hw] for hd in range(heads)], axis=1)
             for c in range(nchunk)], axis=0))

    out = [_dot(ya[t], wout_ref[0:width, :]) + _dot(yb[t], wout_ref[width:2 * width, :]) for t in subs]
    for t, rows in enumerate(tiles):
        o_ref[rows, :] = _layer_norm(DEEPNORM_ALPHA * x_ref[rows, :] + out[t], g_ref[...], b_ref[...])
    ext_ref[0:POOL_HALO, :] = ext_ref[tm:tm + POOL_HALO, :]


def _even_layer(xf, w_in, w_out, pool_w, pool_scale, sgu_g, sgu_b, sgu_w, sgu_bt, ln_g, ln_b, e, l, nb, ns):
    m, d = xf.shape
    n_in = w_in.shape[-1]
    width = pool_scale.shape[-1]
    full = lambda shape: pl.BlockSpec((None,) + shape, lambda b, s: (e,) + (0,) * len(shape))
    return pl.pallas_call(
        _even_kernel,
        grid=(nb, ns),
        in_specs=[
            pl.BlockSpec((WIDE_TILE, d), lambda b, s: (b * ns + s, 0)),
            full((d, n_in)),
            full((2 * width, d)),
            full(pool_w.shape[1:]),
            full((1, width)),
            full((1, width)),
            full((1, width)),
            full(sgu_w.shape[1:]),
            full(sgu_bt.shape[1:]),
            pl.BlockSpec((None, 1, d), lambda b, s: (4 * l + 1, 0, 0)),
            pl.BlockSpec((None, 1, d), lambda b, s: (4 * l + 1, 0, 0)),
        ],
        out_specs=pl.BlockSpec((WIDE_TILE, d), lambda b, s: (b * ns + s, 0)),
        out_shape=jax.ShapeDtypeStruct((m, d), F32),
        scratch_shapes=[pltpu.VMEM((POOL_HALO + WIDE_TILE, width), F32)],
        compiler_params=_cparams(),
        name="even_mixer",
    )(xf, w_in, w_out, pool_w, pool_scale, sgu_g, sgu_b, sgu_w, sgu_bt, ln_g, ln_b)


def _odd_pre_kernel(x_ref, win_ref, mu_ref, w0_ref, wup_ref, a0_ref, aup_ref, gup_ref, kk_ref, ka_ref,
                    rk_ref, cw_ref, cb_ref, wa_ref, ba_ref, wx_ref, bx_ref, lam_ref, ones_ref,
                    r_o, lw_o, k_o, v_o, a_o, b_o, g_o, bonus_o, yd_o,
                    hprev_ref, xr_ref, hlru_ref):
    s = pl.program_id(1)
    tm = x_ref.shape[0]
    w = w0_ref.shape[-1]
    n_c = mu_ref.shape[-1]
    sub = SUB_ROWS
    nsub = tm // sub
    tiles = [slice(t * sub, (t + 1) * sub) for t in range(nsub)]
    dr = wup_ref.shape[0]
    ar = aup_ref.shape[0]
    ones_bd = ones_ref[...]

    @pl.when(s == 0)
    def _():
        hprev_ref[...] = jnp.zeros(hprev_ref.shape, F32)
        xr_ref[0:CONV_HALO, :] = jnp.zeros((CONV_HALO, w), F32)
        hlru_ref[...] = jnp.zeros(hlru_ref.shape, F32)

    h = [_dot(x_ref[rows, :], win_ref[...]) for rows in tiles]
    for t in range(nsub):
        xr_ref[CONV_HALO + t * sub:CONV_HALO + (t + 1) * sub, :] = h[t][:, n_c + w:]

    first = _row_ids(sub, 1) == 0
    ngrp = sub // SUBLANES
    sub_id = lax.broadcasted_iota(jnp.int32, (ngrp, SUBLANES, w), 1)
    hprev = hprev_ref[...]
    hcar = hlru_ref[...]
    for t, rows in enumerate(tiles):
        hc = h[t][:, :n_c]
        prev = jnp.where(first, hprev, pltpu.roll(hc, 1, axis=0))
        hprev = hc[sub - 1:sub, :]
        hc = hc + mu_ref[...] * (prev - hc)
        r = hc[:, 0:w]
        k = hc[:, w:2 * w]
        v = hc[:, 2 * w:3 * w]
        rest = hc[:, 3 * w:]
        wd = rest[:, 0:dr]
        ad = rest[:, dr:dr + ar]
        gd = rest[:, dr + ar:]

        z = w0_ref[...] + _dot(jnp.tanh(wd), wup_ref[...])
        logw = -_softplus(-z) - 0.5
        lw_o[rows, :] = -jnp.exp(logw)
        asig = jax.nn.sigmoid(a0_ref[...] + _dot(ad, aup_ref[...]))
        g_o[rows, :] = _dot(jax.nn.sigmoid(gd), gup_ref[...])
        kk = k * kk_ref[...]
        kk = kk * lax.rsqrt(jnp.maximum(_head_sum(kk * kk, ones_bd, 1), 1e-24))
        k2 = k * (1.0 + (asig - 1.0) * ka_ref[...])
        r_o[rows, :] = r
        k_o[rows, :] = k2
        v_o[rows, :] = v
        a_o[rows, :] = -kk
        b_o[rows, :] = kk * asig
        bonus_o[rows, :] = _head_sum(r * k2 * rk_ref[...], ones_bd, 2) * v

        gate = h[t][:, n_c:n_c + w]
        ext = xr_ref[t * sub:(t + 1) * sub + CONV_HALO, :]
        xc = cw_ref[CONV_WIDTH - 1:CONV_WIDTH, :] * ext[CONV_HALO:, :] + cb_ref[...]
        for j in range(1, CONV_WIDTH):
            xc = xc + cw_ref[CONV_WIDTH - 1 - j:CONV_WIDTH - j, :] * pltpu.roll(ext, j, axis=0)[CONV_HALO:, :]
        rec = jax.nn.sigmoid(_dot(xc, wa_ref[...]) + ba_ref[...])
        inp = jax.nn.sigmoid(_dot(xc, wx_ref[...]) + bx_ref[...])
        log_a = -LRU_C * rec * _softplus(-lam_ref[...])
        a = jnp.exp(log_a)
        bx = jnp.sqrt(-jnp.tanh(log_a) * (a * a + 1.0)) * (inp * xc)
        a3 = a.reshape(ngrp, SUBLANES, w)
        b3 = bx.reshape(ngrp, SUBLANES, w)
        d = 1
        while d < SUBLANES:
            keep = sub_id >= d
            a_sh = jnp.where(keep, pltpu.roll(a3, d, axis=1), 1.0)
            b_sh = jnp.where(keep, pltpu.roll(b3, d, axis=1), 0.0)
            b3 = a3 * b_sh + b3
            a3 = a3 * a_sh
            d *= 2
        gate_act = jax.nn.gelu(gate)
        for j in range(ngrp):
            hj = a3[j] * hcar + b3[j]
            lo = t * sub + j * SUBLANES
            yd_o[lo:lo + SUBLANES, :] = hj * gate_act[j * SUBLANES:(j + 1) * SUBLANES, :]
            hcar = hj[SUBLANES - 1:SUBLANES, :]
    hprev_ref[...] = hprev
    hlru_ref[...] = hcar
    xr_ref[0:CONV_HALO, :] = xr_ref[tm:tm + CONV_HALO, :]


def _odd_pre(xf, w_in, mu, w0, w_up, a0, a_up, g_up, k_k, k_a, r_k, conv_w, conv_b, wa_bd, b_a, wx_bd, b_x,
             lam, ones_bd, e, nb, ns):
    m, d = xf.shape
    w = w0.shape[-1]
    full = lambda arr: pl.BlockSpec((None,) + arr.shape[1:], lambda b, s: (e,) + (0,) * (arr.ndim - 1))
    row = pl.BlockSpec((ROW_TILE, w), lambda b, s: (b * ns + s, 0))
    params = (w_in, mu, w0, w_up, a0, a_up, g_up, k_k, k_a, r_k, conv_w, conv_b, wa_bd, b_a, wx_bd, b_x, lam)
    return pl.pallas_call(
        _odd_pre_kernel,
        grid=(nb, ns),
        in_specs=[pl.BlockSpec((ROW_TILE, d), lambda b, s: (b * ns + s, 0))]
        + [full(p) for p in params]
        + [pl.BlockSpec(ones_bd.shape, lambda b, s: (0, 0))],
        out_specs=[row] * 9,
        out_shape=[jax.ShapeDtypeStruct((m, w), F32)] * 9,
        scratch_shapes=[pltpu.VMEM((1, mu.shape[-1]), F32),
                        pltpu.VMEM((CONV_HALO + ROW_TILE, w), F32),
                        pltpu.VMEM((1, w), F32)],
        compiler_params=_cparams(),
        name="odd_pre",
    )(xf, *params, ones_bd)


def _rwkv_kernel(r_ref, lw_ref, k_ref, v_ref, a_ref, b_ref, y_ref, t_ref, q_ref, g_ref, h_ref):
    s = pl.program_id(1)
    tm, w = r_ref.shape
    n = RWKV_HEAD_DIM
    lc = RWKV_CHUNK
    gw = MXU_WIDTH
    hpg = gw // n
    groups = range(w // gw)

    @pl.when(s == 0)
    def _():
        t_ref[...] = jnp.zeros(t_ref.shape, F32)

    row_t = lax.broadcasted_iota(jnp.int32, (lc, gw), 0)
    col = lax.broadcasted_iota(jnp.int32, (lc, gw), 1)
    col_s = col & (n - 1)
    col_h = col >> (n.bit_length() - 1)
    strict = row_t > col_s
    incl = row_t >= col_s
    eye_m = row_t == col_s
    eye = eye_m.astype(F32)
    blk_r = lax.broadcasted_iota(jnp.int32, (gw, gw), 0) >> (n.bit_length() - 1)
    blk_c = lax.broadcasted_iota(jnp.int32, (gw, gw), 1) >> (n.bit_length() - 1)
    bd_mask = (blk_r == blk_c).astype(BF16)
    rows = _row_ids(lc, w)

    def bd(x):
        xb = x.astype(BF16)
        return jnp.concatenate([xb] * hpg, axis=0) * bd_mask

    def mm(a, b_bf16, dn=_NN):
        return lax.dot_general(a.astype(BF16), b_bf16, dn, preferred_element_type=F32)

    def diag_blocks(p):
        out = p[(hpg - 1) * n:, :]
        for h in range(hpg - 1):
            out = jnp.where(col_h == h, p[h * n:(h + 1) * n, :], out)
        return out

    def state_free(i, carry):
        at, rt, bt, kt, bh, kh, v, pend, where = [], [], [], [], [], [], [], [], []
        for ci in range(RWKV_INTERLEAVE):
            sl = pl.ds((i * RWKV_INTERLEAVE + ci) * lc, lc)
            lw = lw_ref[sl, :]
            cum = lw
            d = 1
            while d < lc:
                cum = cum + jnp.where(rows >= d, pltpu.roll(cum, d, axis=0), 0.0)
                d *= 2
            cl = cum[lc - 1:lc, :]
            einv = jnp.exp(-cum)
            eend = jnp.exp(cl - cum)
            a = a_ref[sl, :]
            b = b_ref[sl, :]
            k = k_ref[sl, :]
            full = (a * jnp.exp(cum - lw), r_ref[sl, :] * jnp.exp(cum), b * einv, k * einv, b * eend, k * eend,
                    v_ref[sl, :], jnp.exp(cl))
            for g in groups:
                for dst, t in zip((at, rt, bt, kt, bh, kh, v, pend), full):
                    dst.append(t[:, g * gw:(g + 1) * gw])
                where.append((sl, slice(g * gw, (g + 1) * gw)))
        units = range(len(where))
        lhs = [jnp.concatenate([at[u], rt[u]], axis=0).astype(BF16) for u in units]
        sb = [mm(lhs[u], bd(bt[u]), _NT) for u in units]
        sk = [mm(lhs[u], bd(kt[u]), _NT) for u in units]
        a_ab = [jnp.where(strict, t[:lc], 0.0) for t in sb]
        m_rb = [jnp.where(incl, t[lc:], 0.0) for t in sb]
        a_ak = [jnp.where(strict, t[:lc], 0.0) for t in sk]
        m_rk = [jnp.where(incl, t[lc:], 0.0) for t in sk]
        xm = [mm(jnp.concatenate([a_ak[u], m_rk[u]], axis=0), bd(v[u])) for u in units]
        x0 = [t[:lc] for t in xm]
        ykv = [t[lc:] for t in xm]
        inv = [eye + t for t in a_ab]
        apow = [mm(a_ab[u], bd(a_ab[u])) for u in units]
        span = 4
        while span < lc:
            p = [mm(jnp.concatenate([inv[u], apow[u]], axis=0), bd(apow[u])) for u in units]
            inv = [inv[u] + p[u][:lc] for u in units]
            apow = [t[lc:] for t in p]
            span *= 2
        inv = [inv[u] + mm(inv[u], bd(apow[u])) for u in units]
        ap = [mm(inv[u], bd(at[u])) for u in units]
        vp = [mm(inv[u], bd(x0[u])) for u in units]
        q = [rt[u] + mm(m_rb[u], bd(ap[u])) for u in units]
        yl = [mm(m_rb[u], bd(vp[u])) + ykv[u] for u in units]
        gm = [diag_blocks(mm(bh[u], ap[u].astype(BF16), _TN))
              + jnp.where(eye_m, jnp.broadcast_to(pend[u], (lc, gw)), 0.0) for u in units]
        hm = [diag_blocks(mm(jnp.concatenate([bh[u], kh[u]], axis=0),
                             jnp.concatenate([vp[u], v[u]], axis=0).astype(BF16), _TN)) for u in units]
        for u in units:
            q_ref[where[u]] = q[u]
            y_ref[where[u]] = yl[u]
            g_ref[where[u]] = gm[u]
            h_ref[where[u]] = hm[u]
        return carry

    for i in range(tm // (lc * RWKV_INTERLEAVE)):
        state_free(i, 0)

    def state_pass(c, carry):
        sl = pl.ds(pl.multiple_of(c * lc, lc), lc)
        fin = [mm(jnp.concatenate([q_ref[sl, g * gw:(g + 1) * gw], g_ref[sl, g * gw:(g + 1) * gw]], axis=0),
                  bd(t_ref[:, g * gw:(g + 1) * gw])) for g in groups]
        for g in groups:
            gs = slice(g * gw, (g + 1) * gw)
            y_ref[sl, gs] = y_ref[sl, gs] + fin[g][:lc]
            t_ref[:, gs] = fin[g][lc:] + h_ref[sl, gs]
        return carry

    lax.fori_loop(0, tm // lc, state_pass, 0)


def _rwkv_scan(r, lw, k, v, a, b, nb, ns):
    m, w = r.shape
    assert RWKV_CHUNK == RWKV_HEAD_DIM and w % MXU_WIDTH == 0 and MXU_WIDTH % RWKV_HEAD_DIM == 0
    assert RWKV_HEAD_DIM & (RWKV_HEAD_DIM - 1) == 0
    row = pl.BlockSpec((ROW_TILE, w), lambda bb, s: (bb * ns + s, 0))
    return pl.pallas_call(
        _rwkv_kernel,
        grid=(nb, ns),
        in_specs=[row] * 6,
        out_specs=row,
        out_shape=jax.ShapeDtypeStruct((m, w), F32),
        scratch_shapes=[pltpu.VMEM((RWKV_HEAD_DIM, w), F32)]
        + [pltpu.VMEM((ROW_TILE, w), F32)] * 3,
        compiler_params=_cparams(),
        name="rwkv7",
    )(r, lw, k, v, a, b)


def _odd_post_kernel(x_ref, y_ref, g_ref, bonus_ref, yd_ref, gng_ref, gnb_ref, ones_ref, wout_ref,
                     lg_ref, lb_ref, o_ref):
    w = y_ref.shape[-1]
    sub = SUB_ROWS
    tiles = [slice(t * sub, (t + 1) * sub) for t in range(x_ref.shape[0] // sub)]
    ones_bd = ones_ref[...]
    inv_n = 1.0 / RWKV_HEAD_DIM
    y = [y_ref[rows, :] for rows in tiles]
    mu = [_head_sum(t, ones_bd, 2) * inv_n for t in y]
    yc = [a - b for a, b in zip(y, mu)]
    var = [_head_sum(t * t, ones_bd, 1) * inv_n for t in yc]
    out = []
    for t, rows in enumerate(tiles):
        yn = yc[t] * lax.rsqrt(var[t] + RWKV_GN_EPS) * gng_ref[...] + gnb_ref[...]
        y_rwkv = (yn + bonus_ref[rows, :]) * g_ref[rows, :]
        out.append(_dot(y_rwkv, wout_ref[0:w, :]) + _dot(yd_ref[rows, :], wout_ref[w:2 * w, :]))
    for t, rows in enumerate(tiles):
        o_ref[rows, :] = _layer_norm(DEEPNORM_ALPHA * x_ref[rows, :] + out[t], lg_ref[...], lb_ref[...])


def _odd_post(xf, y, g, bonus, yd, gn_g, gn_b, ones_bd, w_out, ln_g, ln_b, e, l, nb, ns):
    m, d = xf.shape
    w = y.shape[-1]
    row = pl.BlockSpec((WIDE_TILE, w), lambda b, s: (b * ns + s, 0))
    xrow = pl.BlockSpec((WIDE_TILE, d), lambda b, s: (b * ns + s, 0))
    return pl.pallas_call(
        _odd_post_kernel,
        grid=(nb, ns),
        in_specs=[xrow, row, row, row, row,
                  pl.BlockSpec((None, 1, w), lambda b, s: (e, 0, 0)),
                  pl.BlockSpec((None, 1, w), lambda b, s: (e, 0, 0)),
                  pl.BlockSpec(ones_bd.shape, lambda b, s: (0, 0)),
                  pl.BlockSpec((None, 2 * w, d), lambda b, s: (e, 0, 0)),
                  pl.BlockSpec((None, 1, d), lambda b, s: (4 * l + 1, 0, 0)),
                  pl.BlockSpec((None, 1, d), lambda b, s: (4 * l + 1, 0, 0))],
        out_specs=xrow,
        out_shape=jax.ShapeDtypeStruct((m, d), F32),
        compiler_params=_cparams(),
        name="odd_post",
    )(xf, y, g, bonus, yd, gn_g, gn_b, ones_bd, w_out, ln_g, ln_b)


def _kv_kernel(mem_ref, wkv_ref, kv_ref):
    kv_ref[...] = _dot(mem_ref[...], wkv_ref[...]).astype(BF16)


def _kv_proj(mem, w_kv):
    nb, mlen, d = mem.shape
    nl, _, n2 = w_kv.shape
    return pl.pallas_call(
        _kv_kernel,
        grid=(nl, nb),
        in_specs=[pl.BlockSpec((None, mlen, d), lambda l, b: (b, 0, 0)),
                  pl.BlockSpec((None, d, n2), lambda l, b: (l, 0, 0))],
        out_specs=pl.BlockSpec((None, None, mlen, n2), lambda l, b: (l, b, 0, 0)),
        out_shape=jax.ShapeDtypeStruct((nl, nb, mlen, n2), BF16),
        compiler_params=_cparams(),
        name="xattn_kv",
    )(mem, w_kv)


def _xattn_kernel(x_ref, wq_ref, kv_ref, wo_ref, g_ref, b_ref, o_ref):
    d = x_ref.shape[-1]
    hd = d // XATTN_HEADS
    sub = SUB_ROWS
    nsub = x_ref.shape[0] // sub
    tiles = [slice(t * sub, (t + 1) * sub) for t in range(nsub)]
    heads = XATTN_HEADS
    n_units = nsub * heads
    q, p, proj, outs = {}, {}, {}, [[] for _ in tiles]
    q[0] = _dot(x_ref[tiles[0], :], wq_ref[...])
    for i in range(n_units + XATTN_SKEW + 1):
        if i < n_units:
            t, h = divmod(i, heads)
            if h == 1 and t + 1 < nsub:
                q[t + 1] = _dot(x_ref[tiles[t + 1], :], wq_ref[...])
            qh = q[t][:, h * hd:(h + 1) * hd].astype(BF16)
            sc = lax.dot_general(qh, kv_ref[:, h * hd:(h + 1) * hd], _NT, preferred_element_type=F32)
            sc = sc * (hd ** -0.5)
            sc = sc - jnp.max(sc, axis=-1, keepdims=True)
            e = jnp.exp(sc)
            p[i] = (e / jnp.sum(e, axis=-1, keepdims=True)).astype(BF16)
        j = i - XATTN_SKEW
        if 0 <= j < n_units:
            t, h = divmod(j, heads)
            outs[t].append(jnp.dot(p.pop(j), kv_ref[:, d + h * hd:d + (h + 1) * hd], preferred_element_type=F32))
        j = i - XATTN_SKEW - 1
        if 0 <= j < n_units and j % heads == heads - 1:
            t = j // heads
            proj[t] = _dot(jnp.concatenate(outs[t], axis=-1), wo_ref[...])
    for t, rows in enumerate(tiles):
        o_ref[rows, :] = _layer_norm(DEEPNORM_ALPHA * x_ref[rows, :] + proj[t], g_ref[...], b_ref[...])


def _xattn_layer(xf, w_q, kv, w_o, ln_g, ln_b, l, nb, ns):
    m, d = xf.shape
    mlen = kv.shape[2]
    xrow = pl.BlockSpec((XATTN_TILE, d), lambda b, s: (b * ns + s, 0))
    return pl.pallas_call(
        _xattn_kernel,
        grid=(nb, ns),
        in_specs=[xrow,
                  pl.BlockSpec((None, d, d), lambda b, s: (l, 0, 0)),
                  pl.BlockSpec((None, None, mlen, 2 * d), lambda b, s: (l, b, 0, 0)),
                  pl.BlockSpec((None, d, d), lambda b, s: (l, 0, 0)),
                  pl.BlockSpec((None, 1, d), lambda b, s: (4 * l + 2, 0, 0)),
                  pl.BlockSpec((None, 1, d), lambda b, s: (4 * l + 2, 0, 0))],
        out_specs=xrow,
        out_shape=jax.ShapeDtypeStruct((m, d), F32),
        compiler_params=_cparams(),
        name="xattn",
    )(xf, w_q, kv, w_o, ln_g, ln_b)


def _block_diag(w):
    nblk, bi, bj = w.shape[-3:]
    eye = jnp.eye(nblk, dtype=w.dtype)
    out = w[..., :, :, None, :] * eye[:, None, :, None]
    return out.reshape(w.shape[:-3] + (nblk * bi, nblk * bj))


def kernel(x, mem, ffn1_w_in, ffn1_w_out, ffn2_w_in, ffn2_w_out, ln_g, ln_b, xattn_w_q, xattn_w_kv, xattn_w_o, even_w_in, even_w_out, pool_w, pool_scale, sgu_ln_g, sgu_ln_b, sgu_w, sgu_b, odd_w_in, odd_w_out, rwkv_mu, rwkv_w0, rwkv_w_up, rwkv_a0, rwkv_a_up, rwkv_g_up, rwkv_k_k, rwkv_k_a, rwkv_r_k, rwkv_gn_g, rwkv_gn_b, lru_conv_w, lru_conv_b, lru_w_a, lru_b_a, lru_w_x, lru_b_x, lru_lambda):
    nb, seq, d = x.shape
    assert seq % WIDE_TILE == 0 and seq % XATTN_TILE == 0 and WIDE_TILE % ROW_TILE == 0 and ROW_TILE % SUB_ROWS == 0
    assert SUB_ROWS % SGU_CHUNK == 0 and ROW_TILE % (RWKV_CHUNK * RWKV_INTERLEAVE) == 0
    ns = seq // ROW_TILE
    nsw = seq // WIDE_TILE
    depth = ffn1_w_in.shape[0]
    xf = x.reshape(nb * seq, d)

    bf = lambda t: t.astype(BF16)
    row3 = lambda t: t.reshape(t.shape[0], 1, -1)
    lg = ln_g.reshape(-1, 1, d)
    lb = ln_b.reshape(-1, 1, d)
    f1_in, f1_out, f2_in, f2_out = bf(ffn1_w_in), bf(ffn1_w_out), bf(ffn2_w_in), bf(ffn2_w_out)
    wq, wo = bf(xattn_w_q), bf(xattn_w_o)
    kv = _kv_proj(mem, bf(xattn_w_kv))
    ev_in, ev_out = bf(even_w_in), bf(even_w_out)
    od_in, od_out = bf(odd_w_in), bf(odd_w_out)
    n_e, n_pool, pool_dim = pool_w.shape[:3]
    pw = bf(_block_diag(pool_w.reshape(n_e, n_pool // POOL_PACK, POOL_PACK, pool_dim, pool_dim)))
    sgu_bt = jnp.swapaxes(sgu_b, 1, 2)
    wa_bd, wx_bd = bf(_block_diag(lru_w_a)), bf(_block_diag(lru_w_x))
    w_rwkv = rwkv_w0.shape[-1]
    ones_bd = _block_diag(jnp.ones((w_rwkv // RWKV_HEAD_DIM, RWKV_HEAD_DIM, RWKV_HEAD_DIM), BF16))
    r_k = rwkv_r_k.reshape(rwkv_r_k.shape[0], 1, -1)

    for l in range(depth):
        xf = _ffn_layer(xf, f1_in, f1_out, lg, lb, l, 0)
        e = l // 2
        if l % 2 == 0:
            xf = _even_layer(xf, ev_in, ev_out, pw, row3(pool_scale), row3(sgu_ln_g), row3(sgu_ln_b),
                             sgu_w, sgu_bt, lg, lb, e, l, nb, nsw)
        else:
            r, lw, k, v, a, b, g, bonus, yd = _odd_pre(
                xf, od_in, row3(rwkv_mu), row3(rwkv_w0), bf(rwkv_w_up), row3(rwkv_a0), bf(rwkv_a_up),
                bf(rwkv_g_up), row3(rwkv_k_k), row3(rwkv_k_a), r_k, lru_conv_w, row3(lru_conv_b),
                wa_bd, row3(lru_b_a), wx_bd, row3(lru_b_x), row3(lru_lambda), ones_bd, e, nb, ns)
            y = _rwkv_scan(r, lw, k, v, a, b, nb, ns)
            xf = _odd_post(xf, y, g, bonus, yd, row3(rwkv_gn_g), row3(rwkv_gn_b), ones_bd, od_out,
                           lg, lb, e, l, nb, nsw)
        xf = _xattn_layer(xf, wq, kv, wo, lg, lb, l, nb, seq // XATTN_TILE)
        xf = _ffn_layer(xf, f2_in, f2_out, lg, lb, l, 3)
    return xf.reshape(nb, seq, d)
```

```python
import functools

import jax
import jax.numpy as jnp
from jax import lax
from jax.experimental import pallas as pl
from jax.experimental.pallas import tpu as pltpu

F32 = jnp.float32
BF16 = jnp.bfloat16

DEPTH = 4
LN_EPS = 1e-5
DEEPNORM_ALPHA = (2 * DEPTH) ** 0.25
MACARON_WEIGHT = 0.5
POOL_WINDOWS = (2, 4, 8, 16)
POOL_HALO = 16
POOL_PACK = 2
SGU_CHUNK = 128
XATTN_HEADS = 4
XATTN_SKEW = 2
RWKV_HEAD_DIM = 64
RWKV_CHUNK = 64
RWKV_GN_EPS = 64e-5
MXU_WIDTH = 256
RWKV_INTERLEAVE = 2
RWKV_LAG = 4
LRU_C = 8.0
CONV_WIDTH = 4
SUBLANES = 8
CONV_HALO = SUBLANES

ROW_TILE = 512
FFN_TILE = 1024
WIDE_TILE = 1024
XATTN_TILE = 1024
SUB_ROWS = 256
FFN_CHUNK = 256
VMEM_LIMIT = 56 * 1024 * 1024


def _cparams(grid_rank=2):
    return pltpu.CompilerParams(dimension_semantics=("arbitrary",) * grid_rank,
                                vmem_limit_bytes=VMEM_LIMIT)


def _dot(a, b):
    return jnp.dot(a.astype(BF16), b.astype(BF16), preferred_element_type=F32)


_NN = (((1,), (0,)), ((), ()))
_NT = (((1,), (1,)), ((), ()))
_TN = (((0,), (0,)), ((), ()))


def _mm1(a, b, dn=_NN):
    return lax.dot_general(a.astype(BF16), b.astype(BF16), dn, preferred_element_type=F32)


def _head_sum(x, ones_bd, pieces):
    total = None
    rest = x
    for i in range(pieces):
        p = rest.astype(BF16)
        if i + 1 < pieces:
            rest = rest - p.astype(F32)
        term = jnp.dot(p, ones_bd, preferred_element_type=F32)
        total = term if total is None else total + term
    return total


def _layer_norm(y, g, b, eps=LN_EPS):
    mu = jnp.mean(y, axis=-1, keepdims=True)
    yc = y - mu
    var = jnp.mean(yc * yc, axis=-1, keepdims=True)
    return yc * lax.rsqrt(var + eps) * g + b


def _softplus(t):
    return jnp.maximum(t, 0.0) + jnp.log(1.0 + jnp.exp(-jnp.abs(t)))


def _row_ids(rows, cols):
    return lax.broadcasted_iota(jnp.int32, (rows, cols), 0)


def _ffn_kernel(x_ref, win_ref, wout_ref, g_ref, b_ref, o_ref, h_ref, *, d_ff):
    sub = SUB_ROWS
    nsub = x_ref.shape[0] // sub
    accs = []
    for t in range(nsub):
        rows = slice(t * sub, (t + 1) * sub)
        xb = x_ref[rows, :].astype(BF16)
        for c in range(d_ff // FFN_CHUNK):
            lo = c * FFN_CHUNK
            gate = jnp.dot(xb, win_ref[:, lo:lo + FFN_CHUNK], preferred_element_type=F32)
            up = jnp.dot(xb, win_ref[:, d_ff + lo:d_ff + lo + FFN_CHUNK], preferred_element_type=F32)
            h_ref[rows, lo:lo + FFN_CHUNK] = (gate * jax.nn.sigmoid(gate) * up).astype(BF16)
        accs.append(jnp.dot(h_ref[rows, :], wout_ref[...], preferred_element_type=F32))
    for t in range(nsub):
        rows = slice(t * sub, (t + 1) * sub)
        y = DEEPNORM_ALPHA * x_ref[rows, :] + MACARON_WEIGHT * accs[t]
        o_ref[rows, :] = _layer_norm(y, g_ref[...], b_ref[...])


def _ffn_layer(xf, w_in, w_out, ln_g, ln_b, l, j):
    m, d = xf.shape
    d_ff = w_out.shape[1]
    assert m % FFN_TILE == 0
    once = pl.Buffered(1)
    return pl.pallas_call(
        functools.partial(_ffn_kernel, d_ff=d_ff),
        grid=(m // FFN_TILE,),
        in_specs=[
            pl.BlockSpec((FFN_TILE, d), lambda i: (i, 0)),
            pl.BlockSpec((None, d, 2 * d_ff), lambda i: (l, 0, 0), pipeline_mode=once),
            pl.BlockSpec((None, d_ff, d), lambda i: (l, 0, 0), pipeline_mode=once),
            pl.BlockSpec((None, 1, d), lambda i: (4 * l + j, 0, 0)),
            pl.BlockSpec((None, 1, d), lambda i: (4 * l + j, 0, 0)),
        ],
        out_specs=pl.BlockSpec((FFN_TILE, d), lambda i: (i, 0)),
        out_shape=jax.ShapeDtypeStruct((m, d), F32),
        scratch_shapes=[pltpu.VMEM((FFN_TILE, d_ff), BF16)],
        compiler_params=_cparams(1),
        name="ffn",
    )(xf, w_in, w_out, ln_g, ln_b)


def _even_kernel(x_ref, win_ref, wout_ref, poolw_ref, pscale_ref, sg_ref, sb_ref, sw_ref, sbt_ref,
                 g_ref, b_ref, o_ref, ext_ref):
    s = pl.program_id(1)
    tm = x_ref.shape[0]
    width = pscale_ref.shape[-1]
    pw = width // len(POOL_WINDOWS)
    heads = sw_ref.shape[0]
    hw = width // heads
    sub = SUB_ROWS
    nchunk = sub // SGU_CHUNK
    subs = range(tm // sub)
    tiles = [slice(t * sub, (t + 1) * sub) for t in subs]

    @pl.when(s == 0)
    def _():
        ext_ref[0:POOL_HALO, :] = jnp.zeros((POOL_HALO, width), F32)

    h = [_dot(x_ref[rows, :], win_ref[...]) for rows in tiles]
    for t in subs:
        ext_ref[POOL_HALO + t * sub:POOL_HALO + (t + 1) * sub, :] = h[t][:, :width]

    ti = lax.broadcasted_iota(jnp.int32, (SGU_CHUNK, SGU_CHUNK), 0)
    si = lax.broadcasted_iota(jnp.int32, (SGU_CHUNK, SGU_CHUNK), 1)
    w_sgu = [jnp.where(ti >= si, sw_ref[hd], 0.0).astype(BF16) for hd in range(heads)]

    ya, yb = [], []
    for t in subs:
        ext = ext_ref[t * sub:(t + 1) * sub + POOL_HALO, :]
        pos = (s * tm + t * sub + 1 + _row_ids(sub, 1)).astype(F32)
        pooled = []
        for g, win in enumerate(POOL_WINDOWS):
            acc = ext[:, g * pw:(g + 1) * pw]
            d = 1
            while d < win:
                acc = acc + pltpu.roll(acc, d, axis=0)
                d *= 2
            mean = acc[POOL_HALO:, :] / jnp.minimum(pos, float(win))
            pooled.append(mean - h[t][:, g * pw:(g + 1) * pw])
        yat = [_dot(jnp.concatenate(pooled[p * POOL_PACK:(p + 1) * POOL_PACK], axis=-1), poolw_ref[p])
               for p in range(len(POOL_WINDOWS) // POOL_PACK)]
        ya.append(jnp.concatenate(yat, axis=-1) * pscale_ref[...])

        u = jax.nn.gelu(h[t][:, width:2 * width])
        v = jax.nn.gelu(h[t][:, 2 * width:3 * width])
        vb = _layer_norm(v, sg_ref[...], sb_ref[...]).astype(BF16)
        mixed = []
        for hd in range(heads):
            bias = sbt_ref[:, hd:hd + 1]
            vh = jnp.concatenate([vb[c * SGU_CHUNK:(c + 1) * SGU_CHUNK, hd * hw:(hd + 1) * hw]
                                  for c in range(nchunk)], axis=1)
            mixed.append(jnp.dot(w_sgu[hd], vh, preferred_element_type=F32) + bias)
        yb.append(u * jnp.concatenate(
            [jnp.concatenate([mixed[hd][:, c * hw:(c + 1) * hw] for hd in range(heads)], axis=1)
             for c in range(nchunk)], axis=0))

    out = [_dot(ya[t], wout_ref[0:width, :]) + _dot(yb[t], wout_ref[width:2 * width, :]) for t in subs]
    for t, rows in enumerate(tiles):
        o_ref[rows, :] = _layer_norm(DEEPNORM_ALPHA * x_ref[rows, :] + out[t], g_ref[...], b_ref[...])
    ext_ref[0:POOL_HALO, :] = ext_ref[tm:tm + POOL_HALO, :]


def _even_layer(xf, w_in, w_out, pool_w, pool_scale, sgu_g, sgu_b, sgu_w, sgu_bt, ln_g, ln_b, e, l, nb, ns):
    m, d = xf.shape
    n_in = w_in.shape[-1]
    width = pool_scale.shape[-1]
    full = lambda shape: pl.BlockSpec((None,) + shape, lambda b, s: (e,) + (0,) * len(shape))
    return pl.pallas_call(
        _even_kernel,
        grid=(nb, ns),
        in_specs=[
            pl.BlockSpec((WIDE_TILE, d), lambda b, s: (b * ns + s, 0)),
            full((d, n_in)),
            full((2 * width, d)),
            full(pool_w.shape[1:]),
            full((1, width)),
            full((1, width)),
            full((1, width)),
            full(sgu_w.shape[1:]),
            full(sgu_bt.shape[1:]),
            pl.BlockSpec((None, 1, d), lambda b, s: (4 * l + 1, 0, 0)),
            pl.BlockSpec((None, 1, d), lambda b, s: (4 * l + 1, 0, 0)),
        ],
        out_specs=pl.BlockSpec((WIDE_TILE, d), lambda b, s: (b * ns + s, 0)),
        out_shape=jax.ShapeDtypeStruct((m, d), F32),
        scratch_shapes=[pltpu.VMEM((POOL_HALO + WIDE_TILE, width), F32)],
        compiler_params=_cparams(),
        name="even_mixer",
    )(xf, w_in, w_out, pool_w, pool_scale, sgu_g, sgu_b, sgu_w, sgu_bt, ln_g, ln_b)


def _odd_pre_kernel(x_ref, win_ref, mu_ref, w0_ref, wup_ref, a0_ref, aup_ref, gup_ref, kk_ref, ka_ref,
                    rk_ref, cw_ref, cb_ref, wa_ref, ba_ref, wx_ref, bx_ref, lam_ref, ones_ref,
                    r_o, lw_o, k_o, v_o, a_o, b_o, g_o, bonus_o, yd_o,
                    hprev_ref, xr_ref, hlru_ref):
    s = pl.program_id(1)
    tm = x_ref.shape[0]
    w = w0_ref.shape[-1]
    n_c = mu_ref.shape[-1]
    sub = SUB_ROWS
    nsub = tm // sub
    tiles = [slice(t * sub, (t + 1) * sub) for t in range(nsub)]
    dr = wup_ref.shape[0]
    ar = aup_ref.shape[0]
    ones_bd = ones_ref[...]

    @pl.when(s == 0)
    def _():
        hprev_ref[...] = jnp.zeros(hprev_ref.shape, F32)
        xr_ref[0:CONV_HALO, :] = jnp.zeros((CONV_HALO, w), F32)
        hlru_ref[...] = jnp.zeros(hlru_ref.shape, F32)

    h = [_dot(x_ref[rows, :], win_ref[...]) for rows in tiles]
    for t in range(nsub):
        xr_ref[CONV_HALO + t * sub:CONV_HALO + (t + 1) * sub, :] = h[t][:, n_c + w:]

    first = _row_ids(sub, 1) == 0
    ngrp = sub // SUBLANES
    sub_id = lax.broadcasted_iota(jnp.int32, (ngrp, SUBLANES, w), 1)
    hprev = hprev_ref[...]
    hcar = hlru_ref[...]
    for t, rows in enumerate(tiles):
        hc = h[t][:, :n_c]
        prev = jnp.where(first, hprev, pltpu.roll(hc, 1, axis=0))
        hprev = hc[sub - 1:sub, :]
        hc = hc + mu_ref[...] * (prev - hc)
        r = hc[:, 0:w]
        k = hc[:, w:2 * w]
        v = hc[:, 2 * w:3 * w]
        rest = hc[:, 3 * w:]
        wd = rest[:, 0:dr]
        ad = rest[:, dr:dr + ar]
        gd = rest[:, dr + ar:]

        z = w0_ref[...] + _dot(jnp.tanh(wd), wup_ref[...])
        logw = -_softplus(-z) - 0.5
        lw_o[rows, :] = -jnp.exp(logw)
        asig = jax.nn.sigmoid(a0_ref[...] + _dot(ad, aup_ref[...]))
        g_o[rows, :] = _dot(jax.nn.sigmoid(gd), gup_ref[...])
        kk = k * kk_ref[...]
        kk = kk * lax.rsqrt(jnp.maximum(_head_sum(kk * kk, ones_bd, 1), 1e-24))
        k2 = k * (1.0 + (asig - 1.0) * ka_ref[...])
        r_o[rows, :] = r
        k_o[rows, :] = k2
        v_o[rows, :] = v
        a_o[rows, :] = -kk
        b_o[rows, :] = kk * asig
        bonus_o[rows, :] = _head_sum(r * k2 * rk_ref[...], ones_bd, 2) * v

        gate = h[t][:, n_c:n_c + w]
        ext = xr_ref[t * sub:(t + 1) * sub + CONV_HALO, :]
        xc = cw_ref[CONV_WIDTH - 1:CONV_WIDTH, :] * ext[CONV_HALO:, :] + cb_ref[...]
        for j in range(1, CONV_WIDTH):
            xc = xc + cw_ref[CONV_WIDTH - 1 - j:CONV_WIDTH - j, :] * pltpu.roll(ext, j, axis=0)[CONV_HALO:, :]
        rec = jax.nn.sigmoid(_dot(xc, wa_ref[...]) + ba_ref[...])
        inp = jax.nn.sigmoid(_dot(xc, wx_ref[...]) + bx_ref[...])
        log_a = -LRU_C * rec * _softplus(-lam_ref[...])
        a = jnp.exp(log_a)
        bx = jnp.sqrt(-jnp.tanh(log_a) * (a * a + 1.0)) * (inp * xc)
        a3 = a.reshape(ngrp, SUBLANES, w)
        b3 = bx.reshape(ngrp, SUBLANES, w)
        d = 1
        while d < SUBLANES:
            keep = sub_id >= d
            a_sh = jnp.where(keep, pltpu.roll(a3, d, axis=1), 1.0)
            b_sh = jnp.where(keep, pltpu.roll(b3, d, axis=1), 0.0)
            b3 = a3 * b_sh + b3
            a3 = a3 * a_sh
            d *= 2
        gate_act = jax.nn.gelu(gate)
        for j in range(ngrp):
            hj = a3[j] * hcar + b3[j]
            lo = t * sub + j * SUBLANES
            yd_o[lo:lo + SUBLANES, :] = hj * gate_act[j * SUBLANES:(j + 1) * SUBLANES, :]
            hcar = hj[SUBLANES - 1:SUBLANES, :]
    hprev_ref[...] = hprev
    hlru_ref[...] = hcar
    xr_ref[0:CONV_HALO, :] = xr_ref[tm:tm + CONV_HALO, :]


def _odd_pre(xf, w_in, mu, w0, w_up, a0, a_up, g_up, k_k, k_a, r_k, conv_w, conv_b, wa_bd, b_a, wx_bd, b_x,
             lam, ones_bd, e, nb, ns):
    m, d = xf.shape
    w = w0.shape[-1]
    full = lambda arr: pl.BlockSpec((None,) + arr.shape[1:], lambda b, s: (e,) + (0,) * (arr.ndim - 1))
    row = pl.BlockSpec((ROW_TILE, w), lambda b, s: (b * ns + s, 0))
    params = (w_in, mu, w0, w_up, a0, a_up, g_up, k_k, k_a, r_k, conv_w, conv_b, wa_bd, b_a, wx_bd, b_x, lam)
    return pl.pallas_call(
        _odd_pre_kernel,
        grid=(nb, ns),
        in_specs=[pl.BlockSpec((ROW_TILE, d), lambda b, s: (b * ns + s, 0))]
        + [full(p) for p in params]
        + [pl.BlockSpec(ones_bd.shape, lambda b, s: (0, 0))],
        out_specs=[row] * 9,
        out_shape=[jax.ShapeDtypeStruct((m, w), F32)] * 9,
        scratch_shapes=[pltpu.VMEM((1, mu.shape[-1]), F32),
                        pltpu.VMEM((CONV_HALO + ROW_TILE, w), F32),
                        pltpu.VMEM((1, w), F32)],
        compiler_params=_cparams(),
        name="odd_pre",
    )(xf, *params, ones_bd)


def _rwkv_kernel(r_ref, lw_ref, k_ref, v_ref, a_ref, b_ref, y_ref, t_ref, q_ref, g_ref, h_ref):
    s = pl.program_id(1)
    tm, w = r_ref.shape
    n = RWKV_HEAD_DIM
    lc = RWKV_CHUNK
    gw = MXU_WIDTH
    hpg = gw // n
    groups = range(w // gw)

    @pl.when(s == 0)
    def _():
        t_ref[...] = jnp.zeros(t_ref.shape, F32)

    row_t = lax.broadcasted_iota(jnp.int32, (lc, gw), 0)
    col = lax.broadcasted_iota(jnp.int32, (lc, gw), 1)
    col_s = col & (n - 1)
    col_h = col >> (n.bit_length() - 1)
    strict = row_t > col_s
    incl = row_t >= col_s
    eye_m = row_t == col_s
    eye = eye_m.astype(F32)
    blk_r = lax.broadcasted_iota(jnp.int32, (gw, gw), 0) >> (n.bit_length() - 1)
    blk_c = lax.broadcasted_iota(jnp.int32, (gw, gw), 1) >> (n.bit_length() - 1)
    bd_mask = (blk_r == blk_c).astype(BF16)
    rows = _row_ids(lc, w)

    def bd(x):
        xb = x.astype(BF16)
        return jnp.concatenate([xb] * hpg, axis=0) * bd_mask

    def mm(a, b_bf16, dn=_NN):
        return lax.dot_general(a.astype(BF16), b_bf16, dn, preferred_element_type=F32)

    def diag_blocks(p):
        out = p[(hpg - 1) * n:, :]
        for h in range(hpg - 1):
            out = jnp.where(col_h == h, p[h * n:(h + 1) * n, :], out)
        return out

    def state_free(i):
        at, rt, bt, kt, bh, kh, v, pend, where = [], [], [], [], [], [], [], [], []
        for ci in range(RWKV_INTERLEAVE):
            sl = pl.ds((i * RWKV_INTERLEAVE + ci) * lc, lc)
            lw = lw_ref[sl, :]
            cum = lw
            d = 1
            while d < lc:
                cum = cum + jnp.where(rows >= d, pltpu.roll(cum, d, axis=0), 0.0)
                d *= 2
            cl = cum[lc - 1:lc, :]
            einv = jnp.exp(-cum)
            eend = jnp.exp(cl - cum)
            a = a_ref[sl, :]
            b = b_ref[sl, :]
            k = k_ref[sl, :]
            full = (a * jnp.exp(cum - lw), r_ref[sl, :] * jnp.exp(cum), b * einv, k * einv, b * eend, k * eend,
                    v_ref[sl, :], jnp.exp(cl))
            for g in groups:
                for dst, t in zip((at, rt, bt, kt, bh, kh, v, pend), full):
                    dst.append(t[:, g * gw:(g + 1) * gw])
                where.append((sl, slice(g * gw, (g + 1) * gw)))
            yield
        units = range(len(where))
        lhs = [jnp.concatenate([at[u], rt[u]], axis=0).astype(BF16) for u in units]
        sb = [mm(lhs[u], bd(bt[u]), _NT) for u in units]
        yield
        sk = [mm(lhs[u], bd(kt[u]), _NT) for u in units]
        a_ab = [jnp.where(strict, t[:lc], 0.0) for t in sb]
        m_rb = [jnp.where(incl, t[lc:], 0.0) for t in sb]
        yield
        a_ak = [jnp.where(strict, t[:lc], 0.0) for t in sk]
        m_rk = [jnp.where(incl, t[lc:], 0.0) for t in sk]
        inv = [eye + t for t in a_ab]
        apow = [mm(a_ab[u], bd(a_ab[u])) for u in units]
        yield
        xm = [mm(jnp.concatenate([a_ak[u], m_rk[u]], axis=0), bd(v[u])) for u in units]
        x0 = [t[:lc] for t in xm]
        ykv = [t[lc:] for t in xm]
        yield
        span = 4
        while span < lc:
            p = [mm(jnp.concatenate([inv[u], apow[u]], axis=0), bd(apow[u])) for u in units]
            inv = [inv[u] + p[u][:lc] for u in units]
            apow = [t[lc:] for t in p]
            span *= 2
            yield
        inv = [inv[u] + mm(inv[u], bd(apow[u])) for u in units]
        yield
        ap = [mm(inv[u], bd(at[u])) for u in units]
        yield
        vp = [mm(inv[u], bd(x0[u])) for u in units]
        yield
        q = [rt[u] + mm(m_rb[u], bd(ap[u])) for u in units]
        for u in units:
            q_ref[where[u]] = q[u]
        yield
        yl = [mm(m_rb[u], bd(vp[u])) + ykv[u] for u in units]
        for u in units:
            y_ref[where[u]] = yl[u]
        yield
        gm = [diag_blocks(mm(bh[u], ap[u].astype(BF16), _TN))
              + jnp.where(eye_m, jnp.broadcast_to(pend[u], (lc, gw)), 0.0) for u in units]
        for u in units:
            g_ref[where[u]] = gm[u]
        yield
        hm = [diag_blocks(mm(jnp.concatenate([bh[u], kh[u]], axis=0),
                             jnp.concatenate([vp[u], v[u]], axis=0).astype(BF16), _TN)) for u in units]
        for u in units:
            h_ref[where[u]] = hm[u]

    def state_step(c):
        sl = pl.ds(c * lc, lc)
        fin = [mm(jnp.concatenate([q_ref[sl, g * gw:(g + 1) * gw], g_ref[sl, g * gw:(g + 1) * gw]], axis=0),
                  bd(t_ref[:, g * gw:(g + 1) * gw])) for g in groups]
        for g in groups:
            gs = slice(g * gw, (g + 1) * gw)
            y_ref[sl, gs] = y_ref[sl, gs] + fin[g][:lc]
            t_ref[:, gs] = fin[g][lc:] + h_ref[sl, gs]

    ngroups = tm // (lc * RWKV_INTERLEAVE)
    running, ready_chunks, slot, started = [], [], 0, 0
    while started < ngroups or running or ready_chunks:
        if started < ngroups and slot == started * RWKV_LAG:
            running.append((state_free(started), started))
            started += 1
        still = []
        for gen, gi in running:
            try:
                next(gen)
                still.append((gen, gi))
            except StopIteration:
                ready_chunks.extend(range(gi * RWKV_INTERLEAVE, (gi + 1) * RWKV_INTERLEAVE))
        running = still
        if ready_chunks:
            state_step(ready_chunks.pop(0))
        slot += 1


def _rwkv_scan(r, lw, k, v, a, b, nb, ns):
    m, w = r.shape
    assert RWKV_CHUNK == RWKV_HEAD_DIM and w % MXU_WIDTH == 0 and MXU_WIDTH % RWKV_HEAD_DIM == 0
    assert RWKV_HEAD_DIM & (RWKV_HEAD_DIM - 1) == 0
    row = pl.BlockSpec((ROW_TILE, w), lambda bb, s: (bb * ns + s, 0))
    return pl.pallas_call(
        _rwkv_kernel,
        grid=(nb, ns),
        in_specs=[row] * 6,
        out_specs=row,
        out_shape=jax.ShapeDtypeStruct((m, w), F32),
        scratch_shapes=[pltpu.VMEM((RWKV_HEAD_DIM, w), F32)]
        + [pltpu.VMEM((ROW_TILE, w), F32)] * 3,
        compiler_params=_cparams(),
        name="rwkv7",
    )(r, lw, k, v, a, b)


def _odd_post_kernel(x_ref, y_ref, g_ref, bonus_ref, yd_ref, gng_ref, gnb_ref, ones_ref, wout_ref,
                     lg_ref, lb_ref, o_ref):
    w = y_ref.shape[-1]
    sub = SUB_ROWS
    tiles = [slice(t * sub, (t + 1) * sub) for t in range(x_ref.shape[0] // sub)]
    ones_bd = ones_ref[...]
    inv_n = 1.0 / RWKV_HEAD_DIM
    y = [y_ref[rows, :] for rows in tiles]
    mu = [_head_sum(t, ones_bd, 2) * inv_n for t in y]
    yc = [a - b for a, b in zip(y, mu)]
    var = [_head_sum(t * t, ones_bd, 1) * inv_n for t in yc]
    out = []
    for t, rows in enumerate(tiles):
        yn = yc[t] * lax.rsqrt(var[t] + RWKV_GN_EPS) * gng_ref[...] + gnb_ref[...]
        y_rwkv = (yn + bonus_ref[rows, :]) * g_ref[rows, :]
        out.append(_dot(y_rwkv, wout_ref[0:w, :]) + _dot(yd_ref[rows, :], wout_ref[w:2 * w, :]))
    for t, rows in enumerate(tiles):
        o_ref[rows, :] = _layer_norm(DEEPNORM_ALPHA * x_ref[rows, :] + out[t], lg_ref[...], lb_ref[...])


def _odd_post(xf, y, g, bonus, yd, gn_g, gn_b, ones_bd, w_out, ln_g, ln_b, e, l, nb, ns):
    m, d = xf.shape
    w = y.shape[-1]
    row = pl.BlockSpec((WIDE_TILE, w), lambda b, s: (b * ns + s, 0))
    xrow = pl.BlockSpec((WIDE_TILE, d), lambda b, s: (b * ns + s, 0))
    return pl.pallas_call(
        _odd_post_kernel,
        grid=(nb, ns),
        in_specs=[xrow, row, row, row, row,
                  pl.BlockSpec((None, 1, w), lambda b, s: (e, 0, 0)),
                  pl.BlockSpec((None, 1, w), lambda b, s: (e, 0, 0)),
                  pl.BlockSpec(ones_bd.shape, lambda b, s: (0, 0)),
                  pl.BlockSpec((None, 2 * w, d), lambda b, s: (e, 0, 0)),
                  pl.BlockSpec((None, 1, d), lambda b, s: (4 * l + 1, 0, 0)),
                  pl.BlockSpec((None, 1, d), lambda b, s: (4 * l + 1, 0, 0))],
        out_specs=xrow,
        out_shape=jax.ShapeDtypeStruct((m, d), F32),
        compiler_params=_cparams(),
        name="odd_post",
    )(xf, y, g, bonus, yd, gn_g, gn_b, ones_bd, w_out, ln_g, ln_b)


def _kv_kernel(mem_ref, wkv_ref, kv_ref):
    kv_ref[...] = _dot(mem_ref[...], wkv_ref[...]).astype(BF16)


def _kv_proj(mem, w_kv):
    nb, mlen, d = mem.shape
    nl, _, n2 = w_kv.shape
    return pl.pallas_call(
        _kv_kernel,
        grid=(nl, nb),
        in_specs=[pl.BlockSpec((None, mlen, d), lambda l, b: (b, 0, 0)),
                  pl.BlockSpec((None, d, n2), lambda l, b: (l, 0, 0))],
        out_specs=pl.BlockSpec((None, None, mlen, n2), lambda l, b: (l, b, 0, 0)),
        out_shape=jax.ShapeDtypeStruct((nl, nb, mlen, n2), BF16),
        compiler_params=_cparams(),
        name="xattn_kv",
    )(mem, w_kv)


def _xattn_kernel(x_ref, wq_ref, kv_ref, wo_ref, g_ref, b_ref, o_ref):
    d = x_ref.shape[-1]
    hd = d // XATTN_HEADS
    sub = SUB_ROWS
    nsub = x_ref.shape[0] // sub
    tiles = [slice(t * sub, (t + 1) * sub) for t in range(nsub)]
    heads = XATTN_HEADS
    n_units = nsub * heads
    q, p, proj, outs = {}, {}, {}, [[] for _ in tiles]
    q[0] = _dot(x_ref[tiles[0], :], wq_ref[...])
    for i in range(n_units + XATTN_SKEW + 1):
        if i < n_units:
            t, h = divmod(i, heads)
            if h == 1 and t + 1 < nsub:
                q[t + 1] = _dot(x_ref[tiles[t + 1], :], wq_ref[...])
            qh = q[t][:, h * hd:(h + 1) * hd].astype(BF16)
            sc = lax.dot_general(qh, kv_ref[:, h * hd:(h + 1) * hd], _NT, preferred_element_type=F32)
            sc = sc * (hd ** -0.5)
            sc = sc - jnp.max(sc, axis=-1, keepdims=True)
            e = jnp.exp(sc)
            p[i] = (e / jnp.sum(e, axis=-1, keepdims=True)).astype(BF16)
        j = i - XATTN_SKEW
        if 0 <= j < n_units:
            t, h = divmod(j, heads)
            outs[t].append(jnp.dot(p.pop(j), kv_ref[:, d + h * hd:d + (h + 1) * hd], preferred_element_type=F32))
        j = i - XATTN_SKEW - 1
        if 0 <= j < n_units and j % heads == heads - 1:
            t = j // heads
            proj[t] = _dot(jnp.concatenate(outs[t], axis=-1), wo_ref[...])
    for t, rows in enumerate(tiles):
        o_ref[rows, :] = _layer_norm(DEEPNORM_ALPHA * x_ref[rows, :] + proj[t], g_ref[...], b_ref[...])


def _xattn_layer(xf, w_q, kv, w_o, ln_g, ln_b, l, nb, ns):
    m, d = xf.shape
    mlen = kv.shape[2]
    xrow = pl.BlockSpec((XATTN_TILE, d), lambda b, s: (b * ns + s, 0))
    return pl.pallas_call(
        _xattn_kernel,
        grid=(nb, ns),
        in_specs=[xrow,
                  pl.BlockSpec((None, d, d), lambda b, s: (l, 0, 0)),
                  pl.BlockSpec((None, None, mlen, 2 * d), lambda b, s: (l, b, 0, 0)),
                  pl.BlockSpec((None, d, d), lambda b, s: (l, 0, 0)),
                  pl.BlockSpec((None, 1, d), lambda b, s: (4 * l + 2, 0, 0)),
                  pl.BlockSpec((None, 1, d), lambda b, s: (4 * l + 2, 0, 0))],
        out_specs=xrow,
        out_shape=jax.ShapeDtypeStruct((m, d), F32),
        compiler_params=_cparams(),
        name="xattn",
    )(xf, w_q, kv, w_o, ln_g, ln_b)


def _block_diag(w):
    nblk, bi, bj = w.shape[-3:]
    eye = jnp.eye(nblk, dtype=w.dtype)
    out = w[..., :, :, None, :] * eye[:, None, :, None]
    return out.reshape(w.shape[:-3] + (nblk * bi, nblk * bj))


def kernel(x, mem, ffn1_w_in, ffn1_w_out, ffn2_w_in, ffn2_w_out, ln_g, ln_b, xattn_w_q, xattn_w_kv, xattn_w_o, even_w_in, even_w_out, pool_w, pool_scale, sgu_ln_g, sgu_ln_b, sgu_w, sgu_b, odd_w_in, odd_w_out, rwkv_mu, rwkv_w0, rwkv_w_up, rwkv_a0, rwkv_a_up, rwkv_g_up, rwkv_k_k, rwkv_k_a, rwkv_r_k, rwkv_gn_g, rwkv_gn_b, lru_conv_w, lru_conv_b, lru_w_a, lru_b_a, lru_w_x, lru_b_x, lru_lambda):
    nb, seq, d = x.shape
    assert seq % WIDE_TILE == 0 and seq % XATTN_TILE == 0 and WIDE_TILE % ROW_TILE == 0 and ROW_TILE % SUB_ROWS == 0
    assert SUB_ROWS % SGU_CHUNK == 0 and ROW_TILE % (RWKV_CHUNK * RWKV_INTERLEAVE) == 0
    ns = seq // ROW_TILE
    nsw = seq // WIDE_TILE
    depth = ffn1_w_in.shape[0]
    xf = x.reshape(nb * seq, d)

    bf = lambda t: t.astype(BF16)
    row3 = lambda t: t.reshape(t.shape[0], 1, -1)
    lg = ln_g.reshape(-1, 1, d)
    lb = ln_b.reshape(-1, 1, d)
    f1_in, f1_out, f2_in, f2_out = bf(ffn1_w_in), bf(ffn1_w_out), bf(ffn2_w_in), bf(ffn2_w_out)
    wq, wo = bf(xattn_w_q), bf(xattn_w_o)
    kv = _kv_proj(mem, bf(xattn_w_kv))
    ev_in, ev_out = bf(even_w_in), bf(even_w_out)
    od_in, od_out = bf(odd_w_in), bf(odd_w_out)
    n_e, n_pool, pool_dim = pool_w.shape[:3]
    pw = bf(_block_diag(pool_w.reshape(n_e, n_pool // POOL_PACK, POOL_PACK, pool_dim, pool_dim)))
    sgu_bt = jnp.swapaxes(sgu_b, 1, 2)
    wa_bd, wx_bd = bf(_block_diag(lru_w_a)), bf(_block_diag(lru_w_x))
    w_rwkv = rwkv_w0.shape[-1]
    ones_bd = _block_diag(jnp.ones((w_rwkv // RWKV_HEAD_DIM, RWKV_HEAD_DIM, RWKV_HEAD_DIM), BF16))
    r_k = rwkv_r_k.reshape(rwkv_r_k.shape[0], 1, -1)

    for l in range(depth):
        xf = _ffn_layer(xf, f1_in, f1_out, lg, lb, l, 0)
        e = l // 2
        if l % 2 == 0:
            xf = _even_layer(xf, ev_in, ev_out, pw, row3(pool_scale), row3(sgu_ln_g), row3(sgu_ln_b),
                             sgu_w, sgu_bt, lg, lb, e, l, nb, nsw)
        else:
            r, lw, k, v, a, b, g, bonus, yd = _odd_pre(
                xf, od_in, row3(rwkv_mu), row3(rwkv_w0), bf(rwkv_w_up), row3(rwkv_a0), bf(rwkv_a_up),
                bf(rwkv_g_up), row3(rwkv_k_k), row3(rwkv_k_a), r_k, lru_conv_w, row3(lru_conv_b),
                wa_bd, row3(lru_b_a), wx_bd, row3(lru_b_x), row3(lru_lambda), ones_bd, e, nb, ns)
            y = _rwkv_scan(r, lw, k, v, a, b, nb, ns)
            xf = _odd_post(xf, y, g, bonus, yd, row3(rwkv_gn_g), row3(rwkv_gn_b), ones_bd, od_out,
                           lg, lb, e, l, nb, nsw)
        xf = _xattn_layer(xf, wq, kv, wo, lg, lb, l, nb, seq // XATTN_TILE)
        xf = _ffn_layer(xf, f2_in, f2_out, lg, lb, l, 3)
    return xf.reshape(nb, seq, d)
```

```python
import functools

import jax
import jax.numpy as jnp
from jax import lax
from jax.experimental import pallas as pl
from jax.experimental.pallas import tpu as pltpu

F32 = jnp.float32
BF16 = jnp.bfloat16

DEPTH = 4
LN_EPS = 1e-5
DEEPNORM_ALPHA = (2 * DEPTH) ** 0.25
MACARON_WEIGHT = 0.5
POOL_WINDOWS = (2, 4, 8, 16)
POOL_HALO = 16
POOL_PACK = 2
SGU_CHUNK = 128
XATTN_HEADS = 4
XATTN_SKEW = 2
RWKV_HEAD_DIM = 64
RWKV_CHUNK = 64
RWKV_GN_EPS = 64e-5
MXU_WIDTH = 256
RWKV_INTERLEAVE = 2
RWKV_LAG = 4
LRU_C = 8.0
CONV_WIDTH = 4
SUBLANES = 8
CONV_HALO = SUBLANES

ROW_TILE = 512
FFN_TILE = 1024
WIDE_TILE = 1024
XATTN_TILE = 1024
SUB_ROWS = 256
FFN_CHUNK = 256
VMEM_LIMIT = 56 * 1024 * 1024


def _cparams(grid_rank=2):
    return pltpu.CompilerParams(dimension_semantics=("arbitrary",) * grid_rank,
                                vmem_limit_bytes=VMEM_LIMIT)


def _dot(a, b):
    return jnp.dot(a.astype(BF16), b.astype(BF16), preferred_element_type=F32)


_NN = (((1,), (0,)), ((), ()))
_NT = (((1,), (1,)), ((), ()))
_TN = (((0,), (0,)), ((), ()))


def _mm1(a, b, dn=_NN):
    return lax.dot_general(a.astype(BF16), b.astype(BF16), dn, preferred_element_type=F32)


def _head_sum(x, ones_bd, pieces):
    total = None
    rest = x
    for i in range(pieces):
        p = rest.astype(BF16)
        if i + 1 < pieces:
            rest = rest - p.astype(F32)
        term = jnp.dot(p, ones_bd, preferred_element_type=F32)
        total = term if total is None else total + term
    return total


def _layer_norm(y, g, b, eps=LN_EPS):
    mu = jnp.mean(y, axis=-1, keepdims=True)
    yc = y - mu
    var = jnp.mean(yc * yc, axis=-1, keepdims=True)
    return yc * lax.rsqrt(var + eps) * g + b


def _softplus(t):
    return jnp.maximum(t, 0.0) + jnp.log(1.0 + jnp.exp(-jnp.abs(t)))


def _row_ids(rows, cols):
    return lax.broadcasted_iota(jnp.int32, (rows, cols), 0)


def _ffn_kernel(x_ref, win_ref, wout_ref, g_ref, b_ref, o_ref, h_ref, *, d_ff):
    sub = SUB_ROWS
    nsub = x_ref.shape[0] // sub
    accs = []
    for t in range(nsub):
        rows = slice(t * sub, (t + 1) * sub)
        xb = x_ref[rows, :].astype(BF16)
        for c in range(d_ff // FFN_CHUNK):
            lo = c * FFN_CHUNK
            gate = jnp.dot(xb, win_ref[:, lo:lo + FFN_CHUNK], preferred_element_type=F32)
            up = jnp.dot(xb, win_ref[:, d_ff + lo:d_ff + lo + FFN_CHUNK], preferred_element_type=F32)
            h_ref[rows, lo:lo + FFN_CHUNK] = (gate * jax.nn.sigmoid(gate) * up).astype(BF16)
        accs.append(jnp.dot(h_ref[rows, :], wout_ref[...], preferred_element_type=F32))
    for t in range(nsub):
        rows = slice(t * sub, (t + 1) * sub)
        y = DEEPNORM_ALPHA * x_ref[rows, :] + MACARON_WEIGHT * accs[t]
        o_ref[rows, :] = _layer_norm(y, g_ref[...], b_ref[...])


def _ffn_layer(xf, w_in, w_out, ln_g, ln_b, l, j):
    m, d = xf.shape
    d_ff = w_out.shape[1]
    assert m % FFN_TILE == 0
    once = pl.Buffered(1)
    return pl.pallas_call(
        functools.partial(_ffn_kernel, d_ff=d_ff),
        grid=(m // FFN_TILE,),
        in_specs=[
            pl.BlockSpec((FFN_TILE, d), lambda i: (i, 0)),
            pl.BlockSpec((None, d, 2 * d_ff), lambda i: (l, 0, 0), pipeline_mode=once),
            pl.BlockSpec((None, d_ff, d), lambda i: (l, 0, 0), pipeline_mode=once),
            pl.BlockSpec((None, 1, d), lambda i: (4 * l + j, 0, 0)),
            pl.BlockSpec((None, 1, d), lambda i: (4 * l + j, 0, 0)),
        ],
        out_specs=pl.BlockSpec((FFN_TILE, d), lambda i: (i, 0)),
        out_shape=jax.ShapeDtypeStruct((m, d), F32),
        scratch_shapes=[pltpu.VMEM((FFN_TILE, d_ff), BF16)],
        compiler_params=_cparams(1),
        name="ffn",
    )(xf, w_in, w_out, ln_g, ln_b)


def _even_kernel(x_ref, win_ref, wout_ref, poolw_ref, pscale_ref, sg_ref, sb_ref, sw_ref, sbt_ref,
                 g_ref, b_ref, o_ref, ext_ref):
    s = pl.program_id(1)
    tm = x_ref.shape[0]
    width = pscale_ref.shape[-1]
    pw = width // len(POOL_WINDOWS)
    heads = sw_ref.shape[0]
    hw = width // heads
    sub = SUB_ROWS
    nchunk = sub // SGU_CHUNK
    subs = range(tm // sub)
    tiles = [slice(t * sub, (t + 1) * sub) for t in subs]

    @pl.when(s == 0)
    def _():
        ext_ref[0:POOL_HALO, :] = jnp.zeros((POOL_HALO, width), F32)

    h = [_dot(x_ref[rows, :], win_ref[...]) for rows in tiles]
    for t in subs:
        ext_ref[POOL_HALO + t * sub:POOL_HALO + (t + 1) * sub, :] = h[t][:, :width]

    ti = lax.broadcasted_iota(jnp.int32, (SGU_CHUNK, SGU_CHUNK), 0)
    si = lax.broadcasted_iota(jnp.int32, (SGU_CHUNK, SGU_CHUNK), 1)
    w_sgu = [jnp.where(ti >= si, sw_ref[hd], 0.0).astype(BF16) for hd in range(heads)]

    ya, yb = [], []
    for t in subs:
        ext = ext_ref[t * sub:(t + 1) * sub + POOL_HALO, :]
        pos = (s * tm + t * sub + 1 + _row_ids(sub, 1)).astype(F32)
        pooled = []
        for g, win in enumerate(POOL_WINDOWS):
            acc = ext[:, g * pw:(g + 1) * pw]
            d = 1
            while d < win:
                acc = acc + pltpu.roll(acc, d, axis=0)
                d *= 2
            mean = acc[POOL_HALO:, :] / jnp.minimum(pos, float(win))
            pooled.append(mean - h[t][:, g * pw:(g + 1) * pw])
        yat = [_dot(jnp.concatenate(pooled[p * POOL_PACK:(p + 1) * POOL_PACK], axis=-1), poolw_ref[p])
               for p in range(len(POOL_WINDOWS) // POOL_PACK)]
        ya.append(jnp.concatenate(yat, axis=-1) * pscale_ref[...])

        u = jax.nn.gelu(h[t][:, width:2 * width])
        v = jax.nn.gelu(h[t][:, 2 * width:3 * width])
        vb = _layer_norm(v, sg_ref[...], sb_ref[...]).astype(BF16)
        mixed = []
        for hd in range(heads):
            bias = sbt_ref[:, hd:hd + 1]
            vh = jnp.concatenate([vb[c * SGU_CHUNK:(c + 1) * SGU_CHUNK, hd * hw:(hd + 1) * hw]
                                  for c in range(nchunk)], axis=1)
            mixed.append(jnp.dot(w_sgu[hd], vh, preferred_element_type=F32) + bias)
        yb.append(u * jnp.concatenate(
            [jnp.concatenate([mixed[hd][:, c * hw:(c + 1) * hw] for hd in range(heads)], axis=1)
             for c in range(nchunk)], axis=0))

    out = [_dot(ya[t], wout_ref[0:width, :]) + _dot(yb[t], wout_ref[width:2 * width, :]) for t in subs]
    for t, rows in enumerate(tiles):
        o_ref[rows, :] = _layer_norm(DEEPNORM_ALPHA * x_ref[rows, :] + out[t], g_ref[...], b_ref[...])
    ext_ref[0:POOL_HALO, :] = ext_ref[tm:tm + POOL_HALO, :]


def _even_layer(xf, w_in, w_out, pool_w, pool_scale, sgu_g, sgu_b, sgu_w, sgu_bt, ln_g, ln_b, e, l, nb, ns):
    m, d = xf.shape
    n_in = w_in.shape[-1]
    width = pool_scale.shape[-1]
    full = lambda shape: pl.BlockSpec((None,) + shape, lambda b, s: (e,) + (0,) * len(shape))
    return pl.pallas_call(
        _even_kernel,
        grid=(nb, ns),
        in_specs=[
            pl.BlockSpec((WIDE_TILE, d), lambda b, s: (b * ns + s, 0)),
            full((d, n_in)),
            full((2 * width, d)),
            full(pool_w.shape[1:]),
            full((1, width)),
            full((1, width)),
            full((1, width)),
            full(sgu_w.shape[1:]),
            full(sgu_bt.shape[1:]),
            pl.BlockSpec((None, 1, d), lambda b, s: (4 * l + 1, 0, 0)),
            pl.BlockSpec((None, 1, d), lambda b, s: (4 * l + 1, 0, 0)),
        ],
        out_specs=pl.BlockSpec((WIDE_TILE, d), lambda b, s: (b * ns + s, 0)),
        out_shape=jax.ShapeDtypeStruct((m, d), F32),
        scratch_shapes=[pltpu.VMEM((POOL_HALO + WIDE_TILE, width), F32)],
        compiler_params=_cparams(),
        name="even_mixer",
    )(xf, w_in, w_out, pool_w, pool_scale, sgu_g, sgu_b, sgu_w, sgu_bt, ln_g, ln_b)


def _odd_pre_kernel(x_ref, win_ref, mu_ref, w0_ref, wup_ref, a0_ref, aup_ref, gup_ref, kk_ref, ka_ref,
                    rk_ref, cw_ref, cb_ref, wa_ref, ba_ref, wx_ref, bx_ref, lam_ref, ones_ref,
                    r_o, lw_o, k_o, v_o, a_o, b_o, g_o, bonus_o, yd_o,
                    hprev_ref, xr_ref, hlru_ref):
    s = pl.program_id(1)
    tm = x_ref.shape[0]
    w = w0_ref.shape[-1]
    n_c = mu_ref.shape[-1]
    sub = SUB_ROWS
    nsub = tm // sub
    tiles = [slice(t * sub, (t + 1) * sub) for t in range(nsub)]
    dr = wup_ref.shape[0]
    ar = aup_ref.shape[0]
    ones_bd = ones_ref[...]

    @pl.when(s == 0)
    def _():
        hprev_ref[...] = jnp.zeros(hprev_ref.shape, F32)
        xr_ref[0:CONV_HALO, :] = jnp.zeros((CONV_HALO, w), F32)
        hlru_ref[...] = jnp.zeros(hlru_ref.shape, F32)

    first = _row_ids(sub, 1) == 0
    ngrp = sub // SUBLANES
    sub_id = lax.broadcasted_iota(jnp.int32, (ngrp, SUBLANES, w), 1)

    def project_rwkv(t):
        return _dot(x_ref[tiles[t], :], win_ref[:, :n_c])

    def project_lru(t):
        hd = _dot(x_ref[tiles[t], :], win_ref[:, n_c:])
        xr_ref[CONV_HALO + t * sub:CONV_HALO + (t + 1) * sub, :] = hd[:, w:]
        return hd[:, :w]

    def rwkv_part(t, hc, hprev):
        rows = tiles[t]
        prev = jnp.where(first, hprev, pltpu.roll(hc, 1, axis=0))
        hc = hc + mu_ref[...] * (prev - hc)
        r = hc[:, 0:w]
        k = hc[:, w:2 * w]
        v = hc[:, 2 * w:3 * w]
        rest = hc[:, 3 * w:]
        wd = rest[:, 0:dr]
        ad = rest[:, dr:dr + ar]
        gd = rest[:, dr + ar:]

        z = w0_ref[...] + _dot(jnp.tanh(wd), wup_ref[...])
        logw = -_softplus(-z) - 0.5
        lw_o[rows, :] = -jnp.exp(logw)
        asig = jax.nn.sigmoid(a0_ref[...] + _dot(ad, aup_ref[...]))
        g_o[rows, :] = _dot(jax.nn.sigmoid(gd), gup_ref[...])
        kk = k * kk_ref[...]
        kk = kk * lax.rsqrt(jnp.maximum(_head_sum(kk * kk, ones_bd, 1), 1e-24))
        k2 = k * (1.0 + (asig - 1.0) * ka_ref[...])
        r_o[rows, :] = r
        k_o[rows, :] = k2
        v_o[rows, :] = v
        a_o[rows, :] = -kk
        b_o[rows, :] = kk * asig
        bonus_o[rows, :] = _head_sum(r * k2 * rk_ref[...], ones_bd, 2) * v

    def lru_part(t, gate, hcar):
        ext = xr_ref[t * sub:(t + 1) * sub + CONV_HALO, :]
        xc = cw_ref[CONV_WIDTH - 1:CONV_WIDTH, :] * ext[CONV_HALO:, :] + cb_ref[...]
        for j in range(1, CONV_WIDTH):
            xc = xc + cw_ref[CONV_WIDTH - 1 - j:CONV_WIDTH - j, :] * pltpu.roll(ext, j, axis=0)[CONV_HALO:, :]
        rec = jax.nn.sigmoid(_dot(xc, wa_ref[...]) + ba_ref[...])
        inp = jax.nn.sigmoid(_dot(xc, wx_ref[...]) + bx_ref[...])
        log_a = -LRU_C * rec * _softplus(-lam_ref[...])
        a = jnp.exp(log_a)
        bx = jnp.sqrt(-jnp.tanh(log_a) * (a * a + 1.0)) * (inp * xc)
        a3 = a.reshape(ngrp, SUBLANES, w)
        b3 = bx.reshape(ngrp, SUBLANES, w)
        d = 1
        while d < SUBLANES:
            keep = sub_id >= d
            a_sh = jnp.where(keep, pltpu.roll(a3, d, axis=1), 1.0)
            b_sh = jnp.where(keep, pltpu.roll(b3, d, axis=1), 0.0)
            b3 = a3 * b_sh + b3
            a3 = a3 * a_sh
            d *= 2
        gate_act = jax.nn.gelu(gate)
        for j in range(ngrp):
            hj = a3[j] * hcar + b3[j]
            lo = t * sub + j * SUBLANES
            yd_o[lo:lo + SUBLANES, :] = hj * gate_act[j * SUBLANES:(j + 1) * SUBLANES, :]
            hcar = hj[SUBLANES - 1:SUBLANES, :]
        return hcar

    hprev = hprev_ref[...]
    hcar = hlru_ref[...]
    hc = project_rwkv(0)
    gate = project_lru(0)
    for t in range(nsub):
        last = t + 1 == nsub
        hc_next = None if last else project_rwkv(t + 1)
        rwkv_part(t, hc, hprev)
        hprev = hc[sub - 1:sub, :]
        gate_next = None if last else project_lru(t + 1)
        hcar = lru_part(t, gate, hcar)
        hc, gate = hc_next, gate_next
    hprev_ref[...] = hprev
    hlru_ref[...] = hcar
    xr_ref[0:CONV_HALO, :] = xr_ref[tm:tm + CONV_HALO, :]


def _odd_pre(xf, w_in, mu, w0, w_up, a0, a_up, g_up, k_k, k_a, r_k, conv_w, conv_b, wa_bd, b_a, wx_bd, b_x,
             lam, ones_bd, e, nb, ns):
    m, d = xf.shape
    w = w0.shape[-1]
    full = lambda arr: pl.BlockSpec((None,) + arr.shape[1:], lambda b, s: (e,) + (0,) * (arr.ndim - 1))
    row = pl.BlockSpec((ROW_TILE, w), lambda b, s: (b * ns + s, 0))
    params = (w_in, mu, w0, w_up, a0, a_up, g_up, k_k, k_a, r_k, conv_w, conv_b, wa_bd, b_a, wx_bd, b_x, lam)
    return pl.pallas_call(
        _odd_pre_kernel,
        grid=(nb, ns),
        in_specs=[pl.BlockSpec((ROW_TILE, d), lambda b, s: (b * ns + s, 0))]
        + [full(p) for p in params]
        + [pl.BlockSpec(ones_bd.shape, lambda b, s: (0, 0))],
        out_specs=[row] * 9,
        out_shape=[jax.ShapeDtypeStruct((m, w), F32)] * 9,
        scratch_shapes=[pltpu.VMEM((1, mu.shape[-1]), F32),
                        pltpu.VMEM((CONV_HALO + ROW_TILE, w), F32),
                        pltpu.VMEM((1, w), F32)],
        compiler_params=_cparams(),
        name="odd_pre",
    )(xf, *params, ones_bd)


def _rwkv_kernel(r_ref, lw_ref, k_ref, v_ref, a_ref, b_ref, y_ref, t_ref, q_ref, g_ref, h_ref):
    s = pl.program_id(1)
    tm, w = r_ref.shape
    n = RWKV_HEAD_DIM
    lc = RWKV_CHUNK
    gw = MXU_WIDTH
    hpg = gw // n
    groups = range(w // gw)

    @pl.when(s == 0)
    def _():
        t_ref[...] = jnp.zeros(t_ref.shape, F32)

    row_t = lax.broadcasted_iota(jnp.int32, (lc, gw), 0)
    col = lax.broadcasted_iota(jnp.int32, (lc, gw), 1)
    col_s = col & (n - 1)
    col_h = col >> (n.bit_length() - 1)
    strict = row_t > col_s
    incl = row_t >= col_s
    eye_m = row_t == col_s
    eye = eye_m.astype(F32)
    merge_masks = [(row_t >> 1) == (col_s >> 1)]
    size = 2
    while size < lc:
        shift = size.bit_length()
        merge_masks.append(((row_t >> shift) == (col_s >> shift)) & ((row_t & size) != 0) & ((col_s & size) == 0))
        size *= 2
    blk_r = lax.broadcasted_iota(jnp.int32, (gw, gw), 0) >> (n.bit_length() - 1)
    blk_c = lax.broadcasted_iota(jnp.int32, (gw, gw), 1) >> (n.bit_length() - 1)
    bd_mask = (blk_r == blk_c).astype(BF16)
    rows = _row_ids(lc, w)

    def bd(x):
        xb = x.astype(BF16)
        return jnp.concatenate([xb] * hpg, axis=0) * bd_mask

    def mm(a, b_bf16, dn=_NN):
        return lax.dot_general(a.astype(BF16), b_bf16, dn, preferred_element_type=F32)

    def diag_blocks(p):
        out = p[(hpg - 1) * n:, :]
        for h in range(hpg - 1):
            out = jnp.where(col_h == h, p[h * n:(h + 1) * n, :], out)
        return out

    def state_free(i):
        at, rt, bt, kt, bh, kh, v, pend, where = [], [], [], [], [], [], [], [], []
        for ci in range(RWKV_INTERLEAVE):
            sl = pl.ds((i * RWKV_INTERLEAVE + ci) * lc, lc)
            lw = lw_ref[sl, :]
            cum = lw
            d = 1
            while d < lc:
                cum = cum + jnp.where(rows >= d, pltpu.roll(cum, d, axis=0), 0.0)
                d *= 2
            cl = cum[lc - 1:lc, :]
            einv = jnp.exp(-cum)
            eend = jnp.exp(cl - cum)
            a = a_ref[sl, :]
            b = b_ref[sl, :]
            k = k_ref[sl, :]
            full = (a * jnp.exp(cum - lw), r_ref[sl, :] * jnp.exp(cum), b * einv, k * einv, b * eend, k * eend,
                    v_ref[sl, :], jnp.exp(cl))
            for g in groups:
                for dst, t in zip((at, rt, bt, kt, bh, kh, v, pend), full):
                    dst.append(t[:, g * gw:(g + 1) * gw])
                where.append((sl, slice(g * gw, (g + 1) * gw)))
            yield
        units = range(len(where))
        lhs = [jnp.concatenate([at[u], rt[u]], axis=0).astype(BF16) for u in units]
        sb = [mm(lhs[u], bd(bt[u]), _NT) for u in units]
        yield
        sk = [mm(lhs[u], bd(kt[u]), _NT) for u in units]
        a_ab = [jnp.where(strict, t[:lc], 0.0) for t in sb]
        m_rb = [jnp.where(incl, t[lc:], 0.0) for t in sb]
        yield
        a_ak = [jnp.where(strict, t[:lc], 0.0) for t in sk]
        m_rk = [jnp.where(incl, t[lc:], 0.0) for t in sk]
        xm = [mm(jnp.concatenate([a_ak[u], m_rk[u]], axis=0), bd(v[u])) for u in units]
        x0 = [t[:lc] for t in xm]
        ykv = [t[lc:] for t in xm]
        yield
        inv = [eye + jnp.where(merge_masks[0], t, 0.0) for t in a_ab]
        for below in merge_masks[1:]:
            ew = [mm(jnp.where(below, a_ab[u], 0.0), bd(inv[u])) for u in units]
            yield
            inv = [inv[u] + mm(inv[u], bd(ew[u])) for u in units]
            yield
        ap = [mm(inv[u], bd(at[u])) for u in units]
        yield
        vp = [mm(inv[u], bd(x0[u])) for u in units]
        yield
        q = [rt[u] + mm(m_rb[u], bd(ap[u])) for u in units]
        for u in units:
            q_ref[where[u]] = q[u]
        yield
        yl = [mm(m_rb[u], bd(vp[u])) + ykv[u] for u in units]
        for u in units:
            y_ref[where[u]] = yl[u]
        yield
        gm = [diag_blocks(mm(bh[u], ap[u].astype(BF16), _TN))
              + jnp.where(eye_m, jnp.broadcast_to(pend[u], (lc, gw)), 0.0) for u in units]
        for u in units:
            g_ref[where[u]] = gm[u]
        yield
        hm = [diag_blocks(mm(jnp.concatenate([bh[u], kh[u]], axis=0),
                             jnp.concatenate([vp[u], v[u]], axis=0).astype(BF16), _TN)) for u in units]
        for u in units:
            h_ref[where[u]] = hm[u]

    def state_step(c):
        sl = pl.ds(c * lc, lc)
        fin = [mm(jnp.concatenate([q_ref[sl, g * gw:(g + 1) * gw], g_ref[sl, g * gw:(g + 1) * gw]], axis=0),
                  bd(t_ref[:, g * gw:(g + 1) * gw])) for g in groups]
        for g in groups:
            gs = slice(g * gw, (g + 1) * gw)
            y_ref[sl, gs] = y_ref[sl, gs] + fin[g][:lc]
            t_ref[:, gs] = fin[g][lc:] + h_ref[sl, gs]

    ngroups = tm // (lc * RWKV_INTERLEAVE)
    running, ready_chunks, slot, started = [], [], 0, 0
    while started < ngroups or running or ready_chunks:
        if started < ngroups and slot == started * RWKV_LAG:
            running.append((state_free(started), started))
            started += 1
        still = []
        for gen, gi in running:
            try:
                next(gen)
                still.append((gen, gi))
            except StopIteration:
                ready_chunks.extend(range(gi * RWKV_INTERLEAVE, (gi + 1) * RWKV_INTERLEAVE))
        running = still
        if ready_chunks:
            state_step(ready_chunks.pop(0))
        slot += 1


def _rwkv_scan(r, lw, k, v, a, b, nb, ns):
    m, w = r.shape
    assert RWKV_CHUNK == RWKV_HEAD_DIM and w % MXU_WIDTH == 0 and MXU_WIDTH % RWKV_HEAD_DIM == 0
    assert RWKV_HEAD_DIM & (RWKV_HEAD_DIM - 1) == 0
    row = pl.BlockSpec((ROW_TILE, w), lambda bb, s: (bb * ns + s, 0))
    return pl.pallas_call(
        _rwkv_kernel,
        grid=(nb, ns),
        in_specs=[row] * 6,
        out_specs=row,
        out_shape=jax.ShapeDtypeStruct((m, w), F32),
        scratch_shapes=[pltpu.VMEM((RWKV_HEAD_DIM, w), F32)]
        + [pltpu.VMEM((ROW_TILE, w), F32)] * 3,
        compiler_params=_cparams(),
        name="rwkv7",
    )(r, lw, k, v, a, b)


def _odd_post_kernel(x_ref, y_ref, g_ref, bonus_ref, yd_ref, gng_ref, gnb_ref, ones_ref, wout_ref,
                     lg_ref, lb_ref, o_ref):
    w = y_ref.shape[-1]
    sub = SUB_ROWS
    tiles = [slice(t * sub, (t + 1) * sub) for t in range(x_ref.shape[0] // sub)]
    ones_bd = ones_ref[...]
    inv_n = 1.0 / RWKV_HEAD_DIM
    y = [y_ref[rows, :] for rows in tiles]
    mu = [_head_sum(t, ones_bd, 2) * inv_n for t in y]
    yc = [a - b for a, b in zip(y, mu)]
    var = [_head_sum(t * t, ones_bd, 1) * inv_n for t in yc]
    out = []
    for t, rows in enumerate(tiles):
        yn = yc[t] * lax.rsqrt(var[t] + RWKV_GN_EPS) * gng_ref[...] + gnb_ref[...]
        y_rwkv = (yn + bonus_ref[rows, :]) * g_ref[rows, :]
        out.append(_dot(y_rwkv, wout_ref[0:w, :]) + _dot(yd_ref[rows, :], wout_ref[w:2 * w, :]))
    for t, rows in enumerate(tiles):
        o_ref[rows, :] = _layer_norm(DEEPNORM_ALPHA * x_ref[rows, :] + out[t], lg_ref[...], lb_ref[...])


def _odd_post(xf, y, g, bonus, yd, gn_g, gn_b, ones_bd, w_out, ln_g, ln_b, e, l, nb, ns):
    m, d = xf.shape
    w = y.shape[-1]
    row = pl.BlockSpec((WIDE_TILE, w), lambda b, s: (b * ns + s, 0))
    xrow = pl.BlockSpec((WIDE_TILE, d), lambda b, s: (b * ns + s, 0))
    return pl.pallas_call(
        _odd_post_kernel,
        grid=(nb, ns),
        in_specs=[xrow, row, row, row, row,
                  pl.BlockSpec((None, 1, w), lambda b, s: (e, 0, 0)),
                  pl.BlockSpec((None, 1, w), lambda b, s: (e, 0, 0)),
                  pl.BlockSpec(ones_bd.shape, lambda b, s: (0, 0)),
                  pl.BlockSpec((None, 2 * w, d), lambda b, s: (e, 0, 0)),
                  pl.BlockSpec((None, 1, d), lambda b, s: (4 * l + 1, 0, 0)),
                  pl.BlockSpec((None, 1, d), lambda b, s: (4 * l + 1, 0, 0))],
        out_specs=xrow,
        out_shape=jax.ShapeDtypeStruct((m, d), F32),
        compiler_params=_cparams(),
        name="odd_post",
    )(xf, y, g, bonus, yd, gn_g, gn_b, ones_bd, w_out, ln_g, ln_b)


def _kv_kernel(mem_ref, wkv_ref, kv_ref):
    kv_ref[...] = _dot(mem_ref[...], wkv_ref[...]).astype(BF16)


def _kv_proj(mem, w_kv):
    nb, mlen, d = mem.shape
    nl, _, n2 = w_kv.shape
    return pl.pallas_call(
        _kv_kernel,
        grid=(nl, nb),
        in_specs=[pl.BlockSpec((None, mlen, d), lambda l, b: (b, 0, 0)),
                  pl.BlockSpec((None, d, n2), lambda l, b: (l, 0, 0))],
        out_specs=pl.BlockSpec((None, None, mlen, n2), lambda l, b: (l, b, 0, 0)),
        out_shape=jax.ShapeDtypeStruct((nl, nb, mlen, n2), BF16),
        compiler_params=_cparams(),
        name="xattn_kv",
    )(mem, w_kv)


def _xattn_kernel(x_ref, wq_ref, kv_ref, wo_ref, g_ref, b_ref, o_ref):
    d = x_ref.shape[-1]
    hd = d // XATTN_HEADS
    sub = SUB_ROWS
    nsub = x_ref.shape[0] // sub
    tiles = [slice(t * sub, (t + 1) * sub) for t in range(nsub)]
    heads = XATTN_HEADS
    n_units = nsub * heads
    q, p, proj, outs = {}, {}, {}, [[] for _ in tiles]
    q[0] = _dot(x_ref[tiles[0], :], wq_ref[...])
    for i in range(n_units + XATTN_SKEW + 1):
        if i < n_units:
            t, h = divmod(i, heads)
            if h == 1 and t + 1 < nsub:
                q[t + 1] = _dot(x_ref[tiles[t + 1], :], wq_ref[...])
            qh = q[t][:, h * hd:(h + 1) * hd].astype(BF16)
            sc = lax.dot_general(qh, kv_ref[:, h * hd:(h + 1) * hd], _NT, preferred_element_type=F32)
            sc = sc * (hd ** -0.5)
            sc = sc - jnp.max(sc, axis=-1, keepdims=True)
            e = jnp.exp(sc)
            p[i] = (e / jnp.sum(e, axis=-1, keepdims=True)).astype(BF16)
        j = i - XATTN_SKEW
        if 0 <= j < n_units:
            t, h = divmod(j, heads)
            outs[t].append(jnp.dot(p.pop(j), kv_ref[:, d + h * hd:d + (h + 1) * hd], preferred_element_type=F32))
        j = i - XATTN_SKEW - 1
        if 0 <= j < n_units and j % heads == heads - 1:
            t = j // heads
            proj[t] = _dot(jnp.concatenate(outs[t], axis=-1), wo_ref[...])
    for t, rows in enumerate(tiles):
        o_ref[rows, :] = _layer_norm(DEEPNORM_ALPHA * x_ref[rows, :] + proj[t], g_ref[...], b_ref[...])


def _xattn_layer(xf, w_q, kv, w_o, ln_g, ln_b, l, nb, ns):
    m, d = xf.shape
    mlen = kv.shape[2]
    xrow = pl.BlockSpec((XATTN_TILE, d), lambda b, s: (b * ns + s, 0))
    return pl.pallas_call(
        _xattn_kernel,
        grid=(nb, ns),
        in_specs=[xrow,
                  pl.BlockSpec((None, d, d), lambda b, s: (l, 0, 0)),
                  pl.BlockSpec((None, None, mlen, 2 * d), lambda b, s: (l, b, 0, 0)),
                  pl.BlockSpec((None, d, d), lambda b, s: (l, 0, 0)),
                  pl.BlockSpec((None, 1, d), lambda b, s: (4 * l + 2, 0, 0)),
                  pl.BlockSpec((None, 1, d), lambda b, s: (4 * l + 2, 0, 0))],
        out_specs=xrow,
        out_shape=jax.ShapeDtypeStruct((m, d), F32),
        compiler_params=_cparams(),
        name="xattn",
    )(xf, w_q, kv, w_o, ln_g, ln_b)


def _block_diag(w):
    nblk, bi, bj = w.shape[-3:]
    eye = jnp.eye(nblk, dtype=w.dtype)
    out = w[..., :, :, None, :] * eye[:, None, :, None]
    return out.reshape(w.shape[:-3] + (nblk * bi, nblk * bj))


def kernel(x, mem, ffn1_w_in, ffn1_w_out, ffn2_w_in, ffn2_w_out, ln_g, ln_b, xattn_w_q, xattn_w_kv, xattn_w_o, even_w_in, even_w_out, pool_w, pool_scale, sgu_ln_g, sgu_ln_b, sgu_w, sgu_b, odd_w_in, odd_w_out, rwkv_mu, rwkv_w0, rwkv_w_up, rwkv_a0, rwkv_a_up, rwkv_g_up, rwkv_k_k, rwkv_k_a, rwkv_r_k, rwkv_gn_g, rwkv_gn_b, lru_conv_w, lru_conv_b, lru_w_a, lru_b_a, lru_w_x, lru_b_x, lru_lambda):
    nb, seq, d = x.shape
    assert seq % WIDE_TILE == 0 and seq % XATTN_TILE == 0 and WIDE_TILE % ROW_TILE == 0 and ROW_TILE % SUB_ROWS == 0
    assert SUB_ROWS % SGU_CHUNK == 0 and ROW_TILE % (RWKV_CHUNK * RWKV_INTERLEAVE) == 0
    ns = seq // ROW_TILE
    nsw = seq // WIDE_TILE
    depth = ffn1_w_in.shape[0]
    xf = x.reshape(nb * seq, d)

    bf = lambda t: t.astype(BF16)
    row3 = lambda t: t.reshape(t.shape[0], 1, -1)
    lg = ln_g.reshape(-1, 1, d)
    lb = ln_b.reshape(-1, 1, d)
    f1_in, f1_out, f2_in, f2_out = bf(ffn1_w_in), bf(ffn1_w_out), bf(ffn2_w_in), bf(ffn2_w_out)
    wq, wo = bf(xattn_w_q), bf(xattn_w_o)
    kv = _kv_proj(mem, bf(xattn_w_kv))
    ev_in, ev_out = bf(even_w_in), bf(even_w_out)
    od_in, od_out = bf(odd_w_in), bf(odd_w_out)
    n_e, n_pool, pool_dim = pool_w.shape[:3]
    pw = bf(_block_diag(pool_w.reshape(n_e, n_pool // POOL_PACK, POOL_PACK, pool_dim, pool_dim)))
    sgu_bt = jnp.swapaxes(sgu_b, 1, 2)
    wa_bd, wx_bd = bf(_block_diag(lru_w_a)), bf(_block_diag(lru_w_x))
    w_rwkv = rwkv_w0.shape[-1]
    ones_bd = _block_diag(jnp.ones((w_rwkv // RWKV_HEAD_DIM, RWKV_HEAD_DIM, RWKV_HEAD_DIM), BF16))
    r_k = rwkv_r_k.reshape(rwkv_r_k.shape[0], 1, -1)

    for l in range(depth):
        xf = _ffn_layer(xf, f1_in, f1_out, lg, lb, l, 0)
        e = l // 2
        if l % 2 == 0:
            xf = _even_layer(xf, ev_in, ev_out, pw, row3(pool_scale), row3(sgu_ln_g), row3(sgu_ln_b),
                             sgu_w, sgu_bt, lg, lb, e, l, nb, nsw)
        else:
            r, lw, k, v, a, b, g, bonus, yd = _odd_pre(
                xf, od_in, row3(rwkv_mu), row3(rwkv_w0), bf(rwkv_w_up), row3(rwkv_a0), bf(rwkv_a_up),
                bf(rwkv_g_up), row3(rwkv_k_k), row3(rwkv_k_a), r_k, lru_conv_w, row3(lru_conv_b),
                wa_bd, row3(lru_b_a), wx_bd, row3(lru_b_x), row3(lru_lambda), ones_bd, e, nb, ns)
            y = _rwkv_scan(r, lw, k, v, a, b, nb, ns)
            xf = _odd_post(xf, y, g, bonus, yd, row3(rwkv_gn_g), row3(rwkv_gn_b), ones_bd, od_out,
                           lg, lb, e, l, nb, nsw)
        xf = _xattn_layer(xf, wq, kv, wo, lg, lb, l, nb, seq // XATTN_TILE)
        xf = _ffn_layer(xf, f2_in, f2_out, lg, lb, l, 3)
    return xf.reshape(nb, seq, d)
```

```python
import functools

import jax
import jax.numpy as jnp
from jax import lax
from jax.experimental import pallas as pl
from jax.experimental.pallas import tpu as pltpu

F32 = jnp.float32
BF16 = jnp.bfloat16

DEPTH = 4
LN_EPS = 1e-5
DEEPNORM_ALPHA = (2 * DEPTH) ** 0.25
MACARON_WEIGHT = 0.5
POOL_WINDOWS = (2, 4, 8, 16)
POOL_HALO = 16
POOL_PACK = 2
SGU_CHUNK = 128
XATTN_HEADS = 4
XATTN_SKEW = 2
RWKV_HEAD_DIM = 64
RWKV_CHUNK = 64
RWKV_GN_EPS = 64e-5
MXU_WIDTH = 256
RWKV_INTERLEAVE = 2
RWKV_LAG = 4
LRU_C = 8.0
CONV_WIDTH = 4
SUBLANES = 8
CONV_HALO = SUBLANES

ROW_TILE = 512
FFN_TILE = 1024
WIDE_TILE = 1024
XATTN_TILE = 1024
SUB_ROWS = 256
FFN_CHUNK = 256
VMEM_LIMIT = 56 * 1024 * 1024


def _cparams(grid_rank=2):
    return pltpu.CompilerParams(dimension_semantics=("arbitrary",) * grid_rank,
                                vmem_limit_bytes=VMEM_LIMIT)


def _dot(a, b):
    return jnp.dot(a.astype(BF16), b.astype(BF16), preferred_element_type=F32)


_NN = (((1,), (0,)), ((), ()))
_NT = (((1,), (1,)), ((), ()))
_TN = (((0,), (0,)), ((), ()))


def _mm1(a, b, dn=_NN):
    return lax.dot_general(a.astype(BF16), b.astype(BF16), dn, preferred_element_type=F32)


def _head_sum(x, ones_bd, pieces):
    total = None
    rest = x
    for i in range(pieces):
        p = rest.astype(BF16)
        if i + 1 < pieces:
            rest = rest - p.astype(F32)
        term = jnp.dot(p, ones_bd, preferred_element_type=F32)
        total = term if total is None else total + term
    return total


def _layer_norm(y, g, b, eps=LN_EPS):
    mu = jnp.mean(y, axis=-1, keepdims=True)
    yc = y - mu
    var = jnp.mean(yc * yc, axis=-1, keepdims=True)
    return yc * lax.rsqrt(var + eps) * g + b


def _softplus(t):
    return jnp.maximum(t, 0.0) + jnp.log(1.0 + jnp.exp(-jnp.abs(t)))


def _row_ids(rows, cols):
    return lax.broadcasted_iota(jnp.int32, (rows, cols), 0)


def _ffn_kernel(x_ref, win_ref, wout_ref, g_ref, b_ref, o_ref, h_ref, *, d_ff):
    sub = SUB_ROWS
    nsub = x_ref.shape[0] // sub
    accs = []
    for t in range(nsub):
        rows = slice(t * sub, (t + 1) * sub)
        xb = x_ref[rows, :].astype(BF16)
        for c in range(d_ff // FFN_CHUNK):
            lo = c * FFN_CHUNK
            gate = jnp.dot(xb, win_ref[:, lo:lo + FFN_CHUNK], preferred_element_type=F32)
            up = jnp.dot(xb, win_ref[:, d_ff + lo:d_ff + lo + FFN_CHUNK], preferred_element_type=F32)
            h_ref[rows, lo:lo + FFN_CHUNK] = (gate * jax.nn.sigmoid(gate) * up).astype(BF16)
        accs.append(jnp.dot(h_ref[rows, :], wout_ref[...], preferred_element_type=F32))
    for t in range(nsub):
        rows = slice(t * sub, (t + 1) * sub)
        y = DEEPNORM_ALPHA * x_ref[rows, :] + MACARON_WEIGHT * accs[t]
        o_ref[rows, :] = _layer_norm(y, g_ref[...], b_ref[...])


def _ffn_layer(xf, w_in, w_out, ln_g, ln_b, l, j):
    m, d = xf.shape
    d_ff = w_out.shape[1]
    assert m % FFN_TILE == 0
    once = pl.Buffered(1)
    return pl.pallas_call(
        functools.partial(_ffn_kernel, d_ff=d_ff),
        grid=(m // FFN_TILE,),
        in_specs=[
            pl.BlockSpec((FFN_TILE, d), lambda i: (i, 0)),
            pl.BlockSpec((None, d, 2 * d_ff), lambda i: (l, 0, 0), pipeline_mode=once),
            pl.BlockSpec((None, d_ff, d), lambda i: (l, 0, 0), pipeline_mode=once),
            pl.BlockSpec((None, 1, d), lambda i: (4 * l + j, 0, 0)),
            pl.BlockSpec((None, 1, d), lambda i: (4 * l + j, 0, 0)),
        ],
        out_specs=pl.BlockSpec((FFN_TILE, d), lambda i: (i, 0)),
        out_shape=jax.ShapeDtypeStruct((m, d), F32),
        scratch_shapes=[pltpu.VMEM((FFN_TILE, d_ff), BF16)],
        compiler_params=_cparams(1),
        name="ffn",
    )(xf, w_in, w_out, ln_g, ln_b)


def _even_kernel(x_ref, win_ref, wout_ref, poolw_ref, pscale_ref, sg_ref, sb_ref, sw_ref, sbt_ref,
                 g_ref, b_ref, o_ref, ext_ref):
    s = pl.program_id(1)
    tm = x_ref.shape[0]
    width = pscale_ref.shape[-1]
    pw = width // len(POOL_WINDOWS)
    heads = sw_ref.shape[0]
    hw = width // heads
    sub = SUB_ROWS
    nchunk = sub // SGU_CHUNK
    subs = range(tm // sub)
    tiles = [slice(t * sub, (t + 1) * sub) for t in subs]

    @pl.when(s == 0)
    def _():
        ext_ref[0:POOL_HALO, :] = jnp.zeros((POOL_HALO, width), F32)

    h = [_dot(x_ref[rows, :], win_ref[...]) for rows in tiles]
    for t in subs:
        ext_ref[POOL_HALO + t * sub:POOL_HALO + (t + 1) * sub, :] = h[t][:, :width]

    ti = lax.broadcasted_iota(jnp.int32, (SGU_CHUNK, SGU_CHUNK), 0)
    si = lax.broadcasted_iota(jnp.int32, (SGU_CHUNK, SGU_CHUNK), 1)
    w_sgu = [jnp.where(ti >= si, sw_ref[hd], 0.0).astype(BF16) for hd in range(heads)]

    ya, yb = [], []
    for t in subs:
        ext = ext_ref[t * sub:(t + 1) * sub + POOL_HALO, :]
        pos = (s * tm + t * sub + 1 + _row_ids(sub, 1)).astype(F32)
        pooled = []
        for g, win in enumerate(POOL_WINDOWS):
            acc = ext[:, g * pw:(g + 1) * pw]
            d = 1
            while d < win:
                acc = acc + pltpu.roll(acc, d, axis=0)
                d *= 2
            mean = acc[POOL_HALO:, :] / jnp.minimum(pos, float(win))
            pooled.append(mean - h[t][:, g * pw:(g + 1) * pw])
        yat = [_dot(jnp.concatenate(pooled[p * POOL_PACK:(p + 1) * POOL_PACK], axis=-1), poolw_ref[p])
               for p in range(len(POOL_WINDOWS) // POOL_PACK)]
        ya.append(jnp.concatenate(yat, axis=-1) * pscale_ref[...])

        u = jax.nn.gelu(h[t][:, width:2 * width])
        v = jax.nn.gelu(h[t][:, 2 * width:3 * width])
        vb = _layer_norm(v, sg_ref[...], sb_ref[...]).astype(BF16)
        mixed = []
        for hd in range(heads):
            bias = sbt_ref[:, hd:hd + 1]
            vh = jnp.concatenate([vb[c * SGU_CHUNK:(c + 1) * SGU_CHUNK, hd * hw:(hd + 1) * hw]
                                  for c in range(nchunk)], axis=1)
            mixed.append(jnp.dot(w_sgu[hd], vh, preferred_element_type=F32) + bias)
        yb.append(u * jnp.concatenate(
            [jnp.concatenate([mixed[hd][:, c * hw:(c + 1) * hw] for hd in range(heads)], axis=1)
             for c in range(nchunk)], axis=0))

    out = [_dot(ya[t], wout_ref[0:width, :]) + _dot(yb[t], wout_ref[width:2 * width, :]) for t in subs]
    for t, rows in enumerate(tiles):
        o_ref[rows, :] = _layer_norm(DEEPNORM_ALPHA * x_ref[rows, :] + out[t], g_ref[...], b_ref[...])
    ext_ref[0:POOL_HALO, :] = ext_ref[tm:tm + POOL_HALO, :]


def _even_layer(xf, w_in, w_out, pool_w, pool_scale, sgu_g, sgu_b, sgu_w, sgu_bt, ln_g, ln_b, e, l, nb, ns):
    m, d = xf.shape
    n_in = w_in.shape[-1]
    width = pool_scale.shape[-1]
    full = lambda shape: pl.BlockSpec((None,) + shape, lambda b, s: (e,) + (0,) * len(shape))
    return pl.pallas_call(
        _even_kernel,
        grid=(nb, ns),
        in_specs=[
            pl.BlockSpec((WIDE_TILE, d), lambda b, s: (b * ns + s, 0)),
            full((d, n_in)),
            full((2 * width, d)),
            full(pool_w.shape[1:]),
            full((1, width)),
            full((1, width)),
            full((1, width)),
            full(sgu_w.shape[1:]),
            full(sgu_bt.shape[1:]),
            pl.BlockSpec((None, 1, d), lambda b, s: (4 * l + 1, 0, 0)),
            pl.BlockSpec((None, 1, d), lambda b, s: (4 * l + 1, 0, 0)),
        ],
        out_specs=pl.BlockSpec((WIDE_TILE, d), lambda b, s: (b * ns + s, 0)),
        out_shape=jax.ShapeDtypeStruct((m, d), F32),
        scratch_shapes=[pltpu.VMEM((POOL_HALO + WIDE_TILE, width), F32)],
        compiler_params=_cparams(),
        name="even_mixer",
    )(xf, w_in, w_out, pool_w, pool_scale, sgu_g, sgu_b, sgu_w, sgu_bt, ln_g, ln_b)


def _odd_pre_kernel(x_ref, win_ref, mu_ref, w0_ref, wup_ref, a0_ref, aup_ref, gup_ref, kk_ref, ka_ref,
                    rk_ref, cw_ref, cb_ref, wa_ref, ba_ref, wx_ref, bx_ref, lam_ref, ones_ref,
                    r_o, lw_o, k_o, v_o, a_o, b_o, g_o, bonus_o, yd_o,
                    hprev_ref, xr_ref, hlru_ref):
    s = pl.program_id(1)
    tm = x_ref.shape[0]
    w = w0_ref.shape[-1]
    n_c = mu_ref.shape[-1]
    sub = SUB_ROWS
    nsub = tm // sub
    tiles = [slice(t * sub, (t + 1) * sub) for t in range(nsub)]
    dr = wup_ref.shape[0]
    ar = aup_ref.shape[0]
    ones_bd = ones_ref[...]

    @pl.when(s == 0)
    def _():
        hprev_ref[...] = jnp.zeros(hprev_ref.shape, F32)
        xr_ref[0:CONV_HALO, :] = jnp.zeros((CONV_HALO, w), F32)
        hlru_ref[...] = jnp.zeros(hlru_ref.shape, F32)

    first = _row_ids(sub, 1) == 0
    ngrp = sub // SUBLANES
    sub_id = lax.broadcasted_iota(jnp.int32, (ngrp, SUBLANES, w), 1)

    def project_rwkv(t):
        return _dot(x_ref[tiles[t], :], win_ref[:, :n_c])

    def project_lru(t):
        hd = _dot(x_ref[tiles[t], :], win_ref[:, n_c:])
        xr_ref[CONV_HALO + t * sub:CONV_HALO + (t + 1) * sub, :] = hd[:, w:]
        return hd[:, :w], xr_ref[t * sub:(t + 1) * sub + CONV_HALO, :]

    def rwkv_part(t, hc, hprev):
        rows = tiles[t]
        prev = jnp.where(first, hprev, pltpu.roll(hc, 1, axis=0))
        hc = hc + mu_ref[...] * (prev - hc)
        r = hc[:, 0:w]
        k = hc[:, w:2 * w]
        v = hc[:, 2 * w:3 * w]
        rest = hc[:, 3 * w:]
        wd = rest[:, 0:dr]
        ad = rest[:, dr:dr + ar]
        gd = rest[:, dr + ar:]

        z = w0_ref[...] + _dot(jnp.tanh(wd), wup_ref[...])
        logw = -_softplus(-z) - 0.5
        lw_o[rows, :] = -jnp.exp(logw)
        asig = jax.nn.sigmoid(a0_ref[...] + _dot(ad, aup_ref[...]))
        g_o[rows, :] = _dot(jax.nn.sigmoid(gd), gup_ref[...])
        kk = k * kk_ref[...]
        kk = kk * lax.rsqrt(jnp.maximum(_head_sum(kk * kk, ones_bd, 1), 1e-24))
        k2 = k * (1.0 + (asig - 1.0) * ka_ref[...])
        r_o[rows, :] = r
        k_o[rows, :] = k2
        v_o[rows, :] = v
        a_o[rows, :] = -kk
        b_o[rows, :] = kk * asig
        bonus_o[rows, :] = _head_sum(r * k2 * rk_ref[...], ones_bd, 2) * v

    def lru_part(t, gate, ext, hcar):
        xc = cw_ref[CONV_WIDTH - 1:CONV_WIDTH, :] * ext[CONV_HALO:, :] + cb_ref[...]
        for j in range(1, CONV_WIDTH):
            xc = xc + cw_ref[CONV_WIDTH - 1 - j:CONV_WIDTH - j, :] * pltpu.roll(ext, j, axis=0)[CONV_HALO:, :]
        rec = jax.nn.sigmoid(_dot(xc, wa_ref[...]) + ba_ref[...])
        inp = jax.nn.sigmoid(_dot(xc, wx_ref[...]) + bx_ref[...])
        log_a = -LRU_C * rec * _softplus(-lam_ref[...])
        a = jnp.exp(log_a)
        bx = jnp.sqrt(-jnp.tanh(log_a) * (a * a + 1.0)) * (inp * xc)
        a3 = a.reshape(ngrp, SUBLANES, w)
        b3 = bx.reshape(ngrp, SUBLANES, w)
        d = 1
        while d < SUBLANES:
            keep = sub_id >= d
            a_sh = jnp.where(keep, pltpu.roll(a3, d, axis=1), 1.0)
            b_sh = jnp.where(keep, pltpu.roll(b3, d, axis=1), 0.0)
            b3 = a3 * b_sh + b3
            a3 = a3 * a_sh
            d *= 2
        gate_act = jax.nn.gelu(gate)
        for j in range(ngrp):
            hj = a3[j] * hcar + b3[j]
            lo = t * sub + j * SUBLANES
            yd_o[lo:lo + SUBLANES, :] = hj * gate_act[j * SUBLANES:(j + 1) * SUBLANES, :]
            hcar = hj[SUBLANES - 1:SUBLANES, :]
        return hcar

    hprev = hprev_ref[...]
    hcar = hlru_ref[...]
    hc = project_rwkv(0)
    gate, ext = project_lru(0)
    for t in range(nsub):
        last = t + 1 == nsub
        hc_next = None if last else project_rwkv(t + 1)
        rwkv_part(t, hc, hprev)
        hprev = hc[sub - 1:sub, :]
        lru_next = (None, None) if last else project_lru(t + 1)
        hcar = lru_part(t, gate, ext, hcar)
        hc, (gate, ext) = hc_next, lru_next
    hprev_ref[...] = hprev
    hlru_ref[...] = hcar
    xr_ref[0:CONV_HALO, :] = xr_ref[tm:tm + CONV_HALO, :]


def _odd_pre(xf, w_in, mu, w0, w_up, a0, a_up, g_up, k_k, k_a, r_k, conv_w, conv_b, wa_bd, b_a, wx_bd, b_x,
             lam, ones_bd, e, nb, ns):
    m, d = xf.shape
    w = w0.shape[-1]
    full = lambda arr: pl.BlockSpec((None,) + arr.shape[1:], lambda b, s: (e,) + (0,) * (arr.ndim - 1))
    row = pl.BlockSpec((ROW_TILE, w), lambda b, s: (b * ns + s, 0))
    params = (w_in, mu, w0, w_up, a0, a_up, g_up, k_k, k_a, r_k, conv_w, conv_b, wa_bd, b_a, wx_bd, b_x, lam)
    return pl.pallas_call(
        _odd_pre_kernel,
        grid=(nb, ns),
        in_specs=[pl.BlockSpec((ROW_TILE, d), lambda b, s: (b * ns + s, 0))]
        + [full(p) for p in params]
        + [pl.BlockSpec(ones_bd.shape, lambda b, s: (0, 0))],
        out_specs=[row] * 9,
        out_shape=[jax.ShapeDtypeStruct((m, w), F32)] * 9,
        scratch_shapes=[pltpu.VMEM((1, mu.shape[-1]), F32),
                        pltpu.VMEM((CONV_HALO + ROW_TILE, w), F32),
                        pltpu.VMEM((1, w), F32)],
        compiler_params=_cparams(),
        name="odd_pre",
    )(xf, *params, ones_bd)


def _rwkv_kernel(r_ref, lw_ref, k_ref, v_ref, a_ref, b_ref, y_ref, t_ref, q_ref, g_ref, h_ref):
    s = pl.program_id(1)
    tm, w = r_ref.shape
    n = RWKV_HEAD_DIM
    lc = RWKV_CHUNK
    gw = MXU_WIDTH
    hpg = gw // n
    groups = range(w // gw)

    @pl.when(s == 0)
    def _():
        t_ref[...] = jnp.zeros(t_ref.shape, F32)

    row_t = lax.broadcasted_iota(jnp.int32, (lc, gw), 0)
    col = lax.broadcasted_iota(jnp.int32, (lc, gw), 1)
    col_s = col & (n - 1)
    col_h = col >> (n.bit_length() - 1)
    strict = row_t > col_s
    incl = row_t >= col_s
    eye_m = row_t == col_s
    eye = eye_m.astype(F32)
    merge_masks = [(row_t >> 1) == (col_s >> 1)]
    size = 2
    while size < lc:
        shift = size.bit_length()
        merge_masks.append(((row_t >> shift) == (col_s >> shift)) & ((row_t & size) != 0) & ((col_s & size) == 0))
        size *= 2
    blk_r = lax.broadcasted_iota(jnp.int32, (gw, gw), 0) >> (n.bit_length() - 1)
    blk_c = lax.broadcasted_iota(jnp.int32, (gw, gw), 1) >> (n.bit_length() - 1)
    bd_mask = (blk_r == blk_c).astype(BF16)
    rows = _row_ids(lc, w)

    def bd(x):
        xb = x.astype(BF16)
        return jnp.concatenate([xb] * hpg, axis=0) * bd_mask

    def mm(a, b_bf16, dn=_NN):
        return lax.dot_general(a.astype(BF16), b_bf16, dn, preferred_element_type=F32)

    def diag_blocks(p):
        out = p[(hpg - 1) * n:, :]
        for h in range(hpg - 1):
            out = jnp.where(col_h == h, p[h * n:(h + 1) * n, :], out)
        return out

    def state_free(i):
        at, rt, bt, kt, bh, kh, v, pend, where = [], [], [], [], [], [], [], [], []
        for ci in range(RWKV_INTERLEAVE):
            sl = pl.ds((i * RWKV_INTERLEAVE + ci) * lc, lc)
            lw = lw_ref[sl, :]
            cum = lw
            d = 1
            while d < lc:
                cum = cum + jnp.where(rows >= d, pltpu.roll(cum, d, axis=0), 0.0)
                d *= 2
            cl = cum[lc - 1:lc, :]
            einv = jnp.exp(-cum)
            eend = jnp.exp(cl - cum)
            a = a_ref[sl, :]
            b = b_ref[sl, :]
            k = k_ref[sl, :]
            full = (a * jnp.exp(cum - lw), r_ref[sl, :] * jnp.exp(cum), b * einv, k * einv, b * eend, k * eend,
                    v_ref[sl, :], jnp.exp(cl))
            for g in groups:
                for dst, t in zip((at, rt, bt, kt, bh, kh, v, pend), full):
                    dst.append(t[:, g * gw:(g + 1) * gw])
                where.append((sl, slice(g * gw, (g + 1) * gw)))
            yield
        units = range(len(where))
        lhs = [jnp.concatenate([at[u], rt[u]], axis=0).astype(BF16) for u in units]
        sb = [mm(lhs[u], bd(bt[u]), _NT) for u in units]
        yield
        sk = [mm(lhs[u], bd(kt[u]), _NT) for u in units]
        a_ab = [jnp.where(strict, t[:lc], 0.0) for t in sb]
        m_rb = [jnp.where(incl, t[lc:], 0.0) for t in sb]
        yield
        a_ak = [jnp.where(strict, t[:lc], 0.0) for t in sk]
        m_rk = [jnp.where(incl, t[lc:], 0.0) for t in sk]
        xm = [mm(jnp.concatenate([a_ak[u], m_rk[u]], axis=0), bd(v[u])) for u in units]
        x0 = [t[:lc] for t in xm]
        ykv = [t[lc:] for t in xm]
        yield
        inv = [eye + jnp.where(merge_masks[0], t, 0.0) for t in a_ab]
        for below in merge_masks[1:]:
            we = [mm(inv[u], bd(jnp.where(below, a_ab[u], 0.0))) for u in units]
            yield
            inv = [inv[u] + mm(we[u], bd(inv[u])) for u in units]
            yield
        ap = [mm(inv[u], bd(at[u])) for u in units]
        yield
        vp = [mm(inv[u], bd(x0[u])) for u in units]
        yield
        q = [rt[u] + mm(m_rb[u], bd(ap[u])) for u in units]
        for u in units:
            q_ref[where[u]] = q[u]
        yield
        yl = [mm(m_rb[u], bd(vp[u])) + ykv[u] for u in units]
        for u in units:
            y_ref[where[u]] = yl[u]
        yield
        gm = [diag_blocks(mm(bh[u], ap[u].astype(BF16), _TN))
              + jnp.where(eye_m, jnp.broadcast_to(pend[u], (lc, gw)), 0.0) for u in units]
        for u in units:
            g_ref[where[u]] = gm[u]
        yield
        hm = [diag_blocks(mm(jnp.concatenate([bh[u], kh[u]], axis=0),
                             jnp.concatenate([vp[u], v[u]], axis=0).astype(BF16), _TN)) for u in units]
        for u in units:
            h_ref[where[u]] = hm[u]

    def state_step(c):
        sl = pl.ds(c * lc, lc)
        fin = [mm(jnp.concatenate([q_ref[sl, g * gw:(g + 1) * gw], g_ref[sl, g * gw:(g + 1) * gw]], axis=0),
                  bd(t_ref[:, g * gw:(g + 1) * gw])) for g in groups]
        for g in groups:
            gs = slice(g * gw, (g + 1) * gw)
            y_ref[sl, gs] = y_ref[sl, gs] + fin[g][:lc]
            t_ref[:, gs] = fin[g][lc:] + h_ref[sl, gs]

    ngroups = tm // (lc * RWKV_INTERLEAVE)
    running, ready_chunks, slot, started = [], [], 0, 0
    while started < ngroups or running or ready_chunks:
        if started < ngroups and slot == started * RWKV_LAG:
            running.append((state_free(started), started))
            started += 1
        still = []
        for gen, gi in running:
            try:
                next(gen)
                still.append((gen, gi))
            except StopIteration:
                ready_chunks.extend(range(gi * RWKV_INTERLEAVE, (gi + 1) * RWKV_INTERLEAVE))
        running = still
        if ready_chunks:
            state_step(ready_chunks.pop(0))
        slot += 1


def _rwkv_scan(r, lw, k, v, a, b, nb, ns):
    m, w = r.shape
    assert RWKV_CHUNK == RWKV_HEAD_DIM and w % MXU_WIDTH == 0 and MXU_WIDTH % RWKV_HEAD_DIM == 0
    assert RWKV_HEAD_DIM & (RWKV_HEAD_DIM - 1) == 0
    row = pl.BlockSpec((ROW_TILE, w), lambda bb, s: (bb * ns + s, 0))
    return pl.pallas_call(
        _rwkv_kernel,
        grid=(nb, ns),
        in_specs=[row] * 6,
        out_specs=row,
        out_shape=jax.ShapeDtypeStruct((m, w), F32),
        scratch_shapes=[pltpu.VMEM((RWKV_HEAD_DIM, w), F32)]
        + [pltpu.VMEM((ROW_TILE, w), F32)] * 3,
        compiler_params=_cparams(),
        name="rwkv7",
    )(r, lw, k, v, a, b)


def _odd_post_kernel(x_ref, y_ref, g_ref, bonus_ref, yd_ref, gng_ref, gnb_ref, ones_ref, wout_ref,
                     lg_ref, lb_ref, o_ref):
    w = y_ref.shape[-1]
    sub = SUB_ROWS
    tiles = [slice(t * sub, (t + 1) * sub) for t in range(x_ref.shape[0] // sub)]
    ones_bd = ones_ref[...]
    inv_n = 1.0 / RWKV_HEAD_DIM
    y = [y_ref[rows, :] for rows in tiles]
    mu = [_head_sum(t, ones_bd, 2) * inv_n for t in y]
    yc = [a - b for a, b in zip(y, mu)]
    var = [_head_sum(t * t, ones_bd, 1) * inv_n for t in yc]
    out = []
    for t, rows in enumerate(tiles):
        yn = yc[t] * lax.rsqrt(var[t] + RWKV_GN_EPS) * gng_ref[...] + gnb_ref[...]
        y_rwkv = (yn + bonus_ref[rows, :]) * g_ref[rows, :]
        out.append(_dot(y_rwkv, wout_ref[0:w, :]) + _dot(yd_ref[rows, :], wout_ref[w:2 * w, :]))
    for t, rows in enumerate(tiles):
        o_ref[rows, :] = _layer_norm(DEEPNORM_ALPHA * x_ref[rows, :] + out[t], lg_ref[...], lb_ref[...])


def _odd_post(xf, y, g, bonus, yd, gn_g, gn_b, ones_bd, w_out, ln_g, ln_b, e, l, nb, ns):
    m, d = xf.shape
    w = y.shape[-1]
    row = pl.BlockSpec((WIDE_TILE, w), lambda b, s: (b * ns + s, 0))
    xrow = pl.BlockSpec((WIDE_TILE, d), lambda b, s: (b * ns + s, 0))
    return pl.pallas_call(
        _odd_post_kernel,
        grid=(nb, ns),
        in_specs=[xrow, row, row, row, row,
                  pl.BlockSpec((None, 1, w), lambda b, s: (e, 0, 0)),
                  pl.BlockSpec((None, 1, w), lambda b, s: (e, 0, 0)),
                  pl.BlockSpec(ones_bd.shape, lambda b, s: (0, 0)),
                  pl.BlockSpec((None, 2 * w, d), lambda b, s: (e, 0, 0)),
                  pl.BlockSpec((None, 1, d), lambda b, s: (4 * l + 1, 0, 0)),
                  pl.BlockSpec((None, 1, d), lambda b, s: (4 * l + 1, 0, 0))],
        out_specs=xrow,
        out_shape=jax.ShapeDtypeStruct((m, d), F32),
        compiler_params=_cparams(),
        name="odd_post",
    )(xf, y, g, bonus, yd, gn_g, gn_b, ones_bd, w_out, ln_g, ln_b)


def _kv_kernel(mem_ref, wkv_ref, kv_ref):
    kv_ref[...] = _dot(mem_ref[...], wkv_ref[...]).astype(BF16)


def _kv_proj(mem, w_kv):
    nb, mlen, d = mem.shape
    nl, _, n2 = w_kv.shape
    return pl.pallas_call(
        _kv_kernel,
        grid=(nl, nb),
        in_specs=[pl.BlockSpec((None, mlen, d), lambda l, b: (b, 0, 0)),
                  pl.BlockSpec((None, d, n2), lambda l, b: (l, 0, 0))],
        out_specs=pl.BlockSpec((None, None, mlen, n2), lambda l, b: (l, b, 0, 0)),
        out_shape=jax.ShapeDtypeStruct((nl, nb, mlen, n2), BF16),
        compiler_params=_cparams(),
        name="xattn_kv",
    )(mem, w_kv)


def _xattn_kernel(x_ref, wq_ref, kv_ref, wo_ref, g_ref, b_ref, o_ref):
    d = x_ref.shape[-1]
    hd = d // XATTN_HEADS
    sub = SUB_ROWS
    nsub = x_ref.shape[0] // sub
    tiles = [slice(t * sub, (t + 1) * sub) for t in range(nsub)]
    heads = XATTN_HEADS
    n_units = nsub * heads
    q, p, proj, outs = {}, {}, {}, [[] for _ in tiles]
    q[0] = _dot(x_ref[tiles[0], :], wq_ref[...])
    for i in range(n_units + XATTN_SKEW + 1):
        if i < n_units:
            t, h = divmod(i, heads)
            if h == 1 and t + 1 < nsub:
                q[t + 1] = _dot(x_ref[tiles[t + 1], :], wq_ref[...])
            qh = q[t][:, h * hd:(h + 1) * hd].astype(BF16)
            sc = lax.dot_general(qh, kv_ref[:, h * hd:(h + 1) * hd], _NT, preferred_element_type=F32)
            sc = sc * (hd ** -0.5)
            sc = sc - jnp.max(sc, axis=-1, keepdims=True)
            e = jnp.exp(sc)
            p[i] = (e / jnp.sum(e, axis=-1, keepdims=True)).astype(BF16)
        j = i - XATTN_SKEW
        if 0 <= j < n_units:
            t, h = divmod(j, heads)
            outs[t].append(jnp.dot(p.pop(j), kv_ref[:, d + h * hd:d + (h + 1) * hd], preferred_element_type=F32))
        j = i - XATTN_SKEW - 1
        if 0 <= j < n_units and j % heads == heads - 1:
            t = j // heads
            proj[t] = _dot(jnp.concatenate(outs[t], axis=-1), wo_ref[...])
    for t, rows in enumerate(tiles):
        o_ref[rows, :] = _layer_norm(DEEPNORM_ALPHA * x_ref[rows, :] + proj[t], g_ref[...], b_ref[...])


def _xattn_layer(xf, w_q, kv, w_o, ln_g, ln_b, l, nb, ns):
    m, d = xf.shape
    mlen = kv.shape[2]
    xrow = pl.BlockSpec((XATTN_TILE, d), lambda b, s: (b * ns + s, 0))
    return pl.pallas_call(
        _xattn_kernel,
        grid=(nb, ns),
        in_specs=[xrow,
                  pl.BlockSpec((None, d, d), lambda b, s: (l, 0, 0)),
                  pl.BlockSpec((None, None, mlen, 2 * d), lambda b, s: (l, b, 0, 0)),
                  pl.BlockSpec((None, d, d), lambda b, s: (l, 0, 0)),
                  pl.BlockSpec((None, 1, d), lambda b, s: (4 * l + 2, 0, 0)),
                  pl.BlockSpec((None, 1, d), lambda b, s: (4 * l + 2, 0, 0))],
        out_specs=xrow,
        out_shape=jax.ShapeDtypeStruct((m, d), F32),
        compiler_params=_cparams(),
        name="xattn",
    )(xf, w_q, kv, w_o, ln_g, ln_b)


def _block_diag(w):
    nblk, bi, bj = w.shape[-3:]
    eye = jnp.eye(nblk, dtype=w.dtype)
    out = w[..., :, :, None, :] * eye[:, None, :, None]
    return out.reshape(w.shape[:-3] + (nblk * bi, nblk * bj))


def kernel(x, mem, ffn1_w_in, ffn1_w_out, ffn2_w_in, ffn2_w_out, ln_g, ln_b, xattn_w_q, xattn_w_kv, xattn_w_o, even_w_in, even_w_out, pool_w, pool_scale, sgu_ln_g, sgu_ln_b, sgu_w, sgu_b, odd_w_in, odd_w_out, rwkv_mu, rwkv_w0, rwkv_w_up, rwkv_a0, rwkv_a_up, rwkv_g_up, rwkv_k_k, rwkv_k_a, rwkv_r_k, rwkv_gn_g, rwkv_gn_b, lru_conv_w, lru_conv_b, lru_w_a, lru_b_a, lru_w_x, lru_b_x, lru_lambda):
    nb, seq, d = x.shape
    assert seq % WIDE_TILE == 0 and seq % XATTN_TILE == 0 and WIDE_TILE % ROW_TILE == 0 and ROW_TILE % SUB_ROWS == 0
    assert SUB_ROWS % SGU_CHUNK == 0 and ROW_TILE % (RWKV_CHUNK * RWKV_INTERLEAVE) == 0
    ns = seq // ROW_TILE
    nsw = seq // WIDE_TILE
    depth = ffn1_w_in.shape[0]
    xf = x.reshape(nb * seq, d)

    bf = lambda t: t.astype(BF16)
    row3 = lambda t: t.reshape(t.shape[0], 1, -1)
    lg = ln_g.reshape(-1, 1, d)
    lb = ln_b.reshape(-1, 1, d)
    f1_in, f1_out, f2_in, f2_out = bf(ffn1_w_in), bf(ffn1_w_out), bf(ffn2_w_in), bf(ffn2_w_out)
    wq, wo = bf(xattn_w_q), bf(xattn_w_o)
    kv = _kv_proj(mem, bf(xattn_w_kv))
    ev_in, ev_out = bf(even_w_in), bf(even_w_out)
    od_in, od_out = bf(odd_w_in), bf(odd_w_out)
    n_e, n_pool, pool_dim = pool_w.shape[:3]
    pw = bf(_block_diag(pool_w.reshape(n_e, n_pool // POOL_PACK, POOL_PACK, pool_dim, pool_dim)))
    sgu_bt = jnp.swapaxes(sgu_b, 1, 2)
    wa_bd, wx_bd = bf(_block_diag(lru_w_a)), bf(_block_diag(lru_w_x))
    w_rwkv = rwkv_w0.shape[-1]
    ones_bd = _block_diag(jnp.ones((w_rwkv // RWKV_HEAD_DIM, RWKV_HEAD_DIM, RWKV_HEAD_DIM), BF16))
    r_k = rwkv_r_k.reshape(rwkv_r_k.shape[0], 1, -1)

    for l in range(depth):
        xf = _ffn_layer(xf, f1_in, f1_out, lg, lb, l, 0)
        e = l // 2
        if l % 2 == 0:
            xf = _even_layer(xf, ev_in, ev_out, pw, row3(pool_scale), row3(sgu_ln_g), row3(sgu_ln_b),
                             sgu_w, sgu_bt, lg, lb, e, l, nb, nsw)
        else:
            r, lw, k, v, a, b, g, bonus, yd = _odd_pre(
                xf, od_in, row3(rwkv_mu), row3(rwkv_w0), bf(rwkv_w_up), row3(rwkv_a0), bf(rwkv_a_up),
                bf(rwkv_g_up), row3(rwkv_k_k), row3(rwkv_k_a), r_k, lru_conv_w, row3(lru_conv_b),
                wa_bd, row3(lru_b_a), wx_bd, row3(lru_b_x), row3(lru_lambda), ones_bd, e, nb, ns)
            y = _rwkv_scan(r, lw, k, v, a, b, nb, ns)
            xf = _odd_post(xf, y, g, bonus, yd, row3(rwkv_gn_g), row3(rwkv_gn_b), ones_bd, od_out,
                           lg, lb, e, l, nb, nsw)
        xf = _xattn_layer(xf, wq, kv, wo, lg, lb, l, nb, seq // XATTN_TILE)
        xf = _ffn_layer(xf, f2_in, f2_out, lg, lb, l, 3)
    return xf.reshape(nb, seq, d)
```

```python
import functools

import jax
import jax.numpy as jnp
from jax import lax
from jax.experimental import pallas as pl
from jax.experimental.pallas import tpu as pltpu

F32 = jnp.float32
BF16 = jnp.bfloat16

DEPTH = 4
LN_EPS = 1e-5
DEEPNORM_ALPHA = (2 * DEPTH) ** 0.25
MACARON_WEIGHT = 0.5
POOL_WINDOWS = (2, 4, 8, 16)
POOL_HALO = 16
POOL_PACK = 2
SGU_CHUNK = 128
XATTN_HEADS = 4
XATTN_SKEW = 2
RWKV_HEAD_DIM = 64
RWKV_CHUNK = 64
RWKV_GN_EPS = 64e-5
MXU_WIDTH = 256
RWKV_INTERLEAVE = 2
RWKV_LAG = 4
LRU_C = 8.0
CONV_WIDTH = 4
SUBLANES = 8
CONV_HALO = SUBLANES

ROW_TILE = 512
FFN_TILE = 1024
WIDE_TILE = 1024
XATTN_TILE = 1024
SUB_ROWS = 256
FFN_CHUNK = 256
FFN_LOAD_CHUNK = 512
VMEM_LIMIT = 56 * 1024 * 1024


def _cparams(grid_rank=2):
    return pltpu.CompilerParams(dimension_semantics=("arbitrary",) * grid_rank,
                                vmem_limit_bytes=VMEM_LIMIT)


def _dot(a, b):
    return jnp.dot(a.astype(BF16), b.astype(BF16), preferred_element_type=F32)


_NN = (((1,), (0,)), ((), ()))
_NT = (((1,), (1,)), ((), ()))
_TN = (((0,), (0,)), ((), ()))


def _mm1(a, b, dn=_NN):
    return lax.dot_general(a.astype(BF16), b.astype(BF16), dn, preferred_element_type=F32)


def _head_sum(x, ones_bd, pieces):
    total = None
    rest = x
    for i in range(pieces):
        p = rest.astype(BF16)
        if i + 1 < pieces:
            rest = rest - p.astype(F32)
        term = jnp.dot(p, ones_bd, preferred_element_type=F32)
        total = term if total is None else total + term
    return total


def _layer_norm(y, g, b, eps=LN_EPS):
    mu = jnp.mean(y, axis=-1, keepdims=True)
    yc = y - mu
    var = jnp.mean(yc * yc, axis=-1, keepdims=True)
    return yc * lax.rsqrt(var + eps) * g + b


def _softplus(t):
    return jnp.maximum(t, 0.0) + jnp.log(1.0 + jnp.exp(-jnp.abs(t)))


def _row_ids(rows, cols):
    return lax.broadcasted_iota(jnp.int32, (rows, cols), 0)


def _load_cast(src_hbm, dst_ref, stage_ref, sem, axis, chunk):
    n = src_hbm.shape[axis] // chunk

    def piece(c):
        sl = slice(c * chunk, (c + 1) * chunk)
        return (slice(None), sl) if axis == 1 else (sl, slice(None))

    def copy(c):
        return pltpu.make_async_copy(src_hbm.at[piece(c)], stage_ref.at[c % 2], sem.at[c % 2])

    copy(0).start()
    for c in range(n):
        if c + 1 < n:
            copy(c + 1).start()
        copy(c).wait()
        dst_ref[piece(c)] = stage_ref[c % 2].astype(BF16)


def _ffn_kernel(x_ref, win_hbm, wout_hbm, g_ref, b_ref, o_ref, h_ref, win_ref, wout_ref,
                stage_in, stage_out, sem_in, sem_out, *, d_ff, layer):
    @pl.when(pl.program_id(0) == 0)
    def _():
        _load_cast(win_hbm.at[layer], win_ref, stage_in, sem_in, 1, FFN_LOAD_CHUNK)
        _load_cast(wout_hbm.at[layer], wout_ref, stage_out, sem_out, 0, FFN_LOAD_CHUNK // 2)

    sub = SUB_ROWS
    nsub = x_ref.shape[0] // sub
    accs = []
    for t in range(nsub):
        rows = slice(t * sub, (t + 1) * sub)
        xb = x_ref[rows, :].astype(BF16)
        for c in range(d_ff // FFN_CHUNK):
            lo = c * FFN_CHUNK
            gate = jnp.dot(xb, win_ref[:, lo:lo + FFN_CHUNK], preferred_element_type=F32)
            up = jnp.dot(xb, win_ref[:, d_ff + lo:d_ff + lo + FFN_CHUNK], preferred_element_type=F32)
            h_ref[rows, lo:lo + FFN_CHUNK] = (gate * jax.nn.sigmoid(gate) * up).astype(BF16)
        accs.append(jnp.dot(h_ref[rows, :], wout_ref[...], preferred_element_type=F32))
    for t in range(nsub):
        rows = slice(t * sub, (t + 1) * sub)
        y = DEEPNORM_ALPHA * x_ref[rows, :] + MACARON_WEIGHT * accs[t]
        o_ref[rows, :] = _layer_norm(y, g_ref[...], b_ref[...])


def _ffn_layer(xf, w_in, w_out, ln_g, ln_b, l, j):
    m, d = xf.shape
    d_ff = w_out.shape[1]
    assert m % FFN_TILE == 0 and (2 * d_ff) % FFN_LOAD_CHUNK == 0 and d_ff % (FFN_LOAD_CHUNK // 2) == 0
    assert w_in.dtype == F32 and w_out.dtype == F32
    return pl.pallas_call(
        functools.partial(_ffn_kernel, d_ff=d_ff, layer=l),
        grid=(m // FFN_TILE,),
        in_specs=[
            pl.BlockSpec((FFN_TILE, d), lambda i: (i, 0)),
            pl.BlockSpec(memory_space=pl.ANY),
            pl.BlockSpec(memory_space=pl.ANY),
            pl.BlockSpec((None, 1, d), lambda i: (4 * l + j, 0, 0)),
            pl.BlockSpec((None, 1, d), lambda i: (4 * l + j, 0, 0)),
        ],
        out_specs=pl.BlockSpec((FFN_TILE, d), lambda i: (i, 0)),
        out_shape=jax.ShapeDtypeStruct((m, d), F32),
        scratch_shapes=[pltpu.VMEM((FFN_TILE, d_ff), BF16),
                        pltpu.VMEM((d, 2 * d_ff), BF16),
                        pltpu.VMEM((d_ff, d), BF16),
                        pltpu.VMEM((2, d, FFN_LOAD_CHUNK), F32),
                        pltpu.VMEM((2, FFN_LOAD_CHUNK // 2, d), F32),
                        pltpu.SemaphoreType.DMA((2,)),
                        pltpu.SemaphoreType.DMA((2,))],
        compiler_params=_cparams(1),
        name="ffn",
    )(xf, w_in, w_out, ln_g, ln_b)


def _even_kernel(x_ref, win_ref, wout_ref, poolw_ref, pscale_ref, sg_ref, sb_ref, sw_ref, sbt_ref,
                 g_ref, b_ref, o_ref, ext_ref):
    s = pl.program_id(1)
    tm = x_ref.shape[0]
    width = pscale_ref.shape[-1]
    pw = width // len(POOL_WINDOWS)
    heads = sw_ref.shape[0]
    hw = width // heads
    sub = SUB_ROWS
    nchunk = sub // SGU_CHUNK
    subs = range(tm // sub)
    tiles = [slice(t * sub, (t + 1) * sub) for t in subs]

    @pl.when(s == 0)
    def _():
        ext_ref[0:POOL_HALO, :] = jnp.zeros((POOL_HALO, width), F32)

    h = [_dot(x_ref[rows, :], win_ref[...]) for rows in tiles]
    for t in subs:
        ext_ref[POOL_HALO + t * sub:POOL_HALO + (t + 1) * sub, :] = h[t][:, :width]

    ti = lax.broadcasted_iota(jnp.int32, (SGU_CHUNK, SGU_CHUNK), 0)
    si = lax.broadcasted_iota(jnp.int32, (SGU_CHUNK, SGU_CHUNK), 1)
    w_sgu = [jnp.where(ti >= si, sw_ref[hd], 0.0).astype(BF16) for hd in range(heads)]

    ya, yb = [], []
    for t in subs:
        ext = ext_ref[t * sub:(t + 1) * sub + POOL_HALO, :]
        pos = (s * tm + t * sub + 1 + _row_ids(sub, 1)).astype(F32)
        pooled = []
        for g, win in enumerate(POOL_WINDOWS):
            acc = ext[:, g * pw:(g + 1) * pw]
            d = 1
            while d < win:
                acc = acc + pltpu.roll(acc, d, axis=0)
                d *= 2
            mean = acc[POOL_HALO:, :] / jnp.minimum(pos, float(win))
            pooled.append(mean - h[t][:, g * pw:(g + 1) * pw])
        yat = [_dot(jnp.concatenate(pooled[p * POOL_PACK:(p + 1) * POOL_PACK], axis=-1), poolw_ref[p])
               for p in range(len(POOL_WINDOWS) // POOL_PACK)]
        ya.append(jnp.concatenate(yat, axis=-1) * pscale_ref[...])

        u = jax.nn.gelu(h[t][:, width:2 * width])
        v = jax.nn.gelu(h[t][:, 2 * width:3 * width])
        vb = _layer_norm(v, sg_ref[...], sb_ref[...]).astype(BF16)
        mixed = []
        for hd in range(heads):
            bias = sbt_ref[:, hd:hd + 1]
            vh = jnp.concatenate([vb[c * SGU_CHUNK:(c + 1) * SGU_CHUNK, hd * hw:(hd + 1) * hw]
                                  for c in range(nchunk)], axis=1)
            mixed.append(jnp.dot(w_sgu[hd], vh, preferred_element_type=F32) + bias)
        yb.append(u * jnp.concatenate(
            [jnp.concatenate([mixed[hd][:, c * hw:(c + 1) * hw] for hd in range(heads)], axis=1)
             for c in range(nchunk)], axis=0))

    out = [_dot(ya[t], wout_ref[0:width, :]) + _dot(yb[t], wout_ref[width:2 * width, :]) for t in subs]
    for t, rows in enumerate(tiles):
        o_ref[rows, :] = _layer_norm(DEEPNORM_ALPHA * x_ref[rows, :] + out[t], g_ref[...], b_ref[...])
    ext_ref[0:POOL_HALO, :] = ext_ref[tm:tm + POOL_HALO, :]


def _even_layer(xf, w_in, w_out, pool_w, pool_scale, sgu_g, sgu_b, sgu_w, sgu_bt, ln_g, ln_b, e, l, nb, ns):
    m, d = xf.shape
    n_in = w_in.shape[-1]
    width = pool_scale.shape[-1]
    full = lambda shape: pl.BlockSpec((None,) + shape, lambda b, s: (e,) + (0,) * len(shape))
    return pl.pallas_call(
        _even_kernel,
        grid=(nb, ns),
        in_specs=[
            pl.BlockSpec((WIDE_TILE, d), lambda b, s: (b * ns + s, 0)),
            full((d, n_in)),
            full((2 * width, d)),
            full(pool_w.shape[1:]),
            full((1, width)),
            full((1, width)),
            full((1, width)),
            full(sgu_w.shape[1:]),
            full(sgu_bt.shape[1:]),
            pl.BlockSpec((None, 1, d), lambda b, s: (4 * l + 1, 0, 0)),
            pl.BlockSpec((None, 1, d), lambda b, s: (4 * l + 1, 0, 0)),
        ],
        out_specs=pl.BlockSpec((WIDE_TILE, d), lambda b, s: (b * ns + s, 0)),
        out_shape=jax.ShapeDtypeStruct((m, d), F32),
        scratch_shapes=[pltpu.VMEM((POOL_HALO + WIDE_TILE, width), F32)],
        compiler_params=_cparams(),
        name="even_mixer",
    )(xf, w_in, w_out, pool_w, pool_scale, sgu_g, sgu_b, sgu_w, sgu_bt, ln_g, ln_b)


def _odd_pre_kernel(x_ref, win_ref, mu_ref, w0_ref, wup_ref, a0_ref, aup_ref, gup_ref, kk_ref, ka_ref,
                    rk_ref, cw_ref, cb_ref, wa_ref, ba_ref, wx_ref, bx_ref, lam_ref, ones_ref,
                    r_o, lw_o, k_o, v_o, a_o, b_o, g_o, bonus_o, yd_o,
                    hprev_ref, xr_ref, hlru_ref):
    s = pl.program_id(1)
    tm = x_ref.shape[0]
    w = w0_ref.shape[-1]
    n_c = mu_ref.shape[-1]
    sub = SUB_ROWS
    nsub = tm // sub
    tiles = [slice(t * sub, (t + 1) * sub) for t in range(nsub)]
    dr = wup_ref.shape[0]
    ar = aup_ref.shape[0]
    ones_bd = ones_ref[...]

    @pl.when(s == 0)
    def _():
        hprev_ref[...] = jnp.zeros(hprev_ref.shape, F32)
        xr_ref[0:CONV_HALO, :] = jnp.zeros((CONV_HALO, w), F32)
        hlru_ref[...] = jnp.zeros(hlru_ref.shape, F32)

    first = _row_ids(sub, 1) == 0
    ngrp = sub // SUBLANES
    sub_id = lax.broadcasted_iota(jnp.int32, (ngrp, SUBLANES, w), 1)

    def project_rwkv(t):
        return _dot(x_ref[tiles[t], :], win_ref[:, :n_c])

    def project_lru(t):
        hd = _dot(x_ref[tiles[t], :], win_ref[:, n_c:])
        xr_ref[CONV_HALO + t * sub:CONV_HALO + (t + 1) * sub, :] = hd[:, w:]
        return hd[:, :w], xr_ref[t * sub:(t + 1) * sub + CONV_HALO, :]

    def rwkv_part(t, hc, hprev):
        rows = tiles[t]
        prev = jnp.where(first, hprev, pltpu.roll(hc, 1, axis=0))
        hc = hc + mu_ref[...] * (prev - hc)
        r = hc[:, 0:w]
        k = hc[:, w:2 * w]
        v = hc[:, 2 * w:3 * w]
        rest = hc[:, 3 * w:]
        wd = rest[:, 0:dr]
        ad = rest[:, dr:dr + ar]
        gd = rest[:, dr + ar:]

        z = w0_ref[...] + _dot(jnp.tanh(wd), wup_ref[...])
        logw = -_softplus(-z) - 0.5
        lw_o[rows, :] = -jnp.exp(logw)
        asig = jax.nn.sigmoid(a0_ref[...] + _dot(ad, aup_ref[...]))
        g_o[rows, :] = _dot(jax.nn.sigmoid(gd), gup_ref[...])
        kk = k * kk_ref[...]
        kk = kk * lax.rsqrt(jnp.maximum(_head_sum(kk * kk, ones_bd, 1), 1e-24))
        k2 = k * (1.0 + (asig - 1.0) * ka_ref[...])
        r_o[rows, :] = r
        k_o[rows, :] = k2
        v_o[rows, :] = v
        a_o[rows, :] = -kk
        b_o[rows, :] = kk * asig
        bonus_o[rows, :] = _head_sum(r * k2 * rk_ref[...], ones_bd, 2) * v

    def lru_part(t, gate, ext, hcar):
        xc = cw_ref[CONV_WIDTH - 1:CONV_WIDTH, :] * ext[CONV_HALO:, :] + cb_ref[...]
        for j in range(1, CONV_WIDTH):
            xc = xc + cw_ref[CONV_WIDTH - 1 - j:CONV_WIDTH - j, :] * pltpu.roll(ext, j, axis=0)[CONV_HALO:, :]
        rec = jax.nn.sigmoid(_dot(xc, wa_ref[...]) + ba_ref[...])
        inp = jax.nn.sigmoid(_dot(xc, wx_ref[...]) + bx_ref[...])
        log_a = -LRU_C * rec * _softplus(-lam_ref[...])
        a = jnp.exp(log_a)
        bx = jnp.sqrt(-jnp.tanh(log_a) * (a * a + 1.0)) * (inp * xc)
        a3 = a.reshape(ngrp, SUBLANES, w)
        b3 = bx.reshape(ngrp, SUBLANES, w)
        d = 1
        while d < SUBLANES:
            keep = sub_id >= d
            a_sh = jnp.where(keep, pltpu.roll(a3, d, axis=1), 1.0)
            b_sh = jnp.where(keep, pltpu.roll(b3, d, axis=1), 0.0)
            b3 = a3 * b_sh + b3
            a3 = a3 * a_sh
            d *= 2
        gate_act = jax.nn.gelu(gate)
        for j in range(ngrp):
            hj = a3[j] * hcar + b3[j]
            lo = t * sub + j * SUBLANES
            yd_o[lo:lo + SUBLANES, :] = hj * gate_act[j * SUBLANES:(j + 1) * SUBLANES, :]
            hcar = hj[SUBLANES - 1:SUBLANES, :]
        return hcar

    hprev = hprev_ref[...]
    hcar = hlru_ref[...]
    hc = project_rwkv(0)
    gate, ext = project_lru(0)
    for t in range(nsub):
        last = t + 1 == nsub
        hc_next = None if last else project_rwkv(t + 1)
        rwkv_part(t, hc, hprev)
        hprev = hc[sub - 1:sub, :]
        lru_next = (None, None) if last else project_lru(t + 1)
        hcar = lru_part(t, gate, ext, hcar)
        hc, (gate, ext) = hc_next, lru_next
    hprev_ref[...] = hprev
    hlru_ref[...] = hcar
    xr_ref[0:CONV_HALO, :] = xr_ref[tm:tm + CONV_HALO, :]


def _odd_pre(xf, w_in, mu, w0, w_up, a0, a_up, g_up, k_k, k_a, r_k, conv_w, conv_b, wa_bd, b_a, wx_bd, b_x,
             lam, ones_bd, e, nb, ns):
    m, d = xf.shape
    w = w0.shape[-1]
    full = lambda arr: pl.BlockSpec((None,) + arr.shape[1:], lambda b, s: (e,) + (0,) * (arr.ndim - 1))
    row = pl.BlockSpec((ROW_TILE, w), lambda b, s: (b * ns + s, 0))
    params = (w_in, mu, w0, w_up, a0, a_up, g_up, k_k, k_a, r_k, conv_w, conv_b, wa_bd, b_a, wx_bd, b_x, lam)
    return pl.pallas_call(
        _odd_pre_kernel,
        grid=(nb, ns),
        in_specs=[pl.BlockSpec((ROW_TILE, d), lambda b, s: (b * ns + s, 0))]
        + [full(p) for p in params]
        + [pl.BlockSpec(ones_bd.shape, lambda b, s: (0, 0))],
        out_specs=[row] * 9,
        out_shape=[jax.ShapeDtypeStruct((m, w), F32)] * 9,
        scratch_shapes=[pltpu.VMEM((1, mu.shape[-1]), F32),
                        pltpu.VMEM((CONV_HALO + ROW_TILE, w), F32),
                        pltpu.VMEM((1, w), F32)],
        compiler_params=_cparams(),
        name="odd_pre",
    )(xf, *params, ones_bd)


def _rwkv_kernel(r_ref, lw_ref, k_ref, v_ref, a_ref, b_ref, y_ref, t_ref, q_ref, g_ref, h_ref):
    s = pl.program_id(1)
    tm, w = r_ref.shape
    n = RWKV_HEAD_DIM
    lc = RWKV_CHUNK
    gw = MXU_WIDTH
    hpg = gw // n
    groups = range(w // gw)

    @pl.when(s == 0)
    def _():
        t_ref[...] = jnp.zeros(t_ref.shape, F32)

    row_t = lax.broadcasted_iota(jnp.int32, (lc, gw), 0)
    col = lax.broadcasted_iota(jnp.int32, (lc, gw), 1)
    col_s = col & (n - 1)
    col_h = col >> (n.bit_length() - 1)
    strict = row_t > col_s
    incl = row_t >= col_s
    eye_m = row_t == col_s
    eye = eye_m.astype(F32)
    merge_masks = [(row_t >> 1) == (col_s >> 1)]
    size = 2
    while size < lc:
        shift = size.bit_length()
        merge_masks.append(((row_t >> shift) == (col_s >> shift)) & ((row_t & size) != 0) & ((col_s & size) == 0))
        size *= 2
    blk_r = lax.broadcasted_iota(jnp.int32, (gw, gw), 0) >> (n.bit_length() - 1)
    blk_c = lax.broadcasted_iota(jnp.int32, (gw, gw), 1) >> (n.bit_length() - 1)
    bd_mask = (blk_r == blk_c).astype(BF16)
    rows = _row_ids(lc, w)

    def bd(x):
        xb = x.astype(BF16)
        return jnp.concatenate([xb] * hpg, axis=0) * bd_mask

    def mm(a, b_bf16, dn=_NN):
        return lax.dot_general(a.astype(BF16), b_bf16, dn, preferred_element_type=F32)

    def diag_blocks(p):
        out = p[(hpg - 1) * n:, :]
        for h in range(hpg - 1):
            out = jnp.where(col_h == h, p[h * n:(h + 1) * n, :], out)
        return out

    def state_free(i):
        at, rt, bt, kt, bh, kh, v, pend, where = [], [], [], [], [], [], [], [], []
        for ci in range(RWKV_INTERLEAVE):
            sl = pl.ds((i * RWKV_INTERLEAVE + ci) * lc, lc)
            lw = lw_ref[sl, :]
            cum = lw
            d = 1
            while d < lc:
                cum = cum + jnp.where(rows >= d, pltpu.roll(cum, d, axis=0), 0.0)
                d *= 2
            cl = cum[lc - 1:lc, :]
            einv = jnp.exp(-cum)
            eend = jnp.exp(cl - cum)
            a = a_ref[sl, :]
            b = b_ref[sl, :]
            k = k_ref[sl, :]
            full = (a * jnp.exp(cum - lw), r_ref[sl, :] * jnp.exp(cum), b * einv, k * einv, b * eend, k * eend,
                    v_ref[sl, :], jnp.exp(cl))
            for g in groups:
                for dst, t in zip((at, rt, bt, kt, bh, kh, v, pend), full):
                    dst.append(t[:, g * gw:(g + 1) * gw])
                where.append((sl, slice(g * gw, (g + 1) * gw)))
            yield
        units = range(len(where))
        lhs = [jnp.concatenate([at[u], rt[u]], axis=0).astype(BF16) for u in units]
        sb = [mm(lhs[u], bd(bt[u]), _NT) for u in units]
        yield
        sk = [mm(lhs[u], bd(kt[u]), _NT) for u in units]
        a_ab = [jnp.where(strict, t[:lc], 0.0) for t in sb]
        m_rb = [jnp.where(incl, t[lc:], 0.0) for t in sb]
        yield
        a_ak = [jnp.where(strict, t[:lc], 0.0) for t in sk]
        m_rk = [jnp.where(incl, t[lc:], 0.0) for t in sk]
        xm = [mm(jnp.concatenate([a_ak[u], m_rk[u]], axis=0), bd(v[u])) for u in units]
        x0 = [t[:lc] for t in xm]
        ykv = [t[lc:] for t in xm]
        yield
        inv = [eye + jnp.where(merge_masks[0], t, 0.0) for t in a_ab]
        for below in merge_masks[1:]:
            we = [mm(inv[u], bd(jnp.where(below, a_ab[u], 0.0))) for u in units]
            yield
            inv = [inv[u] + mm(we[u], bd(inv[u])) for u in units]
            yield
        ap = [mm(inv[u], bd(at[u])) for u in units]
        yield
        vp = [mm(inv[u], bd(x0[u])) for u in units]
        yield
        q = [rt[u] + mm(m_rb[u], bd(ap[u])) for u in units]
        for u in units:
            q_ref[where[u]] = q[u]
        yield
        yl = [mm(m_rb[u], bd(vp[u])) + ykv[u] for u in units]
        for u in units:
            y_ref[where[u]] = yl[u]
        yield
        gm = [diag_blocks(mm(bh[u], ap[u].astype(BF16), _TN))
              + jnp.where(eye_m, jnp.broadcast_to(pend[u], (lc, gw)), 0.0) for u in units]
        for u in units:
            g_ref[where[u]] = gm[u]
        yield
        hm = [diag_blocks(mm(jnp.concatenate([bh[u], kh[u]], axis=0),
                             jnp.concatenate([vp[u], v[u]], axis=0).astype(BF16), _TN)) for u in units]
        for u in units:
            h_ref[where[u]] = hm[u]

    def state_step(c):
        sl = pl.ds(c * lc, lc)
        fin = [mm(jnp.concatenate([q_ref[sl, g * gw:(g + 1) * gw], g_ref[sl, g * gw:(g + 1) * gw]], axis=0),
                  bd(t_ref[:, g * gw:(g + 1) * gw])) for g in groups]
        for g in groups:
            gs = slice(g * gw, (g + 1) * gw)
            y_ref[sl, gs] = y_ref[sl, gs] + fin[g][:lc]
            t_ref[:, gs] = fin[g][lc:] + h_ref[sl, gs]

    ngroups = tm // (lc * RWKV_INTERLEAVE)
    running, ready_chunks, slot, started = [], [], 0, 0
    while started < ngroups or running or ready_chunks:
        if started < ngroups and slot == started * RWKV_LAG:
            running.append((state_free(started), started))
            started += 1
        still = []
        for gen, gi in running:
            try:
                next(gen)
                still.append((gen, gi))
            except StopIteration:
                ready_chunks.extend(range(gi * RWKV_INTERLEAVE, (gi + 1) * RWKV_INTERLEAVE))
        running = still
        if ready_chunks:
            state_step(ready_chunks.pop(0))
        slot += 1


def _rwkv_scan(r, lw, k, v, a, b, nb, ns):
    m, w = r.shape
    assert RWKV_CHUNK == RWKV_HEAD_DIM and w % MXU_WIDTH == 0 and MXU_WIDTH % RWKV_HEAD_DIM == 0
    assert RWKV_HEAD_DIM & (RWKV_HEAD_DIM - 1) == 0
    row = pl.BlockSpec((ROW_TILE, w), lambda bb, s: (bb * ns + s, 0))
    return pl.pallas_call(
        _rwkv_kernel,
        grid=(nb, ns),
        in_specs=[row] * 6,
        out_specs=row,
        out_shape=jax.ShapeDtypeStruct((m, w), F32),
        scratch_shapes=[pltpu.VMEM((RWKV_HEAD_DIM, w), F32)]
        + [pltpu.VMEM((ROW_TILE, w), F32)] * 3,
        compiler_params=_cparams(),
        name="rwkv7",
    )(r, lw, k, v, a, b)


def _odd_post_kernel(x_ref, y_ref, g_ref, bonus_ref, yd_ref, gng_ref, gnb_ref, ones_ref, wout_ref,
                     lg_ref, lb_ref, o_ref):
    w = y_ref.shape[-1]
    sub = SUB_ROWS
    tiles = [slice(t * sub, (t + 1) * sub) for t in range(x_ref.shape[0] // sub)]
    ones_bd = ones_ref[...]
    inv_n = 1.0 / RWKV_HEAD_DIM
    y = [y_ref[rows, :] for rows in tiles]
    mu = [_head_sum(t, ones_bd, 2) * inv_n for t in y]
    yc = [a - b for a, b in zip(y, mu)]
    var = [_head_sum(t * t, ones_bd, 1) * inv_n for t in yc]
    out = []
    for t, rows in enumerate(tiles):
        yn = yc[t] * lax.rsqrt(var[t] + RWKV_GN_EPS) * gng_ref[...] + gnb_ref[...]
        y_rwkv = (yn + bonus_ref[rows, :]) * g_ref[rows, :]
        out.append(_dot(y_rwkv, wout_ref[0:w, :]) + _dot(yd_ref[rows, :], wout_ref[w:2 * w, :]))
    for t, rows in enumerate(tiles):
        o_ref[rows, :] = _layer_norm(DEEPNORM_ALPHA * x_ref[rows, :] + out[t], lg_ref[...], lb_ref[...])


def _odd_post(xf, y, g, bonus, yd, gn_g, gn_b, ones_bd, w_out, ln_g, ln_b, e, l, nb, ns):
    m, d = xf.shape
    w = y.shape[-1]
    row = pl.BlockSpec((WIDE_TILE, w), lambda b, s: (b * ns + s, 0))
    xrow = pl.BlockSpec((WIDE_TILE, d), lambda b, s: (b * ns + s, 0))
    return pl.pallas_call(
        _odd_post_kernel,
        grid=(nb, ns),
        in_specs=[xrow, row, row, row, row,
                  pl.BlockSpec((None, 1, w), lambda b, s: (e, 0, 0)),
                  pl.BlockSpec((None, 1, w), lambda b, s: (e, 0, 0)),
                  pl.BlockSpec(ones_bd.shape, lambda b, s: (0, 0)),
                  pl.BlockSpec((None, 2 * w, d), lambda b, s: (e, 0, 0)),
                  pl.BlockSpec((None, 1, d), lambda b, s: (4 * l + 1, 0, 0)),
                  pl.BlockSpec((None, 1, d), lambda b, s: (4 * l + 1, 0, 0))],
        out_specs=xrow,
        out_shape=jax.ShapeDtypeStruct((m, d), F32),
        compiler_params=_cparams(),
        name="odd_post",
    )(xf, y, g, bonus, yd, gn_g, gn_b, ones_bd, w_out, ln_g, ln_b)


def _kv_kernel(mem_ref, wkv_ref, kv_ref):
    kv_ref[...] = _dot(mem_ref[...], wkv_ref[...]).astype(BF16)


def _kv_proj(mem, w_kv):
    nb, mlen, d = mem.shape
    nl, _, n2 = w_kv.shape
    return pl.pallas_call(
        _kv_kernel,
        grid=(nl, nb),
        in_specs=[pl.BlockSpec((None, mlen, d), lambda l, b: (b, 0, 0)),
                  pl.BlockSpec((None, d, n2), lambda l, b: (l, 0, 0))],
        out_specs=pl.BlockSpec((None, None, mlen, n2), lambda l, b: (l, b, 0, 0)),
        out_shape=jax.ShapeDtypeStruct((nl, nb, mlen, n2), BF16),
        compiler_params=_cparams(),
        name="xattn_kv",
    )(mem, w_kv)


def _xattn_kernel(x_ref, wq_ref, kv_ref, wo_ref, g_ref, b_ref, o_ref):
    d = x_ref.shape[-1]
    hd = d // XATTN_HEADS
    sub = SUB_ROWS
    nsub = x_ref.shape[0] // sub
    tiles = [slice(t * sub, (t + 1) * sub) for t in range(nsub)]
    heads = XATTN_HEADS
    n_units = nsub * heads
    q, p, proj, outs = {}, {}, {}, [[] for _ in tiles]
    q[0] = _dot(x_ref[tiles[0], :], wq_ref[...])
    for i in range(n_units + XATTN_SKEW + 1):
        if i < n_units:
            t, h = divmod(i, heads)
            if h == 1 and t + 1 < nsub:
                q[t + 1] = _dot(x_ref[tiles[t + 1], :], wq_ref[...])
            qh = q[t][:, h * hd:(h + 1) * hd].astype(BF16)
            sc = lax.dot_general(qh, kv_ref[:, h * hd:(h + 1) * hd], _NT, preferred_element_type=F32)
            sc = sc * (hd ** -0.5)
            sc = sc - jnp.max(sc, axis=-1, keepdims=True)
            e = jnp.exp(sc)
            p[i] = (e / jnp.sum(e, axis=-1, keepdims=True)).astype(BF16)
        j = i - XATTN_SKEW
        if 0 <= j < n_units:
            t, h = divmod(j, heads)
            outs[t].append(jnp.dot(p.pop(j), kv_ref[:, d + h * hd:d + (h + 1) * hd], preferred_element_type=F32))
        j = i - XATTN_SKEW - 1
        if 0 <= j < n_units and j % heads == heads - 1:
            t = j // heads
            proj[t] = _dot(jnp.concatenate(outs[t], axis=-1), wo_ref[...])
    for t, rows in enumerate(tiles):
        o_ref[rows, :] = _layer_norm(DEEPNORM_ALPHA * x_ref[rows, :] + proj[t], g_ref[...], b_ref[...])


def _xattn_layer(xf, w_q, kv, w_o, ln_g, ln_b, l, nb, ns):
    m, d = xf.shape
    mlen = kv.shape[2]
    xrow = pl.BlockSpec((XATTN_TILE, d), lambda b, s: (b * ns + s, 0))
    return pl.pallas_call(
        _xattn_kernel,
        grid=(nb, ns),
        in_specs=[xrow,
                  pl.BlockSpec((None, d, d), lambda b, s: (l, 0, 0)),
                  pl.BlockSpec((None, None, mlen, 2 * d), lambda b, s: (l, b, 0, 0)),
                  pl.BlockSpec((None, d, d), lambda b, s: (l, 0, 0)),
                  pl.BlockSpec((None, 1, d), lambda b, s: (4 * l + 2, 0, 0)),
                  pl.BlockSpec((None, 1, d), lambda b, s: (4 * l + 2, 0, 0))],
        out_specs=xrow,
        out_shape=jax.ShapeDtypeStruct((m, d), F32),
        compiler_params=_cparams(),
        name="xattn",
    )(xf, w_q, kv, w_o, ln_g, ln_b)


def _block_diag(w):
    nblk, bi, bj = w.shape[-3:]
    eye = jnp.eye(nblk, dtype=w.dtype)
    out = w[..., :, :, None, :] * eye[:, None, :, None]
    return out.reshape(w.shape[:-3] + (nblk * bi, nblk * bj))


def kernel(x, mem, ffn1_w_in, ffn1_w_out, ffn2_w_in, ffn2_w_out, ln_g, ln_b, xattn_w_q, xattn_w_kv, xattn_w_o, even_w_in, even_w_out, pool_w, pool_scale, sgu_ln_g, sgu_ln_b, sgu_w, sgu_b, odd_w_in, odd_w_out, rwkv_mu, rwkv_w0, rwkv_w_up, rwkv_a0, rwkv_a_up, rwkv_g_up, rwkv_k_k, rwkv_k_a, rwkv_r_k, rwkv_gn_g, rwkv_gn_b, lru_conv_w, lru_conv_b, lru_w_a, lru_b_a, lru_w_x, lru_b_x, lru_lambda):
    nb, seq, d = x.shape
    assert seq % WIDE_TILE == 0 and seq % XATTN_TILE == 0 and WIDE_TILE % ROW_TILE == 0 and ROW_TILE % SUB_ROWS == 0
    assert SUB_ROWS % SGU_CHUNK == 0 and ROW_TILE % (RWKV_CHUNK * RWKV_INTERLEAVE) == 0
    ns = seq // ROW_TILE
    nsw = seq // WIDE_TILE
    depth = ffn1_w_in.shape[0]
    xf = x.reshape(nb * seq, d)

    bf = lambda t: t.astype(BF16)
    row3 = lambda t: t.reshape(t.shape[0], 1, -1)
    lg = ln_g.reshape(-1, 1, d)
    lb = ln_b.reshape(-1, 1, d)
    f1_in, f1_out, f2_in, f2_out = ffn1_w_in, ffn1_w_out, ffn2_w_in, ffn2_w_out
    wq, wo = bf(xattn_w_q), bf(xattn_w_o)
    kv = _kv_proj(mem, bf(xattn_w_kv))
    ev_in, ev_out = bf(even_w_in), bf(even_w_out)
    od_in, od_out = bf(odd_w_in), bf(odd_w_out)
    n_e, n_pool, pool_dim = pool_w.shape[:3]
    pw = bf(_block_diag(pool_w.reshape(n_e, n_pool // POOL_PACK, POOL_PACK, pool_dim, pool_dim)))
    sgu_bt = jnp.swapaxes(sgu_b, 1, 2)
    wa_bd, wx_bd = bf(_block_diag(lru_w_a)), bf(_block_diag(lru_w_x))
    w_rwkv = rwkv_w0.shape[-1]
    ones_bd = _block_diag(jnp.ones((w_rwkv // RWKV_HEAD_DIM, RWKV_HEAD_DIM, RWKV_HEAD_DIM), BF16))
    r_k = rwkv_r_k.reshape(rwkv_r_k.shape[0], 1, -1)

    for l in range(depth):
        xf = _ffn_layer(xf, f1_in, f1_out, lg, lb, l, 0)
        e = l // 2
        if l % 2 == 0:
            xf = _even_layer(xf, ev_in, ev_out, pw, row3(pool_scale), row3(sgu_ln_g), row3(sgu_ln_b),
                             sgu_w, sgu_bt, lg, lb, e, l, nb, nsw)
        else:
            r, lw, k, v, a, b, g, bonus, yd = _odd_pre(
                xf, od_in, row3(rwkv_mu), row3(rwkv_w0), bf(rwkv_w_up), row3(rwkv_a0), bf(rwkv_a_up),
                bf(rwkv_g_up), row3(rwkv_k_k), row3(rwkv_k_a), r_k, lru_conv_w, row3(lru_conv_b),
                wa_bd, row3(lru_b_a), wx_bd, row3(lru_b_x), row3(lru_lambda), ones_bd, e, nb, ns)
            y = _rwkv_scan(r, lw, k, v, a, b, nb, ns)
            xf = _odd_post(xf, y, g, bonus, yd, row3(rwkv_gn_g), row3(rwkv_gn_b), ones_bd, od_out,
                           lg, lb, e, l, nb, nsw)
        xf = _xattn_layer(xf, wq, kv, wo, lg, lb, l, nb, seq // XATTN_TILE)
        xf = _ffn_layer(xf, f2_in, f2_out, lg, lb, l, 3)
    return xf.reshape(nb, seq, d)
```

```python
import functools

import jax
import jax.numpy as jnp
from jax import lax
from jax.experimental import pallas as pl
from jax.experimental.pallas import tpu as pltpu

F32 = jnp.float32
BF16 = jnp.bfloat16

DEPTH = 4
LN_EPS = 1e-5
DEEPNORM_ALPHA = (2 * DEPTH) ** 0.25
MACARON_WEIGHT = 0.5
POOL_WINDOWS = (2, 4, 8, 16)
POOL_HALO = 16
POOL_PACK = 2
SGU_CHUNK = 128
XATTN_HEADS = 4
XATTN_SKEW = 2
RWKV_HEAD_DIM = 64
RWKV_CHUNK = 64
RWKV_GN_EPS = 64e-5
MXU_WIDTH = 256
RWKV_INTERLEAVE = 2
RWKV_LAG = 4
LRU_C = 8.0
CONV_WIDTH = 4
SUBLANES = 8
CONV_HALO = SUBLANES

ROW_TILE = 512
FFN_TILE = 1024
WIDE_TILE = 1024
XATTN_TILE = 1024
SUB_ROWS = 256
FFN_CHUNK = 256
FFN_LOAD_CHUNK = 512
VMEM_LIMIT = 56 * 1024 * 1024


def _cparams(grid_rank=2):
    return pltpu.CompilerParams(dimension_semantics=("arbitrary",) * grid_rank,
                                vmem_limit_bytes=VMEM_LIMIT)


def _dot(a, b):
    return jnp.dot(a.astype(BF16), b.astype(BF16), preferred_element_type=F32)


_NN = (((1,), (0,)), ((), ()))
_NT = (((1,), (1,)), ((), ()))
_TN = (((0,), (0,)), ((), ()))


def _head_sum(x, ones_bd, pieces):
    total = None
    rest = x
    for i in range(pieces):
        p = rest.astype(BF16)
        if i + 1 < pieces:
            rest = rest - p.astype(F32)
        term = jnp.dot(p, ones_bd, preferred_element_type=F32)
        total = term if total is None else total + term
    return total


def _layer_norm(y, g, b, eps=LN_EPS):
    mu = jnp.mean(y, axis=-1, keepdims=True)
    yc = y - mu
    var = jnp.mean(yc * yc, axis=-1, keepdims=True)
    return yc * lax.rsqrt(var + eps) * g + b


def _softplus(t):
    return jnp.maximum(t, 0.0) + jnp.log(1.0 + jnp.exp(-jnp.abs(t)))


def _row_ids(rows, cols):
    return lax.broadcasted_iota(jnp.int32, (rows, cols), 0)


def _load_cast(src_hbm, dst_ref, stage_ref, sem, axis, chunk):
    n = src_hbm.shape[axis] // chunk

    def piece(c):
        sl = slice(c * chunk, (c + 1) * chunk)
        return (slice(None), sl) if axis == 1 else (sl, slice(None))

    def copy(c):
        return pltpu.make_async_copy(src_hbm.at[piece(c)], stage_ref.at[c % 2], sem.at[c % 2])

    copy(0).start()
    for c in range(n):
        if c + 1 < n:
            copy(c + 1).start()
        copy(c).wait()
        dst_ref[piece(c)] = stage_ref[c % 2].astype(BF16)


def _ffn_kernel(x_ref, win_hbm, wout_hbm, g_ref, b_ref, o_ref, h_ref, win_ref, wout_ref,
                stage_in, stage_out, sem_in, sem_out, *, d_ff, layer):
    @pl.when(pl.program_id(0) == 0)
    def _():
        _load_cast(win_hbm.at[layer], win_ref, stage_in, sem_in, 1, FFN_LOAD_CHUNK)
        _load_cast(wout_hbm.at[layer], wout_ref, stage_out, sem_out, 0, FFN_LOAD_CHUNK // 2)

    sub = SUB_ROWS
    nsub = x_ref.shape[0] // sub
    accs = []
    for t in range(nsub):
        rows = slice(t * sub, (t + 1) * sub)
        xb = x_ref[rows, :].astype(BF16)
        for c in range(d_ff // FFN_CHUNK):
            lo = c * FFN_CHUNK
            gate = jnp.dot(xb, win_ref[:, lo:lo + FFN_CHUNK], preferred_element_type=F32)
            up = jnp.dot(xb, win_ref[:, d_ff + lo:d_ff + lo + FFN_CHUNK], preferred_element_type=F32)
            h_ref[rows, lo:lo + FFN_CHUNK] = (gate * jax.nn.sigmoid(gate) * up).astype(BF16)
        accs.append(jnp.dot(h_ref[rows, :], wout_ref[...], preferred_element_type=F32))
    for t in range(nsub):
        rows = slice(t * sub, (t + 1) * sub)
        y = DEEPNORM_ALPHA * x_ref[rows, :] + MACARON_WEIGHT * accs[t]
        o_ref[rows, :] = _layer_norm(y, g_ref[...], b_ref[...])


def _ffn_layer(xf, w_in, w_out, ln_g, ln_b, l, j):
    m, d = xf.shape
    d_ff = w_out.shape[1]
    assert m % FFN_TILE == 0 and (2 * d_ff) % FFN_LOAD_CHUNK == 0 and d_ff % (FFN_LOAD_CHUNK // 2) == 0
    assert w_in.dtype == F32 and w_out.dtype == F32
    return pl.pallas_call(
        functools.partial(_ffn_kernel, d_ff=d_ff, layer=l),
        grid=(m // FFN_TILE,),
        in_specs=[
            pl.BlockSpec((FFN_TILE, d), lambda i: (i, 0)),
            pl.BlockSpec(memory_space=pl.ANY),
            pl.BlockSpec(memory_space=pl.ANY),
            pl.BlockSpec((None, 1, d), lambda i: (4 * l + j, 0, 0)),
            pl.BlockSpec((None, 1, d), lambda i: (4 * l + j, 0, 0)),
        ],
        out_specs=pl.BlockSpec((FFN_TILE, d), lambda i: (i, 0)),
        out_shape=jax.ShapeDtypeStruct((m, d), F32),
        scratch_shapes=[pltpu.VMEM((FFN_TILE, d_ff), BF16),
                        pltpu.VMEM((d, 2 * d_ff), BF16),
                        pltpu.VMEM((d_ff, d), BF16),
                        pltpu.VMEM((2, d, FFN_LOAD_CHUNK), F32),
                        pltpu.VMEM((2, FFN_LOAD_CHUNK // 2, d), F32),
                        pltpu.SemaphoreType.DMA((2,)),
                        pltpu.SemaphoreType.DMA((2,))],
        compiler_params=_cparams(1),
        name="ffn",
    )(xf, w_in, w_out, ln_g, ln_b)


def _even_kernel(x_ref, win_ref, wout_ref, poolw_ref, pscale_ref, sg_ref, sb_ref, sw_ref, sbt_ref,
                 g_ref, b_ref, o_ref, ext_ref):
    s = pl.program_id(1)
    tm = x_ref.shape[0]
    width = pscale_ref.shape[-1]
    pw = width // len(POOL_WINDOWS)
    heads = sw_ref.shape[0]
    hw = width // heads
    sub = SUB_ROWS
    nchunk = sub // SGU_CHUNK
    subs = range(tm // sub)
    tiles = [slice(t * sub, (t + 1) * sub) for t in subs]

    @pl.when(s == 0)
    def _():
        ext_ref[0:POOL_HALO, :] = jnp.zeros((POOL_HALO, width), F32)

    h = [_dot(x_ref[rows, :], win_ref[...]) for rows in tiles]
    for t in subs:
        ext_ref[POOL_HALO + t * sub:POOL_HALO + (t + 1) * sub, :] = h[t][:, :width]

    ti = lax.broadcasted_iota(jnp.int32, (SGU_CHUNK, SGU_CHUNK), 0)
    si = lax.broadcasted_iota(jnp.int32, (SGU_CHUNK, SGU_CHUNK), 1)
    w_sgu = [jnp.where(ti >= si, sw_ref[hd], 0.0).astype(BF16) for hd in range(heads)]

    ya, yb = [], []
    for t in subs:
        ext = ext_ref[t * sub:(t + 1) * sub + POOL_HALO, :]
        pos = (s * tm + t * sub + 1 + _row_ids(sub, 1)).astype(F32)
        pooled = []
        for g, win in enumerate(POOL_WINDOWS):
            acc = ext[:, g * pw:(g + 1) * pw]
            d = 1
            while d < win:
                acc = acc + pltpu.roll(acc, d, axis=0)
                d *= 2
            mean = acc[POOL_HALO:, :] / jnp.minimum(pos, float(win))
            pooled.append(mean - h[t][:, g * pw:(g + 1) * pw])
        yat = [_dot(jnp.concatenate(pooled[p * POOL_PACK:(p + 1) * POOL_PACK], axis=-1), poolw_ref[p])
               for p in range(len(POOL_WINDOWS) // POOL_PACK)]
        ya.append(jnp.concatenate(yat, axis=-1) * pscale_ref[...])

        u = jax.nn.gelu(h[t][:, width:2 * width])
        v = jax.nn.gelu(h[t][:, 2 * width:3 * width])
        vb = _layer_norm(v, sg_ref[...], sb_ref[...]).astype(BF16)
        mixed = []
        for hd in range(heads):
            bias = sbt_ref[:, hd:hd + 1]
            vh = jnp.concatenate([vb[c * SGU_CHUNK:(c + 1) * SGU_CHUNK, hd * hw:(hd + 1) * hw]
                                  for c in range(nchunk)], axis=1)
            mixed.append(jnp.dot(w_sgu[hd], vh, preferred_element_type=F32) + bias)
        yb.append(u * jnp.concatenate(
            [jnp.concatenate([mixed[hd][:, c * hw:(c + 1) * hw] for hd in range(heads)], axis=1)
             for c in range(nchunk)], axis=0))

    out = [_dot(ya[t], wout_ref[0:width, :]) + _dot(yb[t], wout_ref[width:2 * width, :]) for t in subs]
    for t, rows in enumerate(tiles):
        o_ref[rows, :] = _layer_norm(DEEPNORM_ALPHA * x_ref[rows, :] + out[t], g_ref[...], b_ref[...])
    ext_ref[0:POOL_HALO, :] = ext_ref[tm:tm + POOL_HALO, :]


def _even_layer(xf, w_in, w_out, pool_w, pool_scale, sgu_g, sgu_b, sgu_w, sgu_bt, ln_g, ln_b, e, l, nb, ns):
    m, d = xf.shape
    n_in = w_in.shape[-1]
    width = pool_scale.shape[-1]
    full = lambda shape: pl.BlockSpec((None,) + shape, lambda b, s: (e,) + (0,) * len(shape))
    return pl.pallas_call(
        _even_kernel,
        grid=(nb, ns),
        in_specs=[
            pl.BlockSpec((WIDE_TILE, d), lambda b, s: (b * ns + s, 0)),
            full((d, n_in)),
            full((2 * width, d)),
            full(pool_w.shape[1:]),
            full((1, width)),
            full((1, width)),
            full((1, width)),
            full(sgu_w.shape[1:]),
            full(sgu_bt.shape[1:]),
            pl.BlockSpec((None, 1, d), lambda b, s: (4 * l + 1, 0, 0)),
            pl.BlockSpec((None, 1, d), lambda b, s: (4 * l + 1, 0, 0)),
        ],
        out_specs=pl.BlockSpec((WIDE_TILE, d), lambda b, s: (b * ns + s, 0)),
        out_shape=jax.ShapeDtypeStruct((m, d), F32),
        scratch_shapes=[pltpu.VMEM((POOL_HALO + WIDE_TILE, width), F32)],
        compiler_params=_cparams(),
        name="even_mixer",
    )(xf, w_in, w_out, pool_w, pool_scale, sgu_g, sgu_b, sgu_w, sgu_bt, ln_g, ln_b)


def _odd_pre_kernel(x_ref, win_ref, mu_ref, w0_ref, wup_ref, a0_ref, aup_ref, gup_ref, kk_ref, ka_ref,
                    rk_ref, cw_ref, cb_ref, wa_ref, ba_ref, wx_ref, bx_ref, lam_ref, ones_ref,
                    r_o, lw_o, k_o, v_o, a_o, b_o, g_o, bonus_o, yd_o,
                    hprev_ref, xr_ref, hlru_ref):
    s = pl.program_id(1)
    tm = x_ref.shape[0]
    w = w0_ref.shape[-1]
    n_c = mu_ref.shape[-1]
    sub = SUB_ROWS
    nsub = tm // sub
    tiles = [slice(t * sub, (t + 1) * sub) for t in range(nsub)]
    dr = wup_ref.shape[0]
    ar = aup_ref.shape[0]
    ones_bd = ones_ref[...]

    @pl.when(s == 0)
    def _():
        hprev_ref[...] = jnp.zeros(hprev_ref.shape, F32)
        xr_ref[0:CONV_HALO, :] = jnp.zeros((CONV_HALO, w), F32)
        hlru_ref[...] = jnp.zeros(hlru_ref.shape, F32)

    first = _row_ids(sub, 1) == 0
    ngrp = sub // SUBLANES
    sub_id = lax.broadcasted_iota(jnp.int32, (ngrp, SUBLANES, w), 1)

    def project_rwkv(t):
        return _dot(x_ref[tiles[t], :], win_ref[:, :n_c])

    def project_lru(t):
        hd = _dot(x_ref[tiles[t], :], win_ref[:, n_c:])
        xr_ref[CONV_HALO + t * sub:CONV_HALO + (t + 1) * sub, :] = hd[:, w:]
        return hd[:, :w], xr_ref[t * sub:(t + 1) * sub + CONV_HALO, :]

    def rwkv_part(t, hc, hprev):
        rows = tiles[t]
        prev = jnp.where(first, hprev, pltpu.roll(hc, 1, axis=0))
        hc = hc + mu_ref[...] * (prev - hc)
        r = hc[:, 0:w]
        k = hc[:, w:2 * w]
        v = hc[:, 2 * w:3 * w]
        rest = hc[:, 3 * w:]
        wd = rest[:, 0:dr]
        ad = rest[:, dr:dr + ar]
        gd = rest[:, dr + ar:]

        z = w0_ref[...] + _dot(jnp.tanh(wd), wup_ref[...])
        logw = -_softplus(-z) - 0.5
        lw_o[rows, :] = -jnp.exp(logw)
        asig = jax.nn.sigmoid(a0_ref[...] + _dot(ad, aup_ref[...]))
        g_o[rows, :] = _dot(jax.nn.sigmoid(gd), gup_ref[...])
        kk = k * kk_ref[...]
        kk = kk * lax.rsqrt(jnp.maximum(_head_sum(kk * kk, ones_bd, 1), 1e-24))
        k2 = k * (1.0 + (asig - 1.0) * ka_ref[...])
        r_o[rows, :] = r
        k_o[rows, :] = k2
        v_o[rows, :] = v
        a_o[rows, :] = -kk
        b_o[rows, :] = kk * asig
        bonus_o[rows, :] = _head_sum(r * k2 * rk_ref[...], ones_bd, 2) * v

    def lru_part(t, gate, ext, hcar):
        xc = cw_ref[CONV_WIDTH - 1:CONV_WIDTH, :] * ext[CONV_HALO:, :] + cb_ref[...]
        for j in range(1, CONV_WIDTH):
            xc = xc + cw_ref[CONV_WIDTH - 1 - j:CONV_WIDTH - j, :] * pltpu.roll(ext, j, axis=0)[CONV_HALO:, :]
        rec = jax.nn.sigmoid(_dot(xc, wa_ref[...]) + ba_ref[...])
        inp = jax.nn.sigmoid(_dot(xc, wx_ref[...]) + bx_ref[...])
        log_a = -LRU_C * rec * _softplus(-lam_ref[...])
        a = jnp.exp(log_a)
        bx = jnp.sqrt(-jnp.tanh(log_a) * (a * a + 1.0)) * (inp * xc)
        a3 = a.reshape(ngrp, SUBLANES, w)
        b3 = bx.reshape(ngrp, SUBLANES, w)
        d = 1
        while d < SUBLANES:
            keep = sub_id >= d
            a_sh = jnp.where(keep, pltpu.roll(a3, d, axis=1), 1.0)
            b_sh = jnp.where(keep, pltpu.roll(b3, d, axis=1), 0.0)
            b3 = a3 * b_sh + b3
            a3 = a3 * a_sh
            d *= 2
        gate_act = jax.nn.gelu(gate)
        for j in range(ngrp):
            hj = a3[j] * hcar + b3[j]
            lo = t * sub + j * SUBLANES
            yd_o[lo:lo + SUBLANES, :] = hj * gate_act[j * SUBLANES:(j + 1) * SUBLANES, :]
            hcar = hj[SUBLANES - 1:SUBLANES, :]
        return hcar

    hprev = hprev_ref[...]
    hcar = hlru_ref[...]
    hc = project_rwkv(0)
    gate, ext = project_lru(0)
    for t in range(nsub):
        last = t + 1 == nsub
        hc_next = None if last else project_rwkv(t + 1)
        rwkv_part(t, hc, hprev)
        hprev = hc[sub - 1:sub, :]
        lru_next = (None, None) if last else project_lru(t + 1)
        hcar = lru_part(t, gate, ext, hcar)
        hc, (gate, ext) = hc_next, lru_next
    hprev_ref[...] = hprev
    hlru_ref[...] = hcar
    xr_ref[0:CONV_HALO, :] = xr_ref[tm:tm + CONV_HALO, :]


def _odd_pre(xf, w_in, mu, w0, w_up, a0, a_up, g_up, k_k, k_a, r_k, conv_w, conv_b, wa_bd, b_a, wx_bd, b_x,
             lam, ones_bd, e, nb, ns):
    m, d = xf.shape
    w = w0.shape[-1]
    full = lambda arr: pl.BlockSpec((None,) + arr.shape[1:], lambda b, s: (e,) + (0,) * (arr.ndim - 1))
    row = pl.BlockSpec((ROW_TILE, w), lambda b, s: (b * ns + s, 0))
    params = (w_in, mu, w0, w_up, a0, a_up, g_up, k_k, k_a, r_k, conv_w, conv_b, wa_bd, b_a, wx_bd, b_x, lam)
    return pl.pallas_call(
        _odd_pre_kernel,
        grid=(nb, ns),
        in_specs=[pl.BlockSpec((ROW_TILE, d), lambda b, s: (b * ns + s, 0))]
        + [full(p) for p in params]
        + [pl.BlockSpec(ones_bd.shape, lambda b, s: (0, 0))],
        out_specs=[row] * 9,
        out_shape=[jax.ShapeDtypeStruct((m, w), F32)] * 9,
        scratch_shapes=[pltpu.VMEM((1, mu.shape[-1]), F32),
                        pltpu.VMEM((CONV_HALO + ROW_TILE, w), F32),
                        pltpu.VMEM((1, w), F32)],
        compiler_params=_cparams(),
        name="odd_pre",
    )(xf, *params, ones_bd)


def _rwkv_kernel(r_ref, lw_ref, k_ref, v_ref, a_ref, b_ref, y_ref, t_ref, q_ref, g_ref, h_ref):
    s = pl.program_id(1)
    tm, w = r_ref.shape
    n = RWKV_HEAD_DIM
    lc = RWKV_CHUNK
    gw = MXU_WIDTH
    hpg = gw // n
    groups = range(w // gw)

    @pl.when(s == 0)
    def _():
        t_ref[...] = jnp.zeros(t_ref.shape, F32)

    row_t = lax.broadcasted_iota(jnp.int32, (lc, gw), 0)
    col = lax.broadcasted_iota(jnp.int32, (lc, gw), 1)
    col_s = col & (n - 1)
    col_h = col >> (n.bit_length() - 1)
    strict = row_t > col_s
    incl = row_t >= col_s
    eye_m = row_t == col_s
    eye = eye_m.astype(F32)
    merge_masks = [(row_t >> 1) == (col_s >> 1)]
    size = 2
    while size < lc:
        shift = size.bit_length()
        merge_masks.append(((row_t >> shift) == (col_s >> shift)) & ((row_t & size) != 0) & ((col_s & size) == 0))
        size *= 2
    blk_r = lax.broadcasted_iota(jnp.int32, (gw, gw), 0) >> (n.bit_length() - 1)
    blk_c = lax.broadcasted_iota(jnp.int32, (gw, gw), 1) >> (n.bit_length() - 1)
    bd_mask = (blk_r == blk_c).astype(BF16)
    rows = _row_ids(lc, w)

    def bd(x):
        xb = x.astype(BF16)
        return jnp.concatenate([xb] * hpg, axis=0) * bd_mask

    def mm(a, b_bf16, dn=_NN):
        return lax.dot_general(a.astype(BF16), b_bf16, dn, preferred_element_type=F32)

    def diag_blocks(p):
        out = p[(hpg - 1) * n:, :]
        for h in range(hpg - 1):
            out = jnp.where(col_h == h, p[h * n:(h + 1) * n, :], out)
        return out

    def state_free(i):
        at, rt, bt, kt, bh, kh, v, pend, where = [], [], [], [], [], [], [], [], []
        for ci in range(RWKV_INTERLEAVE):
            sl = pl.ds((i * RWKV_INTERLEAVE + ci) * lc, lc)
            lw = lw_ref[sl, :]
            cum = lw
            d = 1
            while d < lc:
                cum = cum + jnp.where(rows >= d, pltpu.roll(cum, d, axis=0), 0.0)
                d *= 2
            cl = cum[lc - 1:lc, :]
            einv = jnp.exp(-cum)
            eend = jnp.exp(cl - cum)
            a = a_ref[sl, :]
            b = b_ref[sl, :]
            k = k_ref[sl, :]
            full = (a * jnp.exp(cum - lw), r_ref[sl, :] * jnp.exp(cum), b * einv, k * einv, b * eend, k * eend,
                    v_ref[sl, :], jnp.exp(cl))
            for g in groups:
                for dst, t in zip((at, rt, bt, kt, bh, kh, v, pend), full):
                    dst.append(t[:, g * gw:(g + 1) * gw])
                where.append((sl, slice(g * gw, (g + 1) * gw)))
            yield
        units = range(len(where))
        lhs = [jnp.concatenate([at[u], rt[u]], axis=0).astype(BF16) for u in units]
        sb = [mm(lhs[u], bd(bt[u]), _NT) for u in units]
        yield
        sk = [mm(lhs[u], bd(kt[u]), _NT) for u in units]
        a_ab = [jnp.where(strict, t[:lc], 0.0) for t in sb]
        m_rb = [jnp.where(incl, t[lc:], 0.0) for t in sb]
        yield
        a_ak = [jnp.where(strict, t[:lc], 0.0) for t in sk]
        m_rk = [jnp.where(incl, t[lc:], 0.0) for t in sk]
        xm = [mm(jnp.concatenate([a_ak[u], m_rk[u]], axis=0), bd(v[u])) for u in units]
        x0 = [t[:lc] for t in xm]
        ykv = [t[lc:] for t in xm]
        yield
        inv = [eye + jnp.where(merge_masks[0], t, 0.0) for t in a_ab]
        for below in merge_masks[1:]:
            we = [mm(inv[u], bd(jnp.where(below, a_ab[u], 0.0))) for u in units]
            yield
            inv = [inv[u] + mm(we[u], bd(inv[u])) for u in units]
            yield
        ap = [mm(inv[u], bd(at[u])) for u in units]
        yield
        vp = [mm(inv[u], bd(x0[u])) for u in units]
        yield
        q = [rt[u] + mm(m_rb[u], bd(ap[u])) for u in units]
        for u in units:
            q_ref[where[u]] = q[u]
        yield
        yl = [mm(m_rb[u], bd(vp[u])) + ykv[u] for u in units]
        for u in units:
            y_ref[where[u]] = yl[u]
        yield
        gm = [diag_blocks(mm(bh[u], ap[u].astype(BF16), _TN))
              + jnp.where(eye_m, jnp.broadcast_to(pend[u], (lc, gw)), 0.0) for u in units]
        for u in units:
            g_ref[where[u]] = gm[u]
        yield
        hm = [diag_blocks(mm(jnp.concatenate([bh[u], kh[u]], axis=0),
                             jnp.concatenate([vp[u], v[u]], axis=0).astype(BF16), _TN)) for u in units]
        for u in units:
            h_ref[where[u]] = hm[u]

    def state_step(c):
        sl = pl.ds(c * lc, lc)
        fin = [mm(jnp.concatenate([q_ref[sl, g * gw:(g + 1) * gw], g_ref[sl, g * gw:(g + 1) * gw]], axis=0),
                  bd(t_ref[:, g * gw:(g + 1) * gw])) for g in groups]
        for g in groups:
            gs = slice(g * gw, (g + 1) * gw)
            y_ref[sl, gs] = y_ref[sl, gs] + fin[g][:lc]
            t_ref[:, gs] = fin[g][lc:] + h_ref[sl, gs]

    ngroups = tm // (lc * RWKV_INTERLEAVE)
    running, ready_chunks, slot, started = [], [], 0, 0
    while started < ngroups or running or ready_chunks:
        if started < ngroups and slot == started * RWKV_LAG:
            running.append((state_free(started), started))
            started += 1
        still = []
        for gen, gi in running:
            try:
                next(gen)
                still.append((gen, gi))
            except StopIteration:
                ready_chunks.extend(range(gi * RWKV_INTERLEAVE, (gi + 1) * RWKV_INTERLEAVE))
        running = still
        if ready_chunks:
            state_step(ready_chunks.pop(0))
        slot += 1


def _rwkv_scan(r, lw, k, v, a, b, nb, ns):
    m, w = r.shape
    assert RWKV_CHUNK == RWKV_HEAD_DIM and w % MXU_WIDTH == 0 and MXU_WIDTH % RWKV_HEAD_DIM == 0
    assert RWKV_HEAD_DIM & (RWKV_HEAD_DIM - 1) == 0
    row = pl.BlockSpec((ROW_TILE, w), lambda bb, s: (bb * ns + s, 0))
    return pl.pallas_call(
        _rwkv_kernel,
        grid=(nb, ns),
        in_specs=[row] * 6,
        out_specs=row,
        out_shape=jax.ShapeDtypeStruct((m, w), F32),
        scratch_shapes=[pltpu.VMEM((RWKV_HEAD_DIM, w), F32)]
        + [pltpu.VMEM((ROW_TILE, w), F32)] * 3,
        compiler_params=_cparams(),
        name="rwkv7",
    )(r, lw, k, v, a, b)


def _odd_post_kernel(x_ref, y_ref, g_ref, bonus_ref, yd_ref, gng_ref, gnb_ref, ones_ref, wout_ref,
                     lg_ref, lb_ref, o_ref):
    w = y_ref.shape[-1]
    sub = SUB_ROWS
    tiles = [slice(t * sub, (t + 1) * sub) for t in range(x_ref.shape[0] // sub)]
    ones_bd = ones_ref[...]
    inv_n = 1.0 / RWKV_HEAD_DIM
    y = [y_ref[rows, :] for rows in tiles]
    mu = [_head_sum(t, ones_bd, 2) * inv_n for t in y]
    yc = [a - b for a, b in zip(y, mu)]
    var = [_head_sum(t * t, ones_bd, 1) * inv_n for t in yc]
    out = []
    for t, rows in enumerate(tiles):
        yn = yc[t] * lax.rsqrt(var[t] + RWKV_GN_EPS) * gng_ref[...] + gnb_ref[...]
        y_rwkv = (yn + bonus_ref[rows, :]) * g_ref[rows, :]
        out.append(_dot(y_rwkv, wout_ref[0:w, :]) + _dot(yd_ref[rows, :], wout_ref[w:2 * w, :]))
    for t, rows in enumerate(tiles):
        o_ref[rows, :] = _layer_norm(DEEPNORM_ALPHA * x_ref[rows, :] + out[t], lg_ref[...], lb_ref[...])


def _odd_post(xf, y, g, bonus, yd, gn_g, gn_b, ones_bd, w_out, ln_g, ln_b, e, l, nb, ns):
    m, d = xf.shape
    w = y.shape[-1]
    row = pl.BlockSpec((WIDE_TILE, w), lambda b, s: (b * ns + s, 0))
    xrow = pl.BlockSpec((WIDE_TILE, d), lambda b, s: (b * ns + s, 0))
    return pl.pallas_call(
        _odd_post_kernel,
        grid=(nb, ns),
        in_specs=[xrow, row, row, row, row,
                  pl.BlockSpec((None, 1, w), lambda b, s: (e, 0, 0)),
                  pl.BlockSpec((None, 1, w), lambda b, s: (e, 0, 0)),
                  pl.BlockSpec(ones_bd.shape, lambda b, s: (0, 0)),
                  pl.BlockSpec((None, 2 * w, d), lambda b, s: (e, 0, 0)),
                  pl.BlockSpec((None, 1, d), lambda b, s: (4 * l + 1, 0, 0)),
                  pl.BlockSpec((None, 1, d), lambda b, s: (4 * l + 1, 0, 0))],
        out_specs=xrow,
        out_shape=jax.ShapeDtypeStruct((m, d), F32),
        compiler_params=_cparams(),
        name="odd_post",
    )(xf, y, g, bonus, yd, gn_g, gn_b, ones_bd, w_out, ln_g, ln_b)


def _kv_kernel(mem_ref, wkv_ref, kv_ref):
    kv_ref[...] = _dot(mem_ref[...], wkv_ref[...]).astype(BF16)


def _kv_proj(mem, w_kv):
    nb, mlen, d = mem.shape
    nl, _, n2 = w_kv.shape
    return pl.pallas_call(
        _kv_kernel,
        grid=(nl, nb),
        in_specs=[pl.BlockSpec((None, mlen, d), lambda l, b: (b, 0, 0)),
                  pl.BlockSpec((None, d, n2), lambda l, b: (l, 0, 0))],
        out_specs=pl.BlockSpec((None, None, mlen, n2), lambda l, b: (l, b, 0, 0)),
        out_shape=jax.ShapeDtypeStruct((nl, nb, mlen, n2), BF16),
        compiler_params=_cparams(),
        name="xattn_kv",
    )(mem, w_kv)


def _xattn_kernel(x_ref, wq_ref, kv_ref, wo_ref, g_ref, b_ref, o_ref):
    d = x_ref.shape[-1]
    hd = d // XATTN_HEADS
    sub = SUB_ROWS
    nsub = x_ref.shape[0] // sub
    tiles = [slice(t * sub, (t + 1) * sub) for t in range(nsub)]
    heads = XATTN_HEADS
    n_units = nsub * heads
    q, p, proj, outs = {}, {}, {}, [[] for _ in tiles]
    q[0] = _dot(x_ref[tiles[0], :], wq_ref[...])
    for i in range(n_units + XATTN_SKEW + 1):
        if i < n_units:
            t, h = divmod(i, heads)
            if h == 1 and t + 1 < nsub:
                q[t + 1] = _dot(x_ref[tiles[t + 1], :], wq_ref[...])
            qh = q[t][:, h * hd:(h + 1) * hd].astype(BF16)
            sc = lax.dot_general(qh, kv_ref[:, h * hd:(h + 1) * hd], _NT, preferred_element_type=F32)
            sc = sc * (hd ** -0.5)
            sc = sc - jnp.max(sc, axis=-1, keepdims=True)
            e = jnp.exp(sc)
            p[i] = (e / jnp.sum(e, axis=-1, keepdims=True)).astype(BF16)
        j = i - XATTN_SKEW
        if 0 <= j < n_units:
            t, h = divmod(j, heads)
            outs[t].append(jnp.dot(p.pop(j), kv_ref[:, d + h * hd:d + (h + 1) * hd], preferred_element_type=F32))
        j = i - XATTN_SKEW - 1
        if 0 <= j < n_units and j % heads == heads - 1:
            t = j // heads
            proj[t] = _dot(jnp.concatenate(outs[t], axis=-1), wo_ref[...])
    for t, rows in enumerate(tiles):
        o_ref[rows, :] = _layer_norm(DEEPNORM_ALPHA * x_ref[rows, :] + proj[t], g_ref[...], b_ref[...])


def _xattn_layer(xf, w_q, kv, w_o, ln_g, ln_b, l, nb, ns):
    m, d = xf.shape
    mlen = kv.shape[2]
    xrow = pl.BlockSpec((XATTN_TILE, d), lambda b, s: (b * ns + s, 0))
    return pl.pallas_call(
        _xattn_kernel,
        grid=(nb, ns),
        in_specs=[xrow,
                  pl.BlockSpec((None, d, d), lambda b, s: (l, 0, 0)),
                  pl.BlockSpec((None, None, mlen, 2 * d), lambda b, s: (l, b, 0, 0)),
                  pl.BlockSpec((None, d, d), lambda b, s: (l, 0, 0)),
                  pl.BlockSpec((None, 1, d), lambda b, s: (4 * l + 2, 0, 0)),
                  pl.BlockSpec((None, 1, d), lambda b, s: (4 * l + 2, 0, 0))],
        out_specs=xrow,
        out_shape=jax.ShapeDtypeStruct((m, d), F32),
        compiler_params=_cparams(),
        name="xattn",
    )(xf, w_q, kv, w_o, ln_g, ln_b)


def _block_diag(w):
    nblk, bi, bj = w.shape[-3:]
    eye = jnp.eye(nblk, dtype=w.dtype)
    out = w[..., :, :, None, :] * eye[:, None, :, None]
    return out.reshape(w.shape[:-3] + (nblk * bi, nblk * bj))


def kernel(x, mem, ffn1_w_in, ffn1_w_out, ffn2_w_in, ffn2_w_out, ln_g, ln_b, xattn_w_q, xattn_w_kv, xattn_w_o, even_w_in, even_w_out, pool_w, pool_scale, sgu_ln_g, sgu_ln_b, sgu_w, sgu_b, odd_w_in, odd_w_out, rwkv_mu, rwkv_w0, rwkv_w_up, rwkv_a0, rwkv_a_up, rwkv_g_up, rwkv_k_k, rwkv_k_a, rwkv_r_k, rwkv_gn_g, rwkv_gn_b, lru_conv_w, lru_conv_b, lru_w_a, lru_b_a, lru_w_x, lru_b_x, lru_lambda):
    nb, seq, d = x.shape
    assert seq % WIDE_TILE == 0 and seq % XATTN_TILE == 0 and WIDE_TILE % ROW_TILE == 0 and ROW_TILE % SUB_ROWS == 0
    assert SUB_ROWS % SGU_CHUNK == 0 and ROW_TILE % (RWKV_CHUNK * RWKV_INTERLEAVE) == 0
    ns = seq // ROW_TILE
    nsw = seq // WIDE_TILE
    depth = ffn1_w_in.shape[0]
    xf = x.reshape(nb * seq, d)

    bf = lambda t: t.astype(BF16)
    row3 = lambda t: t.reshape(t.shape[0], 1, -1)
    lg = ln_g.reshape(-1, 1, d)
    lb = ln_b.reshape(-1, 1, d)
    f1_in, f1_out, f2_in, f2_out = ffn1_w_in, ffn1_w_out, ffn2_w_in, ffn2_w_out
    wq, wo = bf(xattn_w_q), bf(xattn_w_o)
    kv = _kv_proj(mem, bf(xattn_w_kv))
    ev_in, ev_out = bf(even_w_in), bf(even_w_out)
    od_in, od_out = bf(odd_w_in), bf(odd_w_out)
    n_e, n_pool, pool_dim = pool_w.shape[:3]
    pw = bf(_block_diag(pool_w.reshape(n_e, n_pool // POOL_PACK, POOL_PACK, pool_dim, pool_dim)))
    sgu_bt = jnp.swapaxes(sgu_b, 1, 2)
    wa_bd, wx_bd = bf(_block_diag(lru_w_a)), bf(_block_diag(lru_w_x))
    w_rwkv = rwkv_w0.shape[-1]
    ones_bd = _block_diag(jnp.ones((w_rwkv // RWKV_HEAD_DIM, RWKV_HEAD_DIM, RWKV_HEAD_DIM), BF16))
    r_k = rwkv_r_k.reshape(rwkv_r_k.shape[0], 1, -1)

    for l in range(depth):
        xf = _ffn_layer(xf, f1_in, f1_out, lg, lb, l, 0)
        e = l // 2
        if l % 2 == 0:
            xf = _even_layer(xf, ev_in, ev_out, pw, row3(pool_scale), row3(sgu_ln_g), row3(sgu_ln_b),
                             sgu_w, sgu_bt, lg, lb, e, l, nb, nsw)
        else:
            r, lw, k, v, a, b, g, bonus, yd = _odd_pre(
                xf, od_in, row3(rwkv_mu), row3(rwkv_w0), bf(rwkv_w_up), row3(rwkv_a0), bf(rwkv_a_up),
                bf(rwkv_g_up), row3(rwkv_k_k), row3(rwkv_k_a), r_k, lru_conv_w, row3(lru_conv_b),
                wa_bd, row3(lru_b_a), wx_bd, row3(lru_b_x), row3(lru_lambda), ones_bd, e, nb, ns)
            y = _rwkv_scan(r, lw, k, v, a, b, nb, ns)
            xf = _odd_post(xf, y, g, bonus, yd, row3(rwkv_gn_g), row3(rwkv_gn_b), ones_bd, od_out,
                           lg, lb, e, l, nb, nsw)
        xf = _xattn_layer(xf, wq, kv, wo, lg, lb, l, nb, seq // XATTN_TILE)
        xf = _ffn_layer(xf, f2_in, f2_out, lg, lb, l, 3)
    return xf.reshape(nb, seq, d)
```

```python
import functools

import jax
import jax.numpy as jnp
from jax import lax
from jax.experimental import pallas as pl
from jax.experimental.pallas import tpu as pltpu

F32 = jnp.float32
BF16 = jnp.bfloat16

DEPTH = 4
LN_EPS = 1e-5
DEEPNORM_ALPHA = (2 * DEPTH) ** 0.25
MACARON_WEIGHT = 0.5
POOL_WINDOWS = (2, 4, 8, 16)
POOL_HALO = 16
POOL_PACK = 2
SGU_CHUNK = 128
XATTN_HEADS = 4
XATTN_SKEW = 2
RWKV_HEAD_DIM = 64
RWKV_CHUNK = 64
RWKV_GN_EPS = 64e-5
MXU_WIDTH = 256
RWKV_INTERLEAVE = 2
RWKV_LAG = 4
LRU_C = 8.0
CONV_WIDTH = 4
SUBLANES = 8
CONV_HALO = SUBLANES

ROW_TILE = 512
FFN_TILE = 1024
WIDE_TILE = 1024
XATTN_TILE = 1024
SUB_ROWS = 256
FFN_CHUNK = 256
FFN_LOAD_CHUNK = 512
FFN_LOAD_SLOTS = 3
VMEM_LIMIT = 56 * 1024 * 1024


def _cparams(grid_rank=2):
    return pltpu.CompilerParams(dimension_semantics=("arbitrary",) * grid_rank,
                                vmem_limit_bytes=VMEM_LIMIT)


def _dot(a, b):
    return jnp.dot(a.astype(BF16), b.astype(BF16), preferred_element_type=F32)


_NN = (((1,), (0,)), ((), ()))
_NT = (((1,), (1,)), ((), ()))
_TN = (((0,), (0,)), ((), ()))


def _head_sum(x, ones_bd, pieces):
    total = None
    rest = x
    for i in range(pieces):
        p = rest.astype(BF16)
        if i + 1 < pieces:
            rest = rest - p.astype(F32)
        term = jnp.dot(p, ones_bd, preferred_element_type=F32)
        total = term if total is None else total + term
    return total


def _layer_norm(y, g, b, eps=LN_EPS):
    mu = jnp.mean(y, axis=-1, keepdims=True)
    yc = y - mu
    var = jnp.mean(yc * yc, axis=-1, keepdims=True)
    return yc * lax.rsqrt(var + eps) * g + b


def _softplus(t):
    return jnp.maximum(t, 0.0) + jnp.log(1.0 + jnp.exp(-jnp.abs(t)))


def _row_ids(rows, cols):
    return lax.broadcasted_iota(jnp.int32, (rows, cols), 0)


def _load_cast(src_hbm, dst_ref, stage_ref, sem, axis, chunk):
    n = src_hbm.shape[axis] // chunk
    slots = FFN_LOAD_SLOTS

    def piece(c):
        sl = slice(c * chunk, (c + 1) * chunk)
        return (slice(None), sl) if axis == 1 else (sl, slice(None))

    def copy(c):
        return pltpu.make_async_copy(src_hbm.at[piece(c)], stage_ref.at[c % slots], sem.at[c % slots])

    for c in range(min(slots - 1, n)):
        copy(c).start()
    for c in range(n):
        if c + slots - 1 < n:
            copy(c + slots - 1).start()
        copy(c).wait()
        dst_ref[piece(c)] = stage_ref[c % slots].astype(BF16)


def _ffn_kernel(x_ref, win_hbm, wout_hbm, g_ref, b_ref, o_ref, h_ref, win_ref, wout_ref,
                stage_in, stage_out, sem_in, sem_out, *, d_ff, layer):
    @pl.when(pl.program_id(0) == 0)
    def _():
        _load_cast(win_hbm.at[layer], win_ref, stage_in, sem_in, 1, FFN_LOAD_CHUNK)
        _load_cast(wout_hbm.at[layer], wout_ref, stage_out, sem_out, 0, FFN_LOAD_CHUNK // 2)

    sub = SUB_ROWS
    nsub = x_ref.shape[0] // sub
    accs = []
    for t in range(nsub):
        rows = slice(t * sub, (t + 1) * sub)
        xb = x_ref[rows, :].astype(BF16)
        for c in range(d_ff // FFN_CHUNK):
            lo = c * FFN_CHUNK
            gate = jnp.dot(xb, win_ref[:, lo:lo + FFN_CHUNK], preferred_element_type=F32)
            up = jnp.dot(xb, win_ref[:, d_ff + lo:d_ff + lo + FFN_CHUNK], preferred_element_type=F32)
            h_ref[rows, lo:lo + FFN_CHUNK] = (gate * jax.nn.sigmoid(gate) * up).astype(BF16)
        accs.append(jnp.dot(h_ref[rows, :], wout_ref[...], preferred_element_type=F32))
    for t in range(nsub):
        rows = slice(t * sub, (t + 1) * sub)
        y = DEEPNORM_ALPHA * x_ref[rows, :] + MACARON_WEIGHT * accs[t]
        o_ref[rows, :] = _layer_norm(y, g_ref[...], b_ref[...])


def _ffn_layer(xf, w_in, w_out, ln_g, ln_b, l, j):
    m, d = xf.shape
    d_ff = w_out.shape[1]
    assert m % FFN_TILE == 0 and (2 * d_ff) % FFN_LOAD_CHUNK == 0 and d_ff % (FFN_LOAD_CHUNK // 2) == 0
    assert w_in.dtype == F32 and w_out.dtype == F32
    return pl.pallas_call(
        functools.partial(_ffn_kernel, d_ff=d_ff, layer=l),
        grid=(m // FFN_TILE,),
        in_specs=[
            pl.BlockSpec((FFN_TILE, d), lambda i: (i, 0)),
            pl.BlockSpec(memory_space=pl.ANY),
            pl.BlockSpec(memory_space=pl.ANY),
            pl.BlockSpec((None, 1, d), lambda i: (4 * l + j, 0, 0)),
            pl.BlockSpec((None, 1, d), lambda i: (4 * l + j, 0, 0)),
        ],
        out_specs=pl.BlockSpec((FFN_TILE, d), lambda i: (i, 0)),
        out_shape=jax.ShapeDtypeStruct((m, d), F32),
        scratch_shapes=[pltpu.VMEM((FFN_TILE, d_ff), BF16),
                        pltpu.VMEM((d, 2 * d_ff), BF16),
                        pltpu.VMEM((d_ff, d), BF16),
                        pltpu.VMEM((FFN_LOAD_SLOTS, d, FFN_LOAD_CHUNK), F32),
                        pltpu.VMEM((FFN_LOAD_SLOTS, FFN_LOAD_CHUNK // 2, d), F32),
                        pltpu.SemaphoreType.DMA((FFN_LOAD_SLOTS,)),
                        pltpu.SemaphoreType.DMA((FFN_LOAD_SLOTS,))],
        compiler_params=_cparams(1),
        name="ffn",
    )(xf, w_in, w_out, ln_g, ln_b)


def _even_kernel(x_ref, win_ref, wout_ref, poolw_ref, pscale_ref, sg_ref, sb_ref, sw_ref, sbt_ref,
                 g_ref, b_ref, o_ref, ext_ref):
    s = pl.program_id(1)
    tm = x_ref.shape[0]
    width = pscale_ref.shape[-1]
    pw = width // len(POOL_WINDOWS)
    heads = sw_ref.shape[0]
    hw = width // heads
    sub = SUB_ROWS
    nchunk = sub // SGU_CHUNK
    subs = range(tm // sub)
    tiles = [slice(t * sub, (t + 1) * sub) for t in subs]

    @pl.when(s == 0)
    def _():
        ext_ref[0:POOL_HALO, :] = jnp.zeros((POOL_HALO, width), F32)

    h = [_dot(x_ref[rows, :], win_ref[...]) for rows in tiles]
    for t in subs:
        ext_ref[POOL_HALO + t * sub:POOL_HALO + (t + 1) * sub, :] = h[t][:, :width]

    ti = lax.broadcasted_iota(jnp.int32, (SGU_CHUNK, SGU_CHUNK), 0)
    si = lax.broadcasted_iota(jnp.int32, (SGU_CHUNK, SGU_CHUNK), 1)
    w_sgu = [jnp.where(ti >= si, sw_ref[hd], 0.0).astype(BF16) for hd in range(heads)]

    ya, yb = [], []
    for t in subs:
        ext = ext_ref[t * sub:(t + 1) * sub + POOL_HALO, :]
        pos = (s * tm + t * sub + 1 + _row_ids(sub, 1)).astype(F32)
        pooled = []
        for g, win in enumerate(POOL_WINDOWS):
            acc = ext[:, g * pw:(g + 1) * pw]
            d = 1
            while d < win:
                acc = acc + pltpu.roll(acc, d, axis=0)
                d *= 2
            mean = acc[POOL_HALO:, :] / jnp.minimum(pos, float(win))
            pooled.append(mean - h[t][:, g * pw:(g + 1) * pw])
        yat = [_dot(jnp.concatenate(pooled[p * POOL_PACK:(p + 1) * POOL_PACK], axis=-1), poolw_ref[p])
               for p in range(len(POOL_WINDOWS) // POOL_PACK)]
        ya.append(jnp.concatenate(yat, axis=-1) * pscale_ref[...])

        u = jax.nn.gelu(h[t][:, width:2 * width])
        v = jax.nn.gelu(h[t][:, 2 * width:3 * width])
        vb = _layer_norm(v, sg_ref[...], sb_ref[...]).astype(BF16)
        mixed = []
        for hd in range(heads):
            bias = sbt_ref[:, hd:hd + 1]
            vh = jnp.concatenate([vb[c * SGU_CHUNK:(c + 1) * SGU_CHUNK, hd * hw:(hd + 1) * hw]
                                  for c in range(nchunk)], axis=1)
            mixed.append(jnp.dot(w_sgu[hd], vh, preferred_element_type=F32) + bias)
        yb.append(u * jnp.concatenate(
            [jnp.concatenate([mixed[hd][:, c * hw:(c + 1) * hw] for hd in range(heads)], axis=1)
             for c in range(nchunk)], axis=0))

    out = [_dot(ya[t], wout_ref[0:width, :]) + _dot(yb[t], wout_ref[width:2 * width, :]) for t in subs]
    for t, rows in enumerate(tiles):
        o_ref[rows, :] = _layer_norm(DEEPNORM_ALPHA * x_ref[rows, :] + out[t], g_ref[...], b_ref[...])
    ext_ref[0:POOL_HALO, :] = ext_ref[tm:tm + POOL_HALO, :]


def _even_layer(xf, w_in, w_out, pool_w, pool_scale, sgu_g, sgu_b, sgu_w, sgu_bt, ln_g, ln_b, e, l, nb, ns):
    m, d = xf.shape
    n_in = w_in.shape[-1]
    width = pool_scale.shape[-1]
    full = lambda shape: pl.BlockSpec((None,) + shape, lambda b, s: (e,) + (0,) * len(shape))
    return pl.pallas_call(
        _even_kernel,
        grid=(nb, ns),
        in_specs=[
            pl.BlockSpec((WIDE_TILE, d), lambda b, s: (b * ns + s, 0)),
            full((d, n_in)),
            full((2 * width, d)),
            full(pool_w.shape[1:]),
            full((1, width)),
            full((1, width)),
            full((1, width)),
            full(sgu_w.shape[1:]),
            full(sgu_bt.shape[1:]),
            pl.BlockSpec((None, 1, d), lambda b, s: (4 * l + 1, 0, 0)),
            pl.BlockSpec((None, 1, d), lambda b, s: (4 * l + 1, 0, 0)),
        ],
        out_specs=pl.BlockSpec((WIDE_TILE, d), lambda b, s: (b * ns + s, 0)),
        out_shape=jax.ShapeDtypeStruct((m, d), F32),
        scratch_shapes=[pltpu.VMEM((POOL_HALO + WIDE_TILE, width), F32)],
        compiler_params=_cparams(),
        name="even_mixer",
    )(xf, w_in, w_out, pool_w, pool_scale, sgu_g, sgu_b, sgu_w, sgu_bt, ln_g, ln_b)


def _odd_pre_kernel(x_ref, win_ref, mu_ref, w0_ref, wup_ref, a0_ref, aup_ref, gup_ref, kk_ref, ka_ref,
                    rk_ref, cw_ref, cb_ref, wa_ref, ba_ref, wx_ref, bx_ref, lam_ref, ones_ref,
                    r_o, lw_o, k_o, v_o, a_o, b_o, g_o, bonus_o, yd_o,
                    hprev_ref, xr_ref, hlru_ref):
    s = pl.program_id(1)
    tm = x_ref.shape[0]
    w = w0_ref.shape[-1]
    n_c = mu_ref.shape[-1]
    sub = SUB_ROWS
    nsub = tm // sub
    tiles = [slice(t * sub, (t + 1) * sub) for t in range(nsub)]
    dr = wup_ref.shape[0]
    ar = aup_ref.shape[0]
    ones_bd = ones_ref[...]

    @pl.when(s == 0)
    def _():
        hprev_ref[...] = jnp.zeros(hprev_ref.shape, F32)
        xr_ref[0:CONV_HALO, :] = jnp.zeros((CONV_HALO, w), F32)
        hlru_ref[...] = jnp.zeros(hlru_ref.shape, F32)

    first = _row_ids(sub, 1) == 0
    ngrp = sub // SUBLANES
    sub_id = lax.broadcasted_iota(jnp.int32, (ngrp, SUBLANES, w), 1)

    def project_rwkv(t):
        return _dot(x_ref[tiles[t], :], win_ref[:, :n_c])

    def project_lru(t):
        hd = _dot(x_ref[tiles[t], :], win_ref[:, n_c:])
        xr_ref[CONV_HALO + t * sub:CONV_HALO + (t + 1) * sub, :] = hd[:, w:]
        return hd[:, :w], xr_ref[t * sub:(t + 1) * sub + CONV_HALO, :]

    def rwkv_part(t, hc, hprev):
        rows = tiles[t]
        prev = jnp.where(first, hprev, pltpu.roll(hc, 1, axis=0))
        hc = hc + mu_ref[...] * (prev - hc)
        r = hc[:, 0:w]
        k = hc[:, w:2 * w]
        v = hc[:, 2 * w:3 * w]
        rest = hc[:, 3 * w:]
        wd = rest[:, 0:dr]
        ad = rest[:, dr:dr + ar]
        gd = rest[:, dr + ar:]

        z = w0_ref[...] + _dot(jnp.tanh(wd), wup_ref[...])
        logw = -_softplus(-z) - 0.5
        lw_o[rows, :] = -jnp.exp(logw)
        asig = jax.nn.sigmoid(a0_ref[...] + _dot(ad, aup_ref[...]))
        g_o[rows, :] = _dot(jax.nn.sigmoid(gd), gup_ref[...])
        kk = k * kk_ref[...]
        kk = kk * lax.rsqrt(jnp.maximum(_head_sum(kk * kk, ones_bd, 1), 1e-24))
        k2 = k * (1.0 + (asig - 1.0) * ka_ref[...])
        r_o[rows, :] = r
        k_o[rows, :] = k2
        v_o[rows, :] = v
        a_o[rows, :] = -kk
        b_o[rows, :] = kk * asig
        bonus_o[rows, :] = _head_sum(r * k2 * rk_ref[...], ones_bd, 2) * v

    def lru_part(t, gate, ext, hcar):
        xc = cw_ref[CONV_WIDTH - 1:CONV_WIDTH, :] * ext[CONV_HALO:, :] + cb_ref[...]
        for j in range(1, CONV_WIDTH):
            xc = xc + cw_ref[CONV_WIDTH - 1 - j:CONV_WIDTH - j, :] * pltpu.roll(ext, j, axis=0)[CONV_HALO:, :]
        rec = jax.nn.sigmoid(_dot(xc, wa_ref[...]) + ba_ref[...])
        inp = jax.nn.sigmoid(_dot(xc, wx_ref[...]) + bx_ref[...])
        log_a = -LRU_C * rec * _softplus(-lam_ref[...])
        a = jnp.exp(log_a)
        bx = jnp.sqrt(-jnp.tanh(log_a) * (a * a + 1.0)) * (inp * xc)
        a3 = a.reshape(ngrp, SUBLANES, w)
        b3 = bx.reshape(ngrp, SUBLANES, w)
        d = 1
        while d < SUBLANES:
            keep = sub_id >= d
            a_sh = jnp.where(keep, pltpu.roll(a3, d, axis=1), 1.0)
            b_sh = jnp.where(keep, pltpu.roll(b3, d, axis=1), 0.0)
            b3 = a3 * b_sh + b3
            a3 = a3 * a_sh
            d *= 2
        gate_act = jax.nn.gelu(gate)
        for j in range(ngrp):
            hj = a3[j] * hcar + b3[j]
            lo = t * sub + j * SUBLANES
            yd_o[lo:lo + SUBLANES, :] = hj * gate_act[j * SUBLANES:(j + 1) * SUBLANES, :]
            hcar = hj[SUBLANES - 1:SUBLANES, :]
        return hcar

    hprev = hprev_ref[...]
    hcar = hlru_ref[...]
    hc = project_rwkv(0)
    gate, ext = project_lru(0)
    for t in range(nsub):
        last = t + 1 == nsub
        hc_next = None if last else project_rwkv(t + 1)
        rwkv_part(t, hc, hprev)
        hprev = hc[sub - 1:sub, :]
        lru_next = (None, None) if last else project_lru(t + 1)
        hcar = lru_part(t, gate, ext, hcar)
        hc, (gate, ext) = hc_next, lru_next
    hprev_ref[...] = hprev
    hlru_ref[...] = hcar
    xr_ref[0:CONV_HALO, :] = xr_ref[tm:tm + CONV_HALO, :]


def _odd_pre(xf, w_in, mu, w0, w_up, a0, a_up, g_up, k_k, k_a, r_k, conv_w, conv_b, wa_bd, b_a, wx_bd, b_x,
             lam, ones_bd, e, nb, ns):
    m, d = xf.shape
    w = w0.shape[-1]
    full = lambda arr: pl.BlockSpec((None,) + arr.shape[1:], lambda b, s: (e,) + (0,) * (arr.ndim - 1))
    row = pl.BlockSpec((ROW_TILE, w), lambda b, s: (b * ns + s, 0))
    params = (w_in, mu, w0, w_up, a0, a_up, g_up, k_k, k_a, r_k, conv_w, conv_b, wa_bd, b_a, wx_bd, b_x, lam)
    return pl.pallas_call(
        _odd_pre_kernel,
        grid=(nb, ns),
        in_specs=[pl.BlockSpec((ROW_TILE, d), lambda b, s: (b * ns + s, 0))]
        + [full(p) for p in params]
        + [pl.BlockSpec(ones_bd.shape, lambda b, s: (0, 0))],
        out_specs=[row] * 9,
        out_shape=[jax.ShapeDtypeStruct((m, w), F32)] * 9,
        scratch_shapes=[pltpu.VMEM((1, mu.shape[-1]), F32),
                        pltpu.VMEM((CONV_HALO + ROW_TILE, w), F32),
                        pltpu.VMEM((1, w), F32)],
        compiler_params=_cparams(),
        name="odd_pre",
    )(xf, *params, ones_bd)


def _rwkv_kernel(r_ref, lw_ref, k_ref, v_ref, a_ref, b_ref, y_ref, t_ref, q_ref, g_ref, h_ref):
    s = pl.program_id(1)
    tm, w = r_ref.shape
    n = RWKV_HEAD_DIM
    lc = RWKV_CHUNK
    gw = MXU_WIDTH
    hpg = gw // n
    groups = range(w // gw)

    @pl.when(s == 0)
    def _():
        t_ref[...] = jnp.zeros(t_ref.shape, F32)

    row_t = lax.broadcasted_iota(jnp.int32, (lc, gw), 0)
    col = lax.broadcasted_iota(jnp.int32, (lc, gw), 1)
    col_s = col & (n - 1)
    col_h = col >> (n.bit_length() - 1)
    strict = row_t > col_s
    incl = row_t >= col_s
    eye_m = row_t == col_s
    eye = eye_m.astype(F32)
    merge_masks = [(row_t >> 1) == (col_s >> 1)]
    size = 2
    while size < lc:
        shift = size.bit_length()
        merge_masks.append(((row_t >> shift) == (col_s >> shift)) & ((row_t & size) != 0) & ((col_s & size) == 0))
        size *= 2
    blk_r = lax.broadcasted_iota(jnp.int32, (gw, gw), 0) >> (n.bit_length() - 1)
    blk_c = lax.broadcasted_iota(jnp.int32, (gw, gw), 1) >> (n.bit_length() - 1)
    bd_mask = (blk_r == blk_c).astype(BF16)
    rows = _row_ids(lc, w)

    def bd(x):
        xb = x.astype(BF16)
        return jnp.concatenate([xb] * hpg, axis=0) * bd_mask

    def mm(a, b_bf16, dn=_NN):
        return lax.dot_general(a.astype(BF16), b_bf16, dn, preferred_element_type=F32)

    def diag_blocks(p):
        out = p[(hpg - 1) * n:, :]
        for h in range(hpg - 1):
            out = jnp.where(col_h == h, p[h * n:(h + 1) * n, :], out)
        return out

    def state_free(i):
        at, rt, bt, kt, bh, kh, v, pend, where = [], [], [], [], [], [], [], [], []
        for ci in range(RWKV_INTERLEAVE):
            sl = pl.ds((i * RWKV_INTERLEAVE + ci) * lc, lc)
            lw = lw_ref[sl, :]
            cum = lw
            d = 1
            while d < lc:
                cum = cum + jnp.where(rows >= d, pltpu.roll(cum, d, axis=0), 0.0)
                d *= 2
            cl = cum[lc - 1:lc, :]
            einv = jnp.exp(-cum)
            eend = jnp.exp(cl - cum)
            a = a_ref[sl, :]
            b = b_ref[sl, :]
            k = k_ref[sl, :]
            full = (a * jnp.exp(cum - lw), r_ref[sl, :] * jnp.exp(cum), b * einv, k * einv, b * eend, k * eend,
                    v_ref[sl, :], jnp.exp(cl))
            for g in groups:
                for dst, t in zip((at, rt, bt, kt, bh, kh, v, pend), full):
                    dst.append(t[:, g * gw:(g + 1) * gw])
                where.append((sl, slice(g * gw, (g + 1) * gw)))
            yield
        units = range(len(where))
        lhs = [jnp.concatenate([at[u], rt[u]], axis=0).astype(BF16) for u in units]
        sb = [mm(lhs[u], bd(bt[u]), _NT) for u in units]
        yield
        sk = [mm(lhs[u], bd(kt[u]), _NT) for u in units]
        a_ab = [jnp.where(strict, t[:lc], 0.0) for t in sb]
        m_rb = [jnp.where(incl, t[lc:], 0.0) for t in sb]
        yield
        a_ak = [jnp.where(strict, t[:lc], 0.0) for t in sk]
        m_rk = [jnp.where(incl, t[lc:], 0.0) for t in sk]
        xm = [mm(jnp.concatenate([a_ak[u], m_rk[u]], axis=0), bd(v[u])) for u in units]
        x0 = [t[:lc] for t in xm]
        ykv = [t[lc:] for t in xm]
        yield
        inv = [eye + jnp.where(merge_masks[0], t, 0.0) for t in a_ab]
        for below in merge_masks[1:]:
            we = [mm(inv[u], bd(jnp.where(below, a_ab[u], 0.0))) for u in units]
            yield
            inv = [inv[u] + mm(we[u], bd(inv[u])) for u in units]
            yield
        ap = [mm(inv[u], bd(at[u])) for u in units]
        yield
        vp = [mm(inv[u], bd(x0[u])) for u in units]
        yield
        q = [rt[u] + mm(m_rb[u], bd(ap[u])) for u in units]
        for u in units:
            q_ref[where[u]] = q[u]
        yield
        yl = [mm(m_rb[u], bd(vp[u])) + ykv[u] for u in units]
        for u in units:
            y_ref[where[u]] = yl[u]
        yield
        gm = [diag_blocks(mm(bh[u], ap[u].astype(BF16), _TN))
              + jnp.where(eye_m, jnp.broadcast_to(pend[u], (lc, gw)), 0.0) for u in units]
        for u in units:
            g_ref[where[u]] = gm[u]
        yield
        hm = [diag_blocks(mm(jnp.concatenate([bh[u], kh[u]], axis=0),
                             jnp.concatenate([vp[u], v[u]], axis=0).astype(BF16), _TN)) for u in units]
        for u in units:
            h_ref[where[u]] = hm[u]

    def state_step(c):
        sl = pl.ds(c * lc, lc)
        fin = [mm(jnp.concatenate([q_ref[sl, g * gw:(g + 1) * gw], g_ref[sl, g * gw:(g + 1) * gw]], axis=0),
                  bd(t_ref[:, g * gw:(g + 1) * gw])) for g in groups]
        for g in groups:
            gs = slice(g * gw, (g + 1) * gw)
            y_ref[sl, gs] = y_ref[sl, gs] + fin[g][:lc]
            t_ref[:, gs] = fin[g][lc:] + h_ref[sl, gs]

    ngroups = tm // (lc * RWKV_INTERLEAVE)
    running, ready_chunks, slot, started = [], [], 0, 0
    while started < ngroups or running or ready_chunks:
        if started < ngroups and slot == started * RWKV_LAG:
            running.append((state_free(started), started))
            started += 1
        still = []
        for gen, gi in running:
            try:
                next(gen)
                still.append((gen, gi))
            except StopIteration:
                ready_chunks.extend(range(gi * RWKV_INTERLEAVE, (gi + 1) * RWKV_INTERLEAVE))
        running = still
        if ready_chunks:
            state_step(ready_chunks.pop(0))
        slot += 1


def _rwkv_scan(r, lw, k, v, a, b, nb, ns):
    m, w = r.shape
    assert RWKV_CHUNK == RWKV_HEAD_DIM and w % MXU_WIDTH == 0 and MXU_WIDTH % RWKV_HEAD_DIM == 0
    assert RWKV_HEAD_DIM & (RWKV_HEAD_DIM - 1) == 0
    row = pl.BlockSpec((ROW_TILE, w), lambda bb, s: (bb * ns + s, 0))
    return pl.pallas_call(
        _rwkv_kernel,
        grid=(nb, ns),
        in_specs=[row] * 6,
        out_specs=row,
        out_shape=jax.ShapeDtypeStruct((m, w), F32),
        scratch_shapes=[pltpu.VMEM((RWKV_HEAD_DIM, w), F32)]
        + [pltpu.VMEM((ROW_TILE, w), F32)] * 3,
        compiler_params=_cparams(),
        name="rwkv7",
    )(r, lw, k, v, a, b)


def _odd_post_kernel(x_ref, y_ref, g_ref, bonus_ref, yd_ref, gng_ref, gnb_ref, ones_ref, wout_ref,
                     lg_ref, lb_ref, o_ref):
    w = y_ref.shape[-1]
    sub = SUB_ROWS
    tiles = [slice(t * sub, (t + 1) * sub) for t in range(x_ref.shape[0] // sub)]
    ones_bd = ones_ref[...]
    inv_n = 1.0 / RWKV_HEAD_DIM
    y = [y_ref[rows, :] for rows in tiles]
    mu = [_head_sum(t, ones_bd, 2) * inv_n for t in y]
    yc = [a - b for a, b in zip(y, mu)]
    var = [_head_sum(t * t, ones_bd, 1) * inv_n for t in yc]
    out = []
    for t, rows in enumerate(tiles):
        yn = yc[t] * lax.rsqrt(var[t] + RWKV_GN_EPS) * gng_ref[...] + gnb_ref[...]
        y_rwkv = (yn + bonus_ref[rows, :]) * g_ref[rows, :]
        out.append(_dot(y_rwkv, wout_ref[0:w, :]) + _dot(yd_ref[rows, :], wout_ref[w:2 * w, :]))
    for t, rows in enumerate(tiles):
        o_ref[rows, :] = _layer_norm(DEEPNORM_ALPHA * x_ref[rows, :] + out[t], lg_ref[...], lb_ref[...])


def _odd_post(xf, y, g, bonus, yd, gn_g, gn_b, ones_bd, w_out, ln_g, ln_b, e, l, nb, ns):
    m, d = xf.shape
    w = y.shape[-1]
    row = pl.BlockSpec((WIDE_TILE, w), lambda b, s: (b * ns + s, 0))
    xrow = pl.BlockSpec((WIDE_TILE, d), lambda b, s: (b * ns + s, 0))
    return pl.pallas_call(
        _odd_post_kernel,
        grid=(nb, ns),
        in_specs=[xrow, row, row, row, row,
                  pl.BlockSpec((None, 1, w), lambda b, s: (e, 0, 0)),
                  pl.BlockSpec((None, 1, w), lambda b, s: (e, 0, 0)),
                  pl.BlockSpec(ones_bd.shape, lambda b, s: (0, 0)),
                  pl.BlockSpec((None, 2 * w, d), lambda b, s: (e, 0, 0)),
                  pl.BlockSpec((None, 1, d), lambda b, s: (4 * l + 1, 0, 0)),
                  pl.BlockSpec((None, 1, d), lambda b, s: (4 * l + 1, 0, 0))],
        out_specs=xrow,
        out_shape=jax.ShapeDtypeStruct((m, d), F32),
        compiler_params=_cparams(),
        name="odd_post",
    )(xf, y, g, bonus, yd, gn_g, gn_b, ones_bd, w_out, ln_g, ln_b)


def _kv_kernel(mem_ref, wkv_ref, kv_ref):
    kv_ref[...] = _dot(mem_ref[...], wkv_ref[...]).astype(BF16)


def _kv_proj(mem, w_kv):
    nb, mlen, d = mem.shape
    nl, _, n2 = w_kv.shape
    return pl.pallas_call(
        _kv_kernel,
        grid=(nl, nb),
        in_specs=[pl.BlockSpec((None, mlen, d), lambda l, b: (b, 0, 0)),
                  pl.BlockSpec((None, d, n2), lambda l, b: (l, 0, 0))],
        out_specs=pl.BlockSpec((None, None, mlen, n2), lambda l, b: (l, b, 0, 0)),
        out_shape=jax.ShapeDtypeStruct((nl, nb, mlen, n2), BF16),
        compiler_params=_cparams(),
        name="xattn_kv",
    )(mem, w_kv)


def _xattn_kernel(x_ref, wq_ref, kv_ref, wo_ref, g_ref, b_ref, o_ref):
    d = x_ref.shape[-1]
    hd = d // XATTN_HEADS
    sub = SUB_ROWS
    nsub = x_ref.shape[0] // sub
    tiles = [slice(t * sub, (t + 1) * sub) for t in range(nsub)]
    heads = XATTN_HEADS
    n_units = nsub * heads
    q, p, proj, outs = {}, {}, {}, [[] for _ in tiles]
    q[0] = _dot(x_ref[tiles[0], :], wq_ref[...])
    for i in range(n_units + XATTN_SKEW + 1):
        if i < n_units:
            t, h = divmod(i, heads)
            if h == 1 and t + 1 < nsub:
                q[t + 1] = _dot(x_ref[tiles[t + 1], :], wq_ref[...])
            qh = q[t][:, h * hd:(h + 1) * hd].astype(BF16)
            sc = lax.dot_general(qh, kv_ref[:, h * hd:(h + 1) * hd], _NT, preferred_element_type=F32)
            sc = sc * (hd ** -0.5)
            sc = sc - jnp.max(sc, axis=-1, keepdims=True)
            e = jnp.exp(sc)
            p[i] = (e / jnp.sum(e, axis=-1, keepdims=True)).astype(BF16)
        j = i - XATTN_SKEW
        if 0 <= j < n_units:
            t, h = divmod(j, heads)
            outs[t].append(jnp.dot(p.pop(j), kv_ref[:, d + h * hd:d + (h + 1) * hd], preferred_element_type=F32))
        j = i - XATTN_SKEW - 1
        if 0 <= j < n_units and j % heads == heads - 1:
            t = j // heads
            proj[t] = _dot(jnp.concatenate(outs[t], axis=-1), wo_ref[...])
    for t, rows in enumerate(tiles):
        o_ref[rows, :] = _layer_norm(DEEPNORM_ALPHA * x_ref[rows, :] + proj[t], g_ref[...], b_ref[...])


def _xattn_layer(xf, w_q, kv, w_o, ln_g, ln_b, l, nb, ns):
    m, d = xf.shape
    mlen = kv.shape[2]
    xrow = pl.BlockSpec((XATTN_TILE, d), lambda b, s: (b * ns + s, 0))
    return pl.pallas_call(
        _xattn_kernel,
        grid=(nb, ns),
        in_specs=[xrow,
                  pl.BlockSpec((None, d, d), lambda b, s: (l, 0, 0)),
                  pl.BlockSpec((None, None, mlen, 2 * d), lambda b, s: (l, b, 0, 0)),
                  pl.BlockSpec((None, d, d), lambda b, s: (l, 0, 0)),
                  pl.BlockSpec((None, 1, d), lambda b, s: (4 * l + 2, 0, 0)),
                  pl.BlockSpec((None, 1, d), lambda b, s: (4 * l + 2, 0, 0))],
        out_specs=xrow,
        out_shape=jax.ShapeDtypeStruct((m, d), F32),
        compiler_params=_cparams(),
        name="xattn",
    )(xf, w_q, kv, w_o, ln_g, ln_b)


def _block_diag(w):
    nblk, bi, bj = w.shape[-3:]
    eye = jnp.eye(nblk, dtype=w.dtype)
    out = w[..., :, :, None, :] * eye[:, None, :, None]
    return out.reshape(w.shape[:-3] + (nblk * bi, nblk * bj))


def kernel(x, mem, ffn1_w_in, ffn1_w_out, ffn2_w_in, ffn2_w_out, ln_g, ln_b, xattn_w_q, xattn_w_kv, xattn_w_o, even_w_in, even_w_out, pool_w, pool_scale, sgu_ln_g, sgu_ln_b, sgu_w, sgu_b, odd_w_in, odd_w_out, rwkv_mu, rwkv_w0, rwkv_w_up, rwkv_a0, rwkv_a_up, rwkv_g_up, rwkv_k_k, rwkv_k_a, rwkv_r_k, rwkv_gn_g, rwkv_gn_b, lru_conv_w, lru_conv_b, lru_w_a, lru_b_a, lru_w_x, lru_b_x, lru_lambda):
    nb, seq, d = x.shape
    assert seq % WIDE_TILE == 0 and seq % XATTN_TILE == 0 and WIDE_TILE % ROW_TILE == 0 and ROW_TILE % SUB_ROWS == 0
    assert SUB_ROWS % SGU_CHUNK == 0 and ROW_TILE % (RWKV_CHUNK * RWKV_INTERLEAVE) == 0
    ns = seq // ROW_TILE
    nsw = seq // WIDE_TILE
    depth = ffn1_w_in.shape[0]
    xf = x.reshape(nb * seq, d)

    bf = lambda t: t.astype(BF16)
    row3 = lambda t: t.reshape(t.shape[0], 1, -1)
    lg = ln_g.reshape(-1, 1, d)
    lb = ln_b.reshape(-1, 1, d)
    f1_in, f1_out, f2_in, f2_out = ffn1_w_in, ffn1_w_out, ffn2_w_in, ffn2_w_out
    wq, wo = bf(xattn_w_q), bf(xattn_w_o)
    kv = _kv_proj(mem, bf(xattn_w_kv))
    ev_in, ev_out = bf(even_w_in), bf(even_w_out)
    od_in, od_out = bf(odd_w_in), bf(odd_w_out)
    n_e, n_pool, pool_dim = pool_w.shape[:3]
    pw = bf(_block_diag(pool_w.reshape(n_e, n_pool // POOL_PACK, POOL_PACK, pool_dim, pool_dim)))
    sgu_bt = jnp.swapaxes(sgu_b, 1, 2)
    wa_bd, wx_bd = bf(_block_diag(lru_w_a)), bf(_block_diag(lru_w_x))
    w_rwkv = rwkv_w0.shape[-1]
    ones_bd = _block_diag(jnp.ones((w_rwkv // RWKV_HEAD_DIM, RWKV_HEAD_DIM, RWKV_HEAD_DIM), BF16))
    r_k = rwkv_r_k.reshape(rwkv_r_k.shape[0], 1, -1)

    for l in range(depth):
        xf = _ffn_layer(xf, f1_in, f1_out, lg, lb, l, 0)
        e = l // 2
        if l % 2 == 0:
            xf = _even_layer(xf, ev_in, ev_out, pw, row3(pool_scale), row3(sgu_ln_g), row3(sgu_ln_b),
                             sgu_w, sgu_bt, lg, lb, e, l, nb, nsw)
        else:
            r, lw, k, v, a, b, g, bonus, yd = _odd_pre(
                xf, od_in, row3(rwkv_mu), row3(rwkv_w0), bf(rwkv_w_up), row3(rwkv_a0), bf(rwkv_a_up),
                bf(rwkv_g_up), row3(rwkv_k_k), row3(rwkv_k_a), r_k, lru_conv_w, row3(lru_conv_b),
                wa_bd, row3(lru_b_a), wx_bd, row3(lru_b_x), row3(lru_lambda), ones_bd, e, nb, ns)
            y = _rwkv_scan(r, lw, k, v, a, b, nb, ns)
            xf = _odd_post(xf, y, g, bonus, yd, row3(rwkv_gn_g), row3(rwkv_gn_b), ones_bd, od_out,
                           lg, lb, e, l, nb, nsw)
        xf = _xattn_layer(xf, wq, kv, wo, lg, lb, l, nb, seq // XATTN_TILE)
        xf = _ffn_layer(xf, f2_in, f2_out, lg, lb, l, 3)
    return xf.reshape(nb, seq, d)
```
